```python
import math
import jax, jax.numpy as jnp
from jax import lax
import numpy as np

D_MODEL = 2048
BATCH = 2
SEQ = 4096
DEPTH = 2
DEC_BATCH = 8
DEC_SEQ = 8
PAST_LEN = 16384
PAGE_SIZE = 128

HEAD_DIM = 64
W_A = D_MODEL // 2
H_A = W_A // HEAD_DIM
W_B = D_MODEL // 4
G_B = 8
CB = W_B // G_B
W_C = D_MODEL - W_A - W_B
SSM_GROUP = 16
G_C = W_C // SSM_GROUP
SSM_STATE = 64
IN_COLS = 3 * W_A + 2 * W_B + W_C
DILATIONS = ((128, 1), (512, 4), (2048, 16))
WIN_MAX = 2048
CHUNK = 128
N_BUCKETS = 32
REL_MAX_DIST = 2048
N_EXPERTS = 64
TOP_K = 6
D_EXPERT = 512
D_SHARED = 1024
ROUTE_SCALE = 2.5
MOE_BLOCK = 128
EPS = 1e-6
DT_MIN = 1e-3
DT_MAX = 1e-1

kernel_name = "hymba_dilated_gmlp_s5_moe_step"

F32 = jnp.float32


def _rmsnorm(x, g):
    xf = x.astype(F32)
    y = xf * lax.rsqrt(jnp.mean(xf * xf, axis=-1, keepdims=True) + EPS)
    return (y * g.astype(F32)).astype(x.dtype)


def _layernorm(x, g, b):
    xf = x.astype(F32)
    mu = jnp.mean(xf, axis=-1, keepdims=True)
    var = jnp.mean(jnp.square(xf - mu), axis=-1, keepdims=True)
    return ((xf - mu) * lax.rsqrt(var + EPS) * g.astype(F32) + b.astype(F32)).astype(x.dtype)


def _t5_bucket(dist):
    max_exact = N_BUCKETS // 2
    dist = jnp.maximum(dist, 0)
    ratio = jnp.log(jnp.maximum(dist, 1).astype(F32) / max_exact) / math.log(REL_MAX_DIST / max_exact)
    large = jnp.minimum(max_exact + (ratio * (N_BUCKETS - max_exact)).astype(jnp.int32), N_BUCKETS - 1)
    return jnp.where(dist < max_exact, dist, large)


def _dilated_branch_prompt(q, k, v, rel_bias, window, dil):
    b, t, h, e = q.shape
    nk = window // dil
    m = t // dil
    nb = -(-m // nk)
    mp = nb * nk

    def to_res(a):
        a = a.reshape(b, m, dil, h, e).transpose(0, 2, 1, 3, 4)
        a = jnp.pad(a, ((0, 0), (0, 0), (0, mp - m), (0, 0), (0, 0)))
        return a.reshape(b, dil, nb, nk, h, e)

    def with_prev(a):
        prev = jnp.pad(a, ((0, 0), (0, 0), (1, 0), (0, 0), (0, 0), (0, 0)))[:, :, :nb]
        return jnp.concatenate([prev, a], axis=3)

    qr = to_res(q)
    kb = with_prev(to_res(k))
    vb = with_prev(to_res(v))
    s = jnp.einsum("brnqhe,brnkhe->brnhqk", qr, kb, preferred_element_type=F32) * (e ** -0.5)
    qi = jnp.arange(nk)[:, None]
    kj = jnp.arange(2 * nk)[None, :]
    rel = qi + nk - kj
    band = (rel >= 0) & (rel <= nk)
    valid = band[None] & ~((jnp.arange(nb) == 0)[:, None, None] & (kj < nk)[None])
    bias = rel_bias.astype(F32)[_t5_bucket(rel * dil)].transpose(2, 0, 1)
    s = jnp.where(valid[None, None, :, None], s + bias, -jnp.inf)
    mx = jnp.max(s, axis=-1)
    p = jnp.exp(s - mx[..., None])
    den = jnp.sum(p, axis=-1)
    num = jnp.einsum("brnhqk,brnkhe->brnqhe", p, vb.astype(F32))

    def from_res(a):
        a = a.reshape((b, dil, mp) + a.shape[4:])[:, :, :m]
        a = jnp.moveaxis(a, 1, 2)
        return a.reshape((b, t) + a.shape[3:])

    return from_res(num), from_res(jnp.swapaxes(den, -1, -2)), from_res(jnp.swapaxes(mx, -1, -2))


def _dilated_branch_sample(q, kf, vf, rel_bias, window, dil, buf):
    s_len, e = q.shape[1], q.shape[-1]
    nk = window // dil
    steps = jnp.arange(nk + 1)
    idx = buf + jnp.arange(s_len)[:, None] - steps[None, :] * dil
    valid = idx >= 0
    idx_c = jnp.maximum(idx, 0)
    kg = jnp.take(kf, idx_c, axis=1)
    vg = jnp.take(vf, idx_c, axis=1)
    bias = rel_bias.astype(F32)[_t5_bucket(steps * dil)].T
    s = jnp.einsum("bqhe,bqkhe->bhqk", q, kg, preferred_element_type=F32) * (e ** -0.5) + bias[None, :, None, :]
    s = jnp.where(valid[None, None], s, -jnp.inf)
    mx = jnp.max(s, axis=-1)
    p = jnp.exp(s - mx[..., None])
    den = jnp.sum(p, axis=-1)
    num = jnp.einsum("bhqk,bqkhe->bqhe", p, vg.astype(F32))
    return num, jnp.swapaxes(den, 1, 2), jnp.swapaxes(mx, 1, 2)


def _combine_dilations(branches):
    mx = jnp.max(jnp.stack([m for _, _, m in branches]), axis=0)
    num = sum(jnp.exp(m - mx)[..., None] * n for n, _, m in branches)
    den = sum(jnp.exp(m - mx) * d for _, d, m in branches)
    return num / den[..., None]


def _chunk_spatial_gate(u, v, w_s, b_s):
    b, t, _ = v.shape
    nc = -(-t // CHUNK)
    tp = nc * CHUNK
    vp = jnp.pad(v, ((0, 0), (0, tp - t), (0, 0))).reshape(b, nc, CHUNK, G_B, CB)
    causal = jnp.tril(jnp.ones((CHUNK, CHUNK), dtype=bool))
    w = jnp.where(causal[None], w_s, 0)
    mix = jnp.einsum("gts,bnsgc->bntgc", w, vp) + b_s.T[None, None, :, :, None]
    return u * mix.reshape(b, tp, W_B)[:, :t]


def _ssm_combine(e1, e2):
    a1, b1 = e1
    a2, b2 = e2
    return a1 * a2, a2 * b1 + b2


def _s5(u, h0_re, h0_im, lam_re, lam_im, log_dt, b_re, b_im, c_re, c_im, d_skip, w_glu, b_glu):
    bsz, t, _ = u.shape
    uf = u.astype(F32).reshape(bsz, t, G_C, SSM_GROUP)
    lam = lax.complex(lam_re.astype(F32), lam_im.astype(F32))
    dt = jnp.exp(log_dt.astype(F32))[:, None]
    a_bar = jnp.exp(lam * dt)
    b_bar = ((a_bar - 1.0) / lam)[:, :, None] * lax.complex(b_re.astype(F32), b_im.astype(F32))
    bu = jnp.einsum("gph,btgh->btgp", b_bar, uf.astype(jnp.complex64))
    a_seq = jnp.broadcast_to(a_bar, bu.shape)
    a_cum, b_cum = lax.associative_scan(_ssm_combine, (a_seq, bu), axis=1)
    h = a_cum * lax.complex(h0_re.astype(F32), h0_im.astype(F32))[:, None] + b_cum
    cmat = lax.complex(c_re.astype(F32), c_im.astype(F32))
    y = jnp.real(jnp.einsum("ghp,btgp->btgh", cmat, h)) + d_skip.astype(F32) * uf
    z = jax.nn.gelu(y.reshape(bsz, t, W_C))
    out = z * jax.nn.sigmoid(z @ w_glu.astype(F32) + b_glu.astype(F32))
    return out.astype(u.dtype), jnp.real(h[:, -1]), jnp.imag(h[:, -1])


def _swiglu(x, wg, wu, wd):
    return (jax.nn.silu(x @ wg) * (x @ wu)) @ wd


def _routed_experts(hf, idx, wts, w_gate, w_up, w_down):
    n, d = hf.shape
    a = n * TOP_K
    n_blocks = -(-(a + N_EXPERTS * (MOE_BLOCK - 1)) // MOE_BLOCK)
    cap = n_blocks * MOE_BLOCK
    flat_e = idx.reshape(-1)
    flat_tok = jnp.arange(a) // TOP_K
    flat_w = wts.reshape(-1)
    order = jnp.argsort(flat_e, stable=True)
    se, st, sw = flat_e[order], flat_tok[order], flat_w[order]
    gs = jnp.bincount(flat_e, length=N_EXPERTS)
    starts = jnp.cumsum(gs) - gs
    padded = (gs + MOE_BLOCK - 1) // MOE_BLOCK * MOE_BLOCK
    pends = jnp.cumsum(padded)
    pstarts = pends - padded
    dest = pstarts[se] + (jnp.arange(a) - starts[se])
    row_tok = jnp.full((cap,), n, jnp.int32).at[dest].set(st.astype(jnp.int32))
    row_w = jnp.zeros((cap,), F32).at[dest].set(sw)
    block_e = jnp.minimum(jnp.searchsorted(pends, jnp.arange(n_blocks) * MOE_BLOCK, side="right"), N_EXPERTS - 1)
    hpad = jnp.concatenate([hf, jnp.zeros((1, d), hf.dtype)], axis=0)

    def block_fn(args):
        toks, e = args
        xb = hpad[toks]
        return _swiglu(xb, w_gate[e], w_up[e], w_down[e])

    outs = lax.map(block_fn, (row_tok.reshape(n_blocks, MOE_BLOCK), block_e)).reshape(cap, d)
    y = jnp.zeros((n + 1, d), F32).at[row_tok].add(outs.astype(F32) * row_w[:, None])
    return y[:n].astype(hf.dtype)


def _moe(h, lw):
    bsz, t, d = h.shape
    hf = h.reshape(bsz * t, d)
    scores = jax.nn.sigmoid((hf @ lw["w_router"]).astype(F32))
    _, idx = lax.top_k(scores + lw["b_router"].astype(F32), TOP_K)
    wts = jnp.take_along_axis(scores, idx, axis=1)
    wts = wts / jnp.sum(wts, axis=-1, keepdims=True) * ROUTE_SCALE
    routed = _routed_experts(hf, idx, wts, lw["w_gate"], lw["w_up"], lw["w_down"])
    shared = _swiglu(hf, lw["w_sh_gate"], lw["w_sh_up"], lw["w_sh_down"])
    return (routed + shared).reshape(bsz, t, d)


def _layer(x, c, lw, rel_bias, k_buf, v_buf, h0_re, h0_im):
    bsz, t, _ = x.shape
    mod = (jax.nn.silu(c.astype(F32)) @ lw["w_ada"].astype(F32) + lw["b_ada"].astype(F32)).astype(x.dtype)
    sh1, sc1, g1, sh2, sc2, g2 = jnp.split(mod[:, None, :], 6, axis=-1)
    h = _rmsnorm(x, lw["g_attn"]) * (1 + sc1) + sh1
    z = h @ lw["w_in"]
    q, k, v, ub, vb, uc = jnp.split(z, [W_A, 2 * W_A, 3 * W_A, 3 * W_A + W_B, 3 * W_A + 2 * W_B], axis=-1)
    q = q.reshape(bsz, t, H_A, HEAD_DIM)
    k = k.reshape(bsz, t, H_A, HEAD_DIM)
    v = v.reshape(bsz, t, H_A, HEAD_DIM)
    if k_buf is None:
        branches = [_dilated_branch_prompt(q, k, v, rel_bias, w, d) for w, d in DILATIONS]
        keep = min(WIN_MAX, t)
        new_k, new_v = k[:, t - keep:], v[:, t - keep:]
    else:
        buf = k_buf.shape[1]
        kf = jnp.concatenate([k_buf.astype(k.dtype), k], axis=1)
        vf = jnp.concatenate([v_buf.astype(v.dtype), v], axis=1)
        branches = [_dilated_branch_sample(q, kf, vf, rel_bias, w, d, buf) for w, d in DILATIONS]
        new_k, new_v = k, v
    a_out = _combine_dilations(branches).reshape(bsz, t, W_A).astype(x.dtype)
    vb = _layernorm(jax.nn.gelu(vb), lw["g_bv"], lw["b_bv"])
    b_out = _chunk_spatial_gate(jax.nn.gelu(ub), vb, lw["w_s"], lw["b_s"])
    c_out, h_re, h_im = _s5(uc, h0_re, h0_im, lw["lam_re"], lw["lam_im"], lw["log_dt"], lw["ssm_b_re"],
                            lw["ssm_b_im"], lw["ssm_c_re"], lw["ssm_c_im"], lw["ssm_d"], lw["w_glu"], lw["b_glu"])
    g_mix = lw["g_mix"]
    mixed = jnp.concatenate([_rmsnorm(a_out, g_mix[:W_A]),
                             _rmsnorm(b_out, g_mix[W_A:W_A + W_B]),
                             _rmsnorm(c_out, g_mix[W_A + W_B:])], axis=-1)
    x = x + g1 * (mixed @ lw["w_out"])
    h2 = _rmsnorm(x, lw["g_ffn"]) * (1 + sc2) + sh2
    x = x + g2 * _moe(h2, lw)
    return x, new_k, new_v, vb, h_re, h_im


def setup_inputs(seed: int = 0) -> dict:
    key = jax.random.key(seed)
    ks = iter(jax.random.split(key, 64))

    def nrm(shape, scale):
        return scale * jax.random.normal(next(ks), shape, F32)

    def gain(shape):
        return 1.0 + nrm(shape, 0.05)

    buf = min(WIN_MAX, PAST_LEN)
    n_idx = jnp.arange(SSM_STATE, dtype=F32)
    return {
        "x_prompt": nrm((BATCH, SEQ, D_MODEL), 1.0),
        "x_sample": nrm((DEC_BATCH, DEC_SEQ, D_MODEL), 1.0),
        "cache_a_k": nrm((DEPTH, DEC_BATCH, buf, H_A, HEAD_DIM), 1.0),
        "cache_a_v": nrm((DEPTH, DEC_BATCH, buf, H_A, HEAD_DIM), 1.0),
        "state_c_re": nrm((DEPTH, DEC_BATCH, G_C, SSM_STATE), 0.5),
        "state_c_im": nrm((DEPTH, DEC_BATCH, G_C, SSM_STATE), 0.5),
        "c_prompt": nrm((BATCH, D_MODEL), 1.0),
        "c_sample": nrm((DEC_BATCH, D_MODEL), 1.0),
        "rel_bias": nrm((N_BUCKETS, H_A), 0.5),
        "w_ada": nrm((DEPTH, D_MODEL, 6 * D_MODEL), 0.5 * D_MODEL ** -0.5),
        "b_ada": nrm((DEPTH, 6 * D_MODEL), 0.01),
        "g_attn": gain((DEPTH, D_MODEL)),
        "w_in": nrm((DEPTH, D_MODEL, IN_COLS), D_MODEL ** -0.5),
        "g_bv": gain((DEPTH, W_B)),
        "b_bv": nrm((DEPTH, W_B), 0.01),
        "w_s": nrm((DEPTH, G_B, CHUNK, CHUNK), CHUNK ** -0.5),
        "b_s": nrm((DEPTH, G_B, CHUNK), 0.1),
        "lam_re": -0.5 + nrm((DEPTH, G_C, SSM_STATE), 0.01),
        "lam_im": math.pi * n_idx + nrm((DEPTH, G_C, SSM_STATE), 0.01),
        "log_dt": jax.random.uniform(next(ks), (DEPTH, G_C), F32, math.log(DT_MIN), math.log(DT_MAX)),
        "ssm_b_re": nrm((DEPTH, G_C, SSM_STATE, SSM_GROUP), (2 * SSM_GROUP) ** -0.5),
        "ssm_b_im": nrm((DEPTH, G_C, SSM_STATE, SSM_GROUP), (2 * SSM_GROUP) ** -0.5),
        "ssm_c_re": nrm((DEPTH, G_C, SSM_GROUP, SSM_STATE), SSM_STATE ** -0.5),
        "ssm_c_im": nrm((DEPTH, G_C, SSM_GROUP, SSM_STATE), SSM_STATE ** -0.5),
        "ssm_d": nrm((DEPTH, G_C, SSM_GROUP), 0.5),
        "w_glu": nrm((DEPTH, W_C, W_C), W_C ** -0.5),
        "b_glu": nrm((DEPTH, W_C), 0.01),
        "g_mix": gain((DEPTH, D_MODEL)),
        "w_out": nrm((DEPTH, D_MODEL, D_MODEL), D_MODEL ** -0.5),
        "g_ffn": gain((DEPTH, D_MODEL)),
        "w_router": nrm((DEPTH, D_MODEL, N_EXPERTS), D_MODEL ** -0.5),
        "b_router": nrm((DEPTH, N_EXPERTS), 0.01),
        "w_gate": nrm((DEPTH, N_EXPERTS, D_MODEL, D_EXPERT), D_MODEL ** -0.5),
        "w_up": nrm((DEPTH, N_EXPERTS, D_MODEL, D_EXPERT), D_MODEL ** -0.5),
        "w_down": nrm((DEPTH, N_EXPERTS, D_EXPERT, D_MODEL), D_EXPERT ** -0.5),
        "w_sh_gate": nrm((DEPTH, D_MODEL, D_SHARED), D_MODEL ** -0.5),
        "w_sh_up": nrm((DEPTH, D_MODEL, D_SHARED), D_MODEL ** -0.5),
        "w_sh_down": nrm((DEPTH, D_SHARED, D_MODEL), D_SHARED ** -0.5),
        "g_final": gain((D_MODEL,)),
    }


def reference(x_prompt, x_sample, cache_a_k, cache_a_v, state_c_re, state_c_im, c_prompt, c_sample,
              rel_bias, w_ada, b_ada, g_attn, w_in, g_bv, b_bv, w_s, b_s, lam_re, lam_im, log_dt,
              ssm_b_re, ssm_b_im, ssm_c_re, ssm_c_im, ssm_d, w_glu, b_glu, g_mix, w_out, g_ffn,
              w_router, b_router, w_gate, w_up, w_down, w_sh_gate, w_sh_up, w_sh_down, g_final):
    def layer_weights(l):
        return dict(w_ada=w_ada[l], b_ada=b_ada[l], g_attn=g_attn[l], w_in=w_in[l], g_bv=g_bv[l],
                    b_bv=b_bv[l], w_s=w_s[l], b_s=b_s[l], lam_re=lam_re[l], lam_im=lam_im[l],
                    log_dt=log_dt[l], ssm_b_re=ssm_b_re[l], ssm_b_im=ssm_b_im[l], ssm_c_re=ssm_c_re[l],
                    ssm_c_im=ssm_c_im[l], ssm_d=ssm_d[l], w_glu=w_glu[l], b_glu=b_glu[l], g_mix=g_mix[l],
                    w_out=w_out[l], g_ffn=g_ffn[l], w_router=w_router[l], b_router=b_router[l],
                    w_gate=w_gate[l], w_up=w_up[l], w_down=w_down[l], w_sh_gate=w_sh_gate[l],
                    w_sh_up=w_sh_up[l], w_sh_down=w_sh_down[l])

    xp, xs = x_prompt, x_sample
    zeros_h = jnp.zeros((x_prompt.shape[0], G_C, SSM_STATE), F32)
    kp, vp, hrp, hip = [], [], [], []
    ks_, vs_, vbs, hrs, his = [], [], [], [], []
    for l in range(DEPTH):
        lw = layer_weights(l)
        xp, nk_, nv_, _, hr, hi = _layer(xp, c_prompt, lw, rel_bias, None, None, zeros_h, zeros_h)
        kp.append(nk_); vp.append(nv_); hrp.append(hr); hip.append(hi)
        xs, nk_, nv_, nvb, hr, hi = _layer(xs, c_sample, lw, rel_bias, cache_a_k[l], cache_a_v[l],
                                          state_c_re[l], state_c_im[l])
        ks_.append(nk_); vs_.append(nv_); vbs.append(nvb); hrs.append(hr); his.append(hi)
    y_prompt = _rmsnorm(xp, g_final)
    y_sample = _rmsnorm(xs, g_final)
    return (y_prompt, y_sample, jnp.stack(kp), jnp.stack(vp), jnp.stack(ks_), jnp.stack(vs_), jnp.stack(vbs),
            jnp.stack(hrp), jnp.stack(hip), jnp.stack(hrs), jnp.stack(his))
```

```python
import functools
import math

import jax
import jax.numpy as jnp
import numpy as np
from jax import lax
from jax.experimental import pallas as pl
from jax.experimental.pallas import tpu as pltpu

F32 = jnp.float32
BF16 = jnp.bfloat16
I32 = jnp.int32

D = 2048
B_P, T_P = 2, 4096
B_S, T_S = 8, 8
DEPTH = 2
HEAD_DIM = 64
W_A = 1024
H_A = 16
W_B = 512
G_B = 8
W_C = 512
SSM_GROUP = 16
G_C = 32
SSM_STATE = 64
IN_COLS = 3 * W_A + 2 * W_B + W_C
DILATIONS = (1, 4, 16)
NK = 128
BUF = 2048
CHUNK = 128
N_BUCKETS = 32
REL_MAX_DIST = 2048
N_EXPERTS = 64
TOP_K = 6
D_EXPERT = 512
D_SHARED = 1024
ROUTE_SCALE = 2.5
EPS = 1e-6

N_PROMPT = B_P * T_P
N_SAMPLE = B_S * T_S
N_REAL = N_PROMPT + N_SAMPLE
TM = 512
NT = N_PROMPT // TM + 1
NP = NT * TM
TD = 256
BM = 256
NB = -(-(N_REAL * TOP_K + N_EXPERTS * (BM - 1)) // BM)
NBP = -(-NB // 8) * 8
CAP = NB * BM
TC_ = 128
KF_ROWS = 2176
S5_L_P = 16
S5_L_S = T_S
BIG = 1e9

VMEM_LIMIT_BYTES = 56 * 1024 * 1024


def _params(dims):
    return pltpu.CompilerParams(dimension_semantics=dims, vmem_limit_bytes=VMEM_LIMIT_BYTES)


def _dot(a, b):
    return jnp.dot(a, b, preferred_element_type=F32)


def _dot_nt(a, b, precision=None):
    return lax.dot_general(a, b, (((1,), (1,)), ((), ())), preferred_element_type=F32,
                           precision=precision)


def _rms(x, g):
    return x * lax.rsqrt(jnp.mean(x * x, axis=-1, keepdims=True) + EPS) * g


def _mod_kernel(c_ref, w_ref, b_ref, o_ref):
    c = c_ref[...]
    s = (c * jax.nn.sigmoid(c)).astype(BF16)
    o_ref[...] = _dot(s, w_ref[...].astype(BF16)) + b_ref[...]


def _mod_all(c_all, w_ada, b_ada):
    tn = 1024
    return pl.pallas_call(
        _mod_kernel,
        grid=(DEPTH, 6 * D // tn),
        in_specs=[pl.BlockSpec((16, D), lambda l, j: (0, 0)),
                  pl.BlockSpec((None, D, tn), lambda l, j: (l, 0, j)),
                  pl.BlockSpec((None, 1, tn), lambda l, j: (l, 0, j))],
        out_specs=pl.BlockSpec((None, 16, tn), lambda l, j: (l, 0, j)),
        out_shape=jax.ShapeDtypeStruct((DEPTH, 16, 6 * D), F32),
        compiler_params=_params(("arbitrary", "arbitrary")),
        name="mod",
    )(c_all, w_ada, b_ada.reshape(DEPTH, 1, 6 * D))


def _tile_tabs(m):
    row = jnp.concatenate([jnp.repeat(m[0:B_P], NT // B_P, axis=0), jnp.zeros((1, m.shape[1]), F32)], 0)
    tok = jnp.concatenate([jnp.repeat(m[B_P:B_P + B_S], T_S, axis=0),
                           jnp.zeros((TM - N_SAMPLE, m.shape[1]), F32)], 0)
    return row[:, None, :], tok


def _pick(last, tok_ref, row_ref):
    return jnp.where(last, tok_ref[...], row_ref[...])


def _modnorm_kernel(x_ref, g_ref, scr_ref, sct_ref, shr_ref, sht_ref, *o_refs):
    last = pl.program_id(0) == NT - 1
    y = _rms(x_ref[...], g_ref[...])
    y = y * (1.0 + _pick(last, sct_ref, scr_ref)) + _pick(last, sht_ref, shr_ref)
    for o in o_refs:
        o[...] = y.astype(o.dtype)


def _modnorm(x, g, sc, sh, out_dtypes):
    row = pl.BlockSpec((None, 1, D), lambda i: (i, 0, 0))
    tok = pl.BlockSpec((TM, D), lambda i: (0, 0))
    til = pl.BlockSpec((TM, D), lambda i: (i, 0))
    return pl.pallas_call(
        _modnorm_kernel,
        grid=(NT,),
        in_specs=[til, pl.BlockSpec((1, D), lambda i: (0, 0)), row, tok, row, tok],
        out_specs=[til for _ in out_dtypes],
        out_shape=[jax.ShapeDtypeStruct((NP, D), dt) for dt in out_dtypes],
        compiler_params=_params(("arbitrary",)),
        name="modnorm",
    )(x, g.reshape(1, D), sc[0], sc[1], sh[0], sh[1])


def _finalnorm_kernel(x_ref, g_ref, o_ref):
    o_ref[...] = _rms(x_ref[...], g_ref[...])


def _finalnorm(x, g):
    til = pl.BlockSpec((TM, D), lambda i: (i, 0))
    return pl.pallas_call(
        _finalnorm_kernel,
        grid=(NT,),
        in_specs=[til, pl.BlockSpec((1, D), lambda i: (0, 0))],
        out_specs=til,
        out_shape=jax.ShapeDtypeStruct((NP, D), F32),
        compiler_params=_params(("arbitrary",)),
        name="finalnorm",
    )(x, g.reshape(1, D))


def _mm_kernel(a_ref, b_ref, o_ref):
    o_ref[...] = _dot(a_ref[...], b_ref[...])


def _matmul(a, b, tn):
    k, n = b.shape
    return pl.pallas_call(
        _mm_kernel,
        grid=(n // tn, NT),
        in_specs=[pl.BlockSpec((TM, k), lambda c, i: (i, 0)),
                  pl.BlockSpec((k, tn), lambda c, i: (0, c))],
        out_specs=pl.BlockSpec((TM, tn), lambda c, i: (i, c)),
        out_shape=jax.ShapeDtypeStruct((NP, n), F32),
        compiler_params=_params(("arbitrary", "arbitrary")),
        name="matmul",
    )(a, b)


def _mm_res_kernel(a_ref, b_ref, x_ref, gr_ref, gt_ref, o_ref):
    last = pl.program_id(1) == NT - 1
    o_ref[...] = x_ref[...] + _pick(last, gt_ref, gr_ref) * _dot(a_ref[...], b_ref[...])


def _matmul_residual(a, b, x, gate, tn):
    k, n = b.shape
    return pl.pallas_call(
        _mm_res_kernel,
        grid=(n // tn, NT),
        in_specs=[pl.BlockSpec((TM, k), lambda c, i: (i, 0)),
                  pl.BlockSpec((k, tn), lambda c, i: (0, c)),
                  pl.BlockSpec((TM, tn), lambda c, i: (i, c)),
                  pl.BlockSpec((None, 1, tn), lambda c, i: (i, 0, c)),
                  pl.BlockSpec((TM, tn), lambda c, i: (0, c))],
        out_specs=pl.BlockSpec((TM, tn), lambda c, i: (i, c)),
        out_shape=jax.ShapeDtypeStruct((NP, n), F32),
        compiler_params=_params(("arbitrary", "arbitrary")),
        name="matmul_residual",
    )(a, b, x, gate[0], gate[1])


def _t5_bucket(dist):
    max_exact = N_BUCKETS // 2
    dist = jnp.maximum(dist, 0)
    ratio = jnp.log(jnp.maximum(dist, 1).astype(F32) / max_exact) / math.log(REL_MAX_DIST / max_exact)
    large = jnp.minimum(max_exact + (ratio * (N_BUCKETS - max_exact)).astype(I32), N_BUCKETS - 1)
    return jnp.where(dist < max_exact, dist, large)


def _bias_kernel(rb_ref, bucket_ref, add_ref, o_ref):
    h = pl.program_id(1)
    bucket = bucket_ref[...]
    acc = jnp.full(bucket.shape, -jnp.inf, F32)
    for b in range(N_BUCKETS):
        acc = jnp.where(bucket == b, rb_ref[b * H_A + h], acc)
    o_ref[...] = acc + add_ref[...]


def _bias_bank(rel_bias, bucket, add):
    n, r, c = bucket.shape
    blk = pl.BlockSpec((None, r, c), lambda i, h, rb: (i, 0, 0))
    return pl.pallas_call(
        _bias_kernel,
        grid_spec=pltpu.PrefetchScalarGridSpec(
            num_scalar_prefetch=1, grid=(n, H_A), in_specs=[blk, blk],
            out_specs=pl.BlockSpec((None, None, r, c), lambda i, h, rb: (i, h, 0, 0))),
        out_shape=jax.ShapeDtypeStruct((n, H_A, r, c), F32),
        compiler_params=_params(("arbitrary", "arbitrary")),
        name="bias_bank",
    )(rel_bias.reshape(-1), bucket, add)


def _prompt_bias_tables():
    qi = jnp.arange(NK)[:, None]
    kj = jnp.arange(2 * NK)[None, :]
    rel = qi + NK - kj
    band = (rel >= 0) & (rel <= NK)
    bucket = jnp.stack([jnp.where(band, _t5_bucket(rel * d), -1) for d in DILATIONS]).astype(I32)
    return bucket, jnp.zeros(bucket.shape, F32)


def _sample_bias_tables():
    dist_np = BUF + np.arange(T_S)[:, None] - np.arange(KF_ROWS)[None, :]
    mult = np.zeros(dist_np.shape, np.float64)
    for d in DILATIONS:
        mult += (dist_np >= 0) & (dist_np % d == 0) & (dist_np // d <= NK)
    mult = np.where(np.arange(KF_ROWS)[None, :] < BUF + T_S, mult, 0.0)
    valid = mult > 0
    add = np.where(valid, np.log(np.maximum(mult, 1.0)), 0.0).astype(np.float32)
    bucket = jnp.where(jnp.asarray(valid), _t5_bucket(jnp.asarray(dist_np, I32)), -1).astype(I32)
    return bucket[None], jnp.asarray(add)[None]


def _attn_prompt_kernel(q_ref, k_ref, v_ref, bias_ref, o_ref, acc_ref, m_ref, l_ref):
    lane = lax.broadcasted_iota(I32, (NK, 128), 1)
    head0 = lane < HEAD_DIM
    prev_cols = lax.broadcasted_iota(I32, (NK, 2 * NK), 1) < NK
    n_units = T_P // NK
    for di, d in enumerate(DILATIONS):
        shift = int(math.log2(d))

        def unit(u, carry, di=di, d=d, shift=shift):
            r = u & (d - 1)
            n = u >> shift
            cur = r + d * NK * n
            prv = r + d * NK * jnp.maximum(n - 1, 0)

            def rows(ref, start):
                if d == 1:
                    return ref[pl.ds(start, NK), :]
                return ref[pl.ds(start, NK, stride=d), :]

            q = rows(q_ref, cur) * (HEAD_DIM ** -0.5)
            k2 = jnp.concatenate([rows(k_ref, prv), rows(k_ref, cur)], axis=0).astype(BF16)
            v2 = jnp.concatenate([rows(v_ref, prv), rows(v_ref, cur)], axis=0).astype(BF16)
            pen = jnp.where(n == 0, -jnp.inf, 0.0).astype(F32)
            if di > 0:
                m_old = rows(m_ref, cur)
                l_old = rows(l_ref, cur)
                a_old = rows(acc_ref, cur)
            pv, rs, mn = [], [], []
            for h in range(2):
                qh = jnp.where(head0 if h == 0 else ~head0, q, 0.0).astype(BF16)
                s = _dot_nt(qh, k2) + bias_ref[di, h] + jnp.where(prev_cols, pen, 0.0)
                m_h = jnp.max(s, axis=-1, keepdims=True)
                if di > 0:
                    m_h = jnp.maximum(m_h, m_old[:, h * HEAD_DIM:h * HEAD_DIM + 1])
                p = jnp.exp(s - m_h)
                rs.append(jnp.sum(p, axis=-1, keepdims=True))
                pv.append(_dot(p.astype(BF16), v2))
                mn.append(m_h)
            m_new = jnp.where(head0, mn[0], mn[1])
            l_new = jnp.where(head0, rs[0], rs[1])
            a_new = jnp.where(head0, pv[0], pv[1])
            if di > 0:
                alpha = jnp.exp(m_old - m_new)
                l_new = alpha * l_old + l_new
                a_new = alpha * a_old + a_new
            if di == len(DILATIONS) - 1:
                o_ref[pl.ds(cur, NK, stride=d), :] = a_new / l_new
            elif d == 1:
                m_ref[pl.ds(cur, NK), :] = m_new
                l_ref[pl.ds(cur, NK), :] = l_new
                acc_ref[pl.ds(cur, NK), :] = a_new
            else:
                m_ref[pl.ds(cur, NK, stride=d), :] = m_new
                l_ref[pl.ds(cur, NK, stride=d), :] = l_new
                acc_ref[pl.ds(cur, NK, stride=d), :] = a_new
            return carry

        lax.fori_loop(0, n_units, unit, 0)


def _attn_prompt(z, bias):
    blk = lambda off: pl.BlockSpec((T_P, 128), lambda b, h, off=off: (b, off + h))
    return pl.pallas_call(
        _attn_prompt_kernel,
        grid=(B_P, H_A // 2),
        in_specs=[blk(0), blk(W_A // 128), blk(2 * W_A // 128),
                  pl.BlockSpec((len(DILATIONS), 2, NK, 2 * NK), lambda b, h: (0, h, 0, 0))],
        out_specs=pl.BlockSpec((T_P, 128), lambda b, h: (b, h)),
        out_shape=jax.ShapeDtypeStruct((N_PROMPT, W_A), F32),
        scratch_shapes=[pltpu.VMEM((T_P, 128), F32)] * 3,
        compiler_params=_params(("arbitrary", "arbitrary")),
        name="attn_prompt",
    )(z, z, z, bias)


def _attn_sample_kernel(q_ref, kn_ref, vn_ref, ck_ref, cv_ref, bias_ref, o_ref, kf_ref, vf_ref):
    for f_ref, c_ref, n_ref in ((kf_ref, ck_ref, kn_ref), (vf_ref, cv_ref, vn_ref)):
        f_ref[0:BUF, :] = c_ref[...]
        f_ref[BUF:BUF + T_S, :] = n_ref[...]
        f_ref[BUF + T_S:KF_ROWS, :] = jnp.zeros((KF_ROWS - BUF - T_S, 128), F32)
    head0 = lax.broadcasted_iota(I32, (T_S, 128), 1) < HEAD_DIM
    q = q_ref[...] * (HEAD_DIM ** -0.5)
    q2 = jnp.concatenate([jnp.where(head0, q, 0.0), jnp.where(head0, 0.0, q)], axis=0).astype(BF16)
    bias2 = jnp.concatenate([bias_ref[0], bias_ref[1]], axis=0)
    s = _dot_nt(q2, kf_ref[...].astype(BF16)) + bias2
    m = jnp.max(s, axis=-1, keepdims=True)
    p = jnp.exp(s - m)
    l = jnp.sum(p, axis=-1, keepdims=True)
    o2 = _dot(p.astype(BF16), vf_ref[...].astype(BF16)) / l
    o_ref[...] = jnp.where(head0, o2[0:T_S], o2[T_S:2 * T_S])


def _attn_sample(z, cache_k, cache_v, layer, bias):
    row0 = N_PROMPT // T_S
    blk = lambda off: pl.BlockSpec((T_S, 128), lambda b, h, off=off: (row0 + b, off + h))
    cache = pl.BlockSpec((None, BUF, 128), lambda b, h: (layer * B_S + b, 0, h))
    return pl.pallas_call(
        _attn_sample_kernel,
        grid=(B_S, H_A // 2),
        in_specs=[blk(0), blk(W_A // 128), blk(2 * W_A // 128), cache, cache,
                  pl.BlockSpec((2, T_S, KF_ROWS), lambda b, h: (h, 0, 0))],
        out_specs=pl.BlockSpec((T_S, 128), lambda b, h: (b, h)),
        out_shape=jax.ShapeDtypeStruct((N_SAMPLE, W_A), F32),
        scratch_shapes=[pltpu.VMEM((KF_ROWS, 128), F32)] * 2,
        compiler_params=_params(("arbitrary", "arbitrary")),
        name="attn_sample",
    )(z, z, z, cache_k, cache_v, bias)


def _gmlp_kernel(ub_ref, vb_ref, gbv_ref, bbv_ref, ws_ref, bs_ref, gmix_ref, bo_ref, vn_ref):
    v = jax.nn.gelu(vb_ref[...])
    mu = jnp.mean(v, axis=-1, keepdims=True)
    var = jnp.mean(jnp.square(v - mu), axis=-1, keepdims=True)
    vn = (v - mu) * lax.rsqrt(var + EPS) * gbv_ref[...] + bbv_ref[...]
    vn_ref[...] = vn
    vnb = vn.astype(BF16)
    causal = (lax.broadcasted_iota(I32, (CHUNK, CHUNK), 1) <= lax.broadcasted_iota(I32, (CHUNK, CHUNK), 0))
    first = lax.broadcasted_iota(I32, (CHUNK, 128), 1) < W_B // G_B
    parts = []
    for gp in range(G_B // 2):
        v2 = vnb[:, gp * 128:(gp + 1) * 128]
        m0 = _dot(jnp.where(causal, ws_ref[2 * gp], 0.0).astype(BF16), v2)
        m1 = _dot(jnp.where(causal, ws_ref[2 * gp + 1], 0.0).astype(BF16), v2)
        parts.append(jnp.where(first, m0, m1))
    mix = jnp.concatenate(parts, axis=1) + bs_ref[...]
    bo_ref[...] = _rms(jax.nn.gelu(ub_ref[...]) * mix, gmix_ref[...])


def _gmlp(z, n_chunks, col_u, col_v, g_bv, b_bv, w_s, b_s, g_mix_b):
    bs_wide = jnp.repeat(b_s.T, W_B // G_B, axis=1)
    vec = pl.BlockSpec((1, W_B), lambda i: (0, 0))
    out = pl.BlockSpec((CHUNK, W_B), lambda i: (i, 0))
    return pl.pallas_call(
        _gmlp_kernel,
        grid=(n_chunks,),
        in_specs=[pl.BlockSpec((CHUNK, W_B), lambda i: (i, col_u)),
                  pl.BlockSpec((CHUNK, W_B), lambda i: (i, col_v)),
                  vec, vec,
                  pl.BlockSpec((G_B, CHUNK, CHUNK), lambda i: (0, 0, 0)),
                  pl.BlockSpec((CHUNK, W_B), lambda i: (0, 0)),
                  vec],
        out_specs=[out, out],
        out_shape=[jax.ShapeDtypeStruct((n_chunks * CHUNK, W_B), F32)] * 2,
        compiler_params=_params(("arbitrary",)),
        name="gmlp",
    )(z, z, g_bv.reshape(1, W_B), b_bv.reshape(1, W_B), w_s, bs_wide, g_mix_b.reshape(1, W_B))


def _cmul(a, b):
    return a[0] * b[0] - a[1] * b[1], a[0] * b[1] + a[1] * b[0]


def _s5_param_kernel(lr_ref, li_ref, ldt_ref, btr_ref, bti_ref, cr_ref, ci_ref,
                     tz_ref, minr_ref, mini_ref, cor_ref, coi_ref, al_ref, *, L):
    hi = lax.Precision.HIGHEST
    lr, li = lr_ref[...], li_ref[...]
    dt = jnp.exp(ldt_ref[...])
    mag = jnp.exp(lr * dt)
    a = (mag * jnp.cos(li * dt), mag * jnp.sin(li * dt))
    den = lr * lr + li * li
    qr = ((a[0] - 1.0) * lr + a[1] * li) / den
    qi = (a[1] * lr - (a[0] - 1.0) * li) / den
    bb = _cmul((qr, qi), (btr_ref[...], bti_ref[...]))
    pw = [(jnp.ones_like(lr), jnp.zeros_like(lr))]
    for _ in range(L):
        pw.append(_cmul(pw[-1], a))
    c = (cr_ref[...], ci_ref[...])
    ck = [_cmul(c, p) for p in pw]
    n = SSM_GROUP * L
    cstack = [jnp.concatenate([x[j] for x in ck[0:L]], axis=0) for j in range(2)]
    btile = [jnp.concatenate([bb[j]] * L, axis=0) for j in range(2)]
    kw = _dot_nt(cstack[0], btile[0], hi) - _dot_nt(cstack[1], btile[1], hi)
    colb = lax.broadcasted_iota(I32, (n, n), 1) >> int(math.log2(SSM_GROUP))
    tz = jnp.zeros((n, n), F32)
    for s in range(L):
        sh = kw if s == 0 else jnp.concatenate([jnp.zeros((SSM_GROUP * s, n), F32), kw[:n - SSM_GROUP * s]], axis=0)
        tz = jnp.where(colb == s, sh, tz)
    tz_ref[...] = tz
    mins = [_cmul(pw[L - 1 - s], bb) for s in range(L)]
    minr_ref[...] = jnp.concatenate([x[0] for x in mins], axis=0)
    mini_ref[...] = jnp.concatenate([x[1] for x in mins], axis=0)
    cor_ref[...] = jnp.concatenate([x[0] for x in ck[1:L + 1]], axis=0)
    coi_ref[...] = jnp.concatenate([x[1] for x in ck[1:L + 1]], axis=0)
    al_ref[...] = jnp.concatenate([pw[L][0], pw[L][1]], axis=0)


def _s5_params(lam_re, lam_im, log_dt, b_re, b_im, c_re, c_im, L):
    n = SSM_GROUP * L
    p = SSM_STATE
    vec = pl.BlockSpec((None, 1, p), lambda g: (g, 0, 0))
    mat = pl.BlockSpec((None, SSM_GROUP, p), lambda g: (g, 0, 0))
    big = pl.BlockSpec((None, n, p), lambda g: (g, 0, 0))
    return pl.pallas_call(
        functools.partial(_s5_param_kernel, L=L),
        grid=(G_C,),
        in_specs=[vec, vec, vec, mat, mat, mat, mat],
        out_specs=[pl.BlockSpec((None, n, n), lambda g: (g, 0, 0)), big, big, big, big,
                   pl.BlockSpec((None, 2, p), lambda g: (g, 0, 0))],
        out_shape=[jax.ShapeDtypeStruct((G_C, n, n), F32)] + [jax.ShapeDtypeStruct((G_C, n, p), F32)] * 4
                  + [jax.ShapeDtypeStruct((G_C, 2, p), F32)],
        compiler_params=_params(("arbitrary",)),
        name="s5_params",
    )(lam_re.reshape(G_C, 1, p), lam_im.reshape(G_C, 1, p),
      jnp.broadcast_to(log_dt[:, None, None], (G_C, 1, p)),
      b_re.transpose(0, 2, 1), b_im.transpose(0, 2, 1), c_re, c_im)


def _s5_main_kernel(u_ref, tz_ref, minr_ref, mini_ref, cor_ref, coi_ref, al_ref, d_ref, h0r_ref, h0i_ref,
                    y_ref, hr_ref, hi_ref, er_ref, ei_ref, gr_ref, gi_ref, *, nj, nsc, nb):
    u = u_ref[...]
    ub = u.astype(BF16)
    y = _dot_nt(ub, tz_ref[...].astype(BF16)) + d_ref[...] * u
    s = (_dot(ub, minr_ref[...].astype(BF16)), _dot(ub, mini_ref[...].astype(BF16)))
    al = (al_ref[0:1, :], al_ref[1:2, :])
    r = nsc * nb
    xl = [(s[0][0:r], s[1][0:r])]
    for j in range(1, nj):
        t = _cmul(al, xl[-1])
        xl.append((t[0] + s[0][j * r:(j + 1) * r], t[1] + s[1][j * r:(j + 1) * r]))
    alp = [al]
    for j in range(1, nj):
        alp.append(_cmul(alp[-1], al))
    er_ref[...] = xl[-1][0]
    ei_ref[...] = xl[-1][1]
    g = (h0r_ref[...], h0i_ref[...])
    for sc in range(nsc):
        gr_ref[sc * nb:(sc + 1) * nb, :] = g[0]
        gi_ref[sc * nb:(sc + 1) * nb, :] = g[1]
        t = _cmul(alp[nj - 1], g)
        g = (t[0] + er_ref[sc * nb:(sc + 1) * nb, :], t[1] + ei_ref[sc * nb:(sc + 1) * nb, :])
    gcat = (gr_ref[...], gi_ref[...])
    hprev = [gcat]
    for j in range(1, nj):
        t = _cmul(alp[j - 1], gcat)
        hprev.append((xl[j - 1][0] + t[0], xl[j - 1][1] + t[1]))
    hpr = jnp.concatenate([x[0] for x in hprev], axis=0).astype(BF16)
    hpi = jnp.concatenate([x[1] for x in hprev], axis=0).astype(BF16)
    y_ref[...] = y + _dot_nt(hpr, cor_ref[...].astype(BF16)) - _dot_nt(hpi, coi_ref[...].astype(BF16))
    hr_ref[...] = g[0]
    hi_ref[...] = g[1]


def _s5_main(u, params, d_tile, h0r, h0i, nj, nsc, nb):
    tz, minr, mini, cor, coi, al = params
    m, n = u.shape[1], u.shape[2]
    p = SSM_STATE
    big = pl.BlockSpec((None, n, p), lambda g: (g, 0, 0))
    st = pl.BlockSpec((None, nb, p), lambda g: (g, 0, 0))
    return pl.pallas_call(
        functools.partial(_s5_main_kernel, nj=nj, nsc=nsc, nb=nb),
        grid=(G_C,),
        in_specs=[pl.BlockSpec((None, m, n), lambda g: (g, 0, 0)),
                  pl.BlockSpec((None, n, n), lambda g: (g, 0, 0)), big, big, big, big,
                  pl.BlockSpec((None, 2, p), lambda g: (g, 0, 0)),
                  pl.BlockSpec((None, 1, n), lambda g: (g, 0, 0)), st, st],
        out_specs=[pl.BlockSpec((None, m, n), lambda g: (g, 0, 0)), st, st],
        out_shape=[jax.ShapeDtypeStruct((G_C, m, n), F32)] + [jax.ShapeDtypeStruct((G_C, nb, p), F32)] * 2,
        scratch_shapes=[pltpu.VMEM((nsc * nb, p), F32)] * 4,
        compiler_params=_params(("arbitrary",)),
        name="s5_main",
    )(u, tz, minr, mini, cor, coi, al, d_tile, h0r, h0i)


def _s5_post_kernel(y_ref, w_ref, b_ref, g_ref, o_ref):
    zz = jax.nn.gelu(y_ref[...])
    out = zz * jax.nn.sigmoid(_dot(zz.astype(BF16), w_ref[...]) + b_ref[...])
    o_ref[...] = _rms(out, g_ref[...])


def _s5_post(y, w_glu, b_glu, g_mix_c):
    n = y.shape[0]
    tt = min(n, TM)
    vec = pl.BlockSpec((1, W_C), lambda i: (0, 0))
    return pl.pallas_call(
        _s5_post_kernel,
        grid=(n // tt,),
        in_specs=[pl.BlockSpec((tt, W_C), lambda i: (i, 0)), pl.BlockSpec((W_C, W_C), lambda i: (0, 0)), vec, vec],
        out_specs=pl.BlockSpec((tt, W_C), lambda i: (i, 0)),
        out_shape=jax.ShapeDtypeStruct((n, W_C), F32),
        compiler_params=_params(("arbitrary",)),
        name="s5_post",
    )(y, w_glu.astype(BF16), b_glu.reshape(1, W_C), g_mix_c.reshape(1, W_C))


def _mix_kernel(ap_ref, bp_ref, cp_ref, as_ref, bs_ref, cs_ref, g_ref, o_ref):
    def emit(a_ref, b_ref, c_ref):
        o_ref[:, 0:W_A] = _rms(a_ref[...], g_ref[...]).astype(BF16)
        o_ref[:, W_A:W_A + W_B] = b_ref[...].astype(BF16)
        o_ref[:, W_A + W_B:D] = c_ref[...].astype(BF16)

    last = pl.program_id(0) == NT - 1

    @pl.when(jnp.logical_not(last))
    def _():
        emit(ap_ref, bp_ref, cp_ref)

    @pl.when(last)
    def _():
        emit(as_ref, bs_ref, cs_ref)


def _mix(a_p, b_p, c_p, a_s, b_s, c_s, g_mix_a):
    pr = lambda w: pl.BlockSpec((TM, w), lambda i: (jnp.minimum(i, NT - 2), 0))
    sm = lambda w: pl.BlockSpec((TM, w), lambda i: (0, 0))
    return pl.pallas_call(
        _mix_kernel,
        grid=(NT,),
        in_specs=[pr(W_A), pr(W_B), pr(W_C), sm(W_A), sm(W_B), sm(W_C), pl.BlockSpec((1, W_A), lambda i: (0, 0))],
        out_specs=pl.BlockSpec((TM, D), lambda i: (i, 0)),
        out_shape=jax.ShapeDtypeStruct((NP, D), BF16),
        compiler_params=_params(("arbitrary",)),
        name="mix",
    )(a_p, b_p, c_p, a_s, b_s, c_s, g_mix_a.reshape(1, W_A))


def _router_kernel(h_ref, w_ref, b_ref, ind_ref, wd_ref, cnt_ref):
    i = pl.program_id(0)
    scores = jax.nn.sigmoid(_dot(h_ref[...], w_ref[...]))
    lane = lax.broadcasted_iota(I32, (TD, N_EXPERTS), 1).astype(F32)
    row = lax.broadcasted_iota(I32, (TD, N_EXPERTS), 0) + i * TD
    sel = scores + b_ref[...]
    ind = jnp.zeros((TD, N_EXPERTS), F32)
    for _ in range(TOP_K):
        m = jnp.max(sel, axis=-1, keepdims=True)
        idx = jnp.min(jnp.where(sel == m, lane, float(N_EXPERTS)), axis=-1, keepdims=True)
        hit = lane == idx
        ind = jnp.where(hit, 1.0, ind)
        sel = jnp.where(hit, -jnp.inf, sel)
    ind = jnp.where(row < N_REAL, ind, 0.0)
    wsel = ind * scores
    wd_ref[...] = wsel / jnp.sum(wsel + (1.0 - jnp.max(ind, axis=-1, keepdims=True)) / N_EXPERTS,
                                 axis=-1, keepdims=True) * ROUTE_SCALE
    ind_ref[...] = ind

    @pl.when(i == 0)
    def _():
        cnt_ref[...] = jnp.zeros((8, N_EXPERTS), F32)

    cnt_ref[...] += jnp.sum(ind, axis=0, keepdims=True)


def _router(h2b, w_router, b_router):
    til = pl.BlockSpec((TD, N_EXPERTS), lambda i: (i, 0))
    return pl.pallas_call(
        _router_kernel,
        grid=(NP // TD,),
        in_specs=[pl.BlockSpec((TD, D), lambda i: (i, 0)), pl.BlockSpec((D, N_EXPERTS), lambda i: (0, 0)),
                  pl.BlockSpec((1, N_EXPERTS), lambda i: (0, 0))],
        out_specs=[til, til, pl.BlockSpec((8, N_EXPERTS), lambda i: (0, 0))],
        out_shape=[jax.ShapeDtypeStruct((NP, N_EXPERTS), F32)] * 2 + [jax.ShapeDtypeStruct((8, N_EXPERTS), F32)],
        compiler_params=_params(("arbitrary",)),
        name="router",
    )(h2b, w_router.astype(BF16), b_router.reshape(1, N_EXPERTS))


def _group_starts(cnt):
    padded = jnp.floor((cnt + (BM - 1)) / BM) * BM
    tri = (lax.broadcasted_iota(I32, (N_EXPERTS, N_EXPERTS), 0)
           <= lax.broadcasted_iota(I32, (N_EXPERTS, N_EXPERTS), 1)).astype(F32)
    pend = jnp.dot(padded, tri, preferred_element_type=F32, precision=lax.Precision.HIGHEST)
    return padded, pend - padded, pend


def _plan_kernel(cnt_ref, be_ref, misc_ref):
    cnt = cnt_ref[...]
    _, pstart, pend = _group_starts(cnt)
    rowi = (lax.broadcasted_iota(I32, (NBP, N_EXPERTS), 0) * BM).astype(F32)
    be = jnp.minimum(jnp.sum(jnp.where(pend[0:1] <= rowi, 1.0, 0.0), axis=-1, keepdims=True), N_EXPERTS - 1.0)
    be_ref[...] = jnp.broadcast_to(be, (NBP, 128)).astype(I32)
    misc_ref[0:8, :] = pstart.astype(I32)
    misc_ref[8:16, :] = cnt.astype(I32)
    misc_ref[16:24, :] = (pend / BM).astype(I32)


def _plan(cnt):
    return pl.pallas_call(
        _plan_kernel,
        out_shape=[jax.ShapeDtypeStruct((NBP, 128), I32), jax.ShapeDtypeStruct((24, N_EXPERTS), I32)],
        name="moe_plan",
    )(cnt)


def _slots_kernel(ind_ref, wd_ref, cnt_ref, dest_ref, w_ref, run_ref):
    @pl.when(pl.program_id(0) == 0)
    def _():
        run_ref[...] = jnp.zeros((8, N_EXPERTS), F32)

    ind = ind_ref[...]
    _, pstart, _ = _group_starts(cnt_ref[...])
    ltri = (lax.broadcasted_iota(I32, (TD, TD), 1) < lax.broadcasted_iota(I32, (TD, TD), 0)).astype(BF16)
    pos = _dot(ltri, ind.astype(BF16)) + run_ref[0:1, :]
    run_ref[...] += jnp.sum(ind, axis=0, keepdims=True)
    cur = jnp.where(ind > 0.0, pstart[0:1] + pos, BIG)
    lane = lax.broadcasted_iota(I32, (TD, 128), 1)
    dest = jnp.full((TD, 128), -1.0, F32)
    wts = jnp.zeros((TD, 128), F32)
    wd = wd_ref[...]
    for k in range(TOP_K):
        m = jnp.min(cur, axis=-1, keepdims=True)
        hit = cur == m
        wk = jnp.sum(jnp.where(hit, wd, 0.0), axis=-1, keepdims=True)
        dest = jnp.where(lane == k, jnp.where(m < BIG, m, -1.0), dest)
        wts = jnp.where(lane == k, jnp.where(m < BIG, wk, 0.0), wts)
        cur = jnp.where(hit, BIG, cur)
    dest_ref[...] = dest.astype(I32)
    w_ref[...] = wts


def _slots(ind, wd, cnt):
    til = pl.BlockSpec((TD, N_EXPERTS), lambda i: (i, 0))
    out = pl.BlockSpec((TD, 128), lambda i: (i, 0))
    return pl.pallas_call(
        _slots_kernel,
        grid=(NP // TD,),
        in_specs=[til, til, pl.BlockSpec((8, N_EXPERTS), lambda i: (0, 0))],
        out_specs=[out, out],
        out_shape=[jax.ShapeDtypeStruct((NP, 128), I32), jax.ShapeDtypeStruct((NP, 128), F32)],
        scratch_shapes=[pltpu.VMEM((8, N_EXPERTS), F32)],
        compiler_params=_params(("arbitrary",)),
        name="moe_slots",
    )(ind, wd, cnt)


def _dispatch_kernel(ps_ref, cnt_ref, dest_ref, x_ref, xs_ref, zrow_ref, sem):
    i = pl.program_id(0)
    n_tiles = NP // TD

    def row_copy(r, d):
        return pltpu.make_async_copy(x_ref.at[pl.ds(r, 1), :], xs_ref.at[pl.ds(d, 1), :], sem)

    def zero_copy(d):
        return pltpu.make_async_copy(zrow_ref.at[pl.ds(0, 1), :], xs_ref.at[pl.ds(d, 1), :], sem)

    @pl.when(i < n_tiles)
    def _():
        def issue(r, c):
            for k in range(TOP_K):
                d = dest_ref[r * 8 + k]

                @pl.when(d >= 0)
                def _():
                    row_copy(r, d).start()
            return c

        def drain(r, c):
            for k in range(TOP_K):
                d = dest_ref[r * 8 + k]

                @pl.when(d >= 0)
                def _():
                    row_copy(r, d).wait()
            return c

        lax.fori_loop(0, TD, issue, 0)
        lax.fori_loop(0, TD, drain, 0)

    @pl.when(i == n_tiles)
    def _():
        zrow_ref[...] = jnp.zeros(zrow_ref.shape, F32)

        def per_expert(e, c):
            cnt = cnt_ref[e]
            base = ps_ref[e]
            end = ((cnt + (BM - 1)) >> 8) << 8
            lax.fori_loop(cnt, end, lambda j, c2: (zero_copy(base + j).start(), c2)[1], 0)
            lax.fori_loop(cnt, end, lambda j, c2: (zero_copy(base + j).wait(), c2)[1], 0)
            return c

        lax.fori_loop(0, N_EXPERTS, per_expert, 0)


def _dispatch(pstart_i, cnt_i, dest_flat, x):
    n_tiles = NP // TD
    assert BM == 256
    return pl.pallas_call(
        _dispatch_kernel,
        grid_spec=pltpu.PrefetchScalarGridSpec(
            num_scalar_prefetch=2, grid=(n_tiles + 1,),
            in_specs=[pl.BlockSpec((TD * 8,), lambda i, ps, cn: (jnp.minimum(i, n_tiles - 1),),
                                   memory_space=pltpu.SMEM),
                      pl.BlockSpec((TD, D), lambda i, ps, cn: (jnp.minimum(i, n_tiles - 1), 0))],
            out_specs=pl.BlockSpec(memory_space=pl.ANY),
            scratch_shapes=[pltpu.VMEM((8, D), F32), pltpu.SemaphoreType.DMA(())]),
        out_shape=jax.ShapeDtypeStruct((CAP, D), F32),
        compiler_params=_params(("arbitrary",)),
        name="moe_dispatch",
    )(pstart_i, cnt_i, dest_flat, x)


def _expert_kernel(be_ref, nbu_ref, x_ref, wg_ref, wu_ref, wd_ref, o_ref):
    i = pl.program_id(0)

    @pl.when(i < nbu_ref[0])
    def _():
        x = x_ref[...].astype(BF16)
        g = _dot(x, wg_ref[...].astype(BF16))
        u = _dot(x, wu_ref[...].astype(BF16))
        a = (g * jax.nn.sigmoid(g) * u).astype(BF16)
        o_ref[...] = _dot(a, wd_ref[...].astype(BF16))


def _experts(layer, be, nbu, xs, w_gate, w_up, w_down):
    blk = lambda i, be, nbu: (jnp.minimum(i, nbu[0] - 1), 0)
    wsel = lambda i, be, nbu: (layer, be[jnp.minimum(i, nbu[0] - 1)], 0, 0)
    return pl.pallas_call(
        _expert_kernel,
        grid_spec=pltpu.PrefetchScalarGridSpec(
            num_scalar_prefetch=2, grid=(NB,),
            in_specs=[pl.BlockSpec((BM, D), blk),
                      pl.BlockSpec((None, None, D, D_EXPERT), wsel),
                      pl.BlockSpec((None, None, D, D_EXPERT), wsel),
                      pl.BlockSpec((None, None, D_EXPERT, D), wsel)],
            out_specs=pl.BlockSpec((BM, D), blk)),
        out_shape=jax.ShapeDtypeStruct((CAP, D), F32),
        compiler_params=_params(("arbitrary",)),
        name="moe_experts",
    )(be, nbu, xs, w_gate, w_up, w_down)


def _shared_kernel(h_ref, wg_ref, wu_ref, wd_ref, o_ref):
    h = h_ref[...]
    g = _dot(h, wg_ref[...])
    u = _dot(h, wu_ref[...])
    o_ref[...] = _dot((g * jax.nn.sigmoid(g) * u).astype(BF16), wd_ref[...])


def _shared(h2b, wg, wu, wd):
    return pl.pallas_call(
        _shared_kernel,
        grid=(NT,),
        in_specs=[pl.BlockSpec((TM, D), lambda i: (i, 0)),
                  pl.BlockSpec((D, D_SHARED), lambda i: (0, 0)),
                  pl.BlockSpec((D, D_SHARED), lambda i: (0, 0)),
                  pl.BlockSpec((D_SHARED, D), lambda i: (0, 0))],
        out_specs=pl.BlockSpec((TM, D), lambda i: (i, 0)),
        out_shape=jax.ShapeDtypeStruct((NP, D), F32),
        compiler_params=_params(("arbitrary",)),
        name="moe_shared",
    )(h2b, wg.astype(BF16), wu.astype(BF16), wd.astype(BF16))


def _combine_kernel(dest_ref, w_ref, ys_ref, sh_ref, x_ref, gr_ref, gt_ref, o_ref, rows_ref, sem):
    i = pl.program_id(0)

    def row_copy(r, k):
        d = jnp.maximum(dest_ref[r * 8 + k], 0)
        return pltpu.make_async_copy(ys_ref.at[pl.ds(d, 1), :], rows_ref.at[k, pl.ds(r, 1), :], sem)

    def issue(r, c):
        for k in range(TOP_K):
            row_copy(r, k).start()
        return c

    def drain(r, c):
        for k in range(TOP_K):
            row_copy(r, k).wait()
        return c

    lax.fori_loop(0, TC_, issue, 0)
    lax.fori_loop(0, TC_, drain, 0)
    w = w_ref[...]
    routed = w[:, 0:1] * rows_ref[0]
    for k in range(1, TOP_K):
        routed = routed + w[:, k:k + 1] * rows_ref[k]
    last = i >= N_PROMPT // TC_
    o_ref[...] = x_ref[...] + _pick(last, gt_ref, gr_ref) * (routed + sh_ref[...])


def _combine(dest_flat, w6, ys, shared, x, gate):
    n_p = N_PROMPT // TC_
    per = TM // TC_
    til = pl.BlockSpec((TC_, D), lambda i: (i, 0))
    return pl.pallas_call(
        _combine_kernel,
        grid=(NP // TC_,),
        in_specs=[pl.BlockSpec((TC_ * 8,), lambda i: (i,), memory_space=pltpu.SMEM),
                  pl.BlockSpec((TC_, 128), lambda i: (i, 0)),
                  pl.BlockSpec(memory_space=pl.ANY), til, til,
                  pl.BlockSpec((None, 1, D), lambda i: (i // per, 0, 0)),
                  pl.BlockSpec((TC_, D), lambda i: (jnp.maximum(i - n_p, 0), 0))],
        out_specs=til,
        out_shape=jax.ShapeDtypeStruct((NP, D), F32),
        scratch_shapes=[pltpu.VMEM((TOP_K, TC_, D), F32), pltpu.SemaphoreType.DMA(())],
        compiler_params=_params(("arbitrary",)),
        name="moe_combine",
    )(dest_flat, w6, ys, shared, x, gate[0], gate[1])


def _pad_rows(a, rows):
    return jnp.concatenate([a, jnp.zeros((rows - a.shape[0],) + a.shape[1:], a.dtype)], axis=0)


def _s5_layer(z, lw, s5p_p, s5p_s, h0r_s, h0i_s):
    g_mix_c = lw["g_mix"][W_A + W_B:]
    nj = nsc = 16
    uc_p = z[:N_PROMPT, IN_COLS - W_C:]
    u_p = uc_p.reshape(B_P, nsc, nj, S5_L_P, G_C, SSM_GROUP).transpose(4, 2, 1, 0, 3, 5)
    u_p = u_p.reshape(G_C, nj * nsc * B_P, S5_L_P * SSM_GROUP)
    zeros_h = jnp.zeros((G_C, B_P, SSM_STATE), F32)
    d_p = jnp.tile(lw["ssm_d"], (1, S5_L_P))[:, None, :]
    y_p, hr_p, hi_p = _s5_main(u_p, s5p_p, d_p, zeros_h, zeros_h, nj, nsc, B_P)
    y_p = y_p.reshape(G_C, nj, nsc, B_P, S5_L_P, SSM_GROUP).transpose(3, 2, 1, 4, 0, 5).reshape(N_PROMPT, W_C)
    c_p = _s5_post(y_p, lw["w_glu"], lw["b_glu"], g_mix_c)

    uc_s = z[N_PROMPT:N_REAL, IN_COLS - W_C:]
    u_s = uc_s.reshape(B_S, T_S, G_C, SSM_GROUP).transpose(2, 0, 1, 3).reshape(G_C, B_S, S5_L_S * SSM_GROUP)
    d_s = jnp.tile(lw["ssm_d"], (1, S5_L_S))[:, None, :]
    y_s, hr_s, hi_s = _s5_main(u_s, s5p_s, d_s, h0r_s.transpose(1, 0, 2), h0i_s.transpose(1, 0, 2), 1, 1, B_S)
    y_s = y_s.reshape(G_C, B_S, T_S, SSM_GROUP).transpose(1, 2, 0, 3).reshape(N_SAMPLE, W_C)
    c_s = _s5_post(y_s, lw["w_glu"], lw["b_glu"], g_mix_c)
    return (c_p, c_s, hr_p.transpose(1, 0, 2), hi_p.transpose(1, 0, 2),
            hr_s.transpose(1, 0, 2), hi_s.transpose(1, 0, 2))


def _layer(x, l, lw, experts, mod, bias_p, bias_s, cache_k, cache_v, h0r_s, h0i_s):
    sh1, sc1, g1, sh2, sc2, g2 = [_tile_tabs(m) for m in jnp.split(mod, 6, axis=-1)]
    (h,) = _modnorm(x, lw["g_attn"], sc1, sh1, (BF16,))
    z = _matmul(h, lw["w_in"].astype(BF16), 1536)

    a_p = _attn_prompt(z, bias_p)
    a_s = _attn_sample(z, cache_k, cache_v, l, bias_s)

    g_mix_b = lw["g_mix"][W_A:W_A + W_B]
    gm = (lw["g_bv"], lw["b_bv"], lw["w_s"], lw["b_s"], g_mix_b)
    b_p, _ = _gmlp(z, N_PROMPT // CHUNK, 3 * W_A // W_B, 3 * W_A // W_B + 1, *gm)
    uv_s = z[N_PROMPT:N_REAL, 3 * W_A:3 * W_A + 2 * W_B].reshape(B_S, T_S, 2 * W_B)
    uv_s = jnp.pad(uv_s, ((0, 0), (0, CHUNK - T_S), (0, 0))).reshape(B_S * CHUNK, 2 * W_B)
    b_s, vn_s = _gmlp(uv_s, B_S, 0, 1, *gm)
    b_s = b_s.reshape(B_S, CHUNK, W_B)[:, :T_S].reshape(N_SAMPLE, W_B)
    new_vb = vn_s.reshape(B_S, CHUNK, W_B)[:, :T_S]

    s5_in = (lw["lam_re"], lw["lam_im"], lw["log_dt"], lw["ssm_b_re"], lw["ssm_b_im"], lw["ssm_c_re"],
             lw["ssm_c_im"])
    c_p, c_s, hr_p, hi_p, hr_s, hi_s = _s5_layer(z, lw, _s5_params(*s5_in, S5_L_P), _s5_params(*s5_in, S5_L_S),
                                                  h0r_s, h0i_s)

    mixed = _mix(a_p, b_p, c_p, _pad_rows(a_s, TM), _pad_rows(b_s, TM), _pad_rows(c_s, TM), lw["g_mix"][:W_A])
    x = _matmul_residual(mixed, lw["w_out"].astype(BF16), x, g1, 1024)

    h2b, h2 = _modnorm(x, lw["g_ffn"], sc2, sh2, (BF16, F32))
    ind, wd, cnt = _router(h2b, lw["w_router"], lw["b_router"])
    be, misc = _plan(cnt)
    dest, w6 = _slots(ind, wd, cnt)
    dest_flat = dest[:, :8].reshape(-1)
    xs = _dispatch(misc[0], misc[8], dest_flat, h2)
    ys = _experts(l, be[:NB, 0], misc[16, N_EXPERTS - 1:], xs, *experts)
    shared = _shared(h2b, lw["w_sh_gate"], lw["w_sh_up"], lw["w_sh_down"])
    x = _combine(dest_flat, w6, ys, shared, x, g2)

    kv = lambda lo: (z[:N_PROMPT, lo:lo + W_A].reshape(B_P, T_P, H_A, HEAD_DIM)[:, T_P - BUF:],
                     z[N_PROMPT:N_REAL, lo:lo + W_A].reshape(B_S, T_S, H_A, HEAD_DIM))
    (k_p, k_s), (v_p, v_s) = kv(W_A), kv(2 * W_A)
    return x, (k_p, v_p, k_s, v_s, new_vb, hr_p, hi_p, hr_s, hi_s)


def kernel(x_prompt, x_sample, cache_a_k, cache_a_v, state_c_re, state_c_im, c_prompt, c_sample, rel_bias, w_ada, b_ada, g_attn, w_in, g_bv, b_bv, w_s, b_s, lam_re, lam_im, log_dt, ssm_b_re, ssm_b_im, ssm_c_re, ssm_c_im, ssm_d, w_glu, b_glu, g_mix, w_out, g_ffn, w_router, b_router, w_gate, w_up, w_down, w_sh_gate, w_sh_up, w_sh_down, g_final):
    assert x_prompt.shape == (B_P, T_P, D) and x_sample.shape == (B_S, T_S, D)
    assert cache_a_k.shape == (DEPTH, B_S, BUF, H_A, HEAD_DIM)
    per_layer = dict(g_attn=g_attn, w_in=w_in, g_bv=g_bv, b_bv=b_bv, w_s=w_s, b_s=b_s, lam_re=lam_re,
                     lam_im=lam_im, log_dt=log_dt, ssm_b_re=ssm_b_re, ssm_b_im=ssm_b_im, ssm_c_re=ssm_c_re,
                     ssm_c_im=ssm_c_im, ssm_d=ssm_d, w_glu=w_glu, b_glu=b_glu, g_mix=g_mix, w_out=w_out,
                     g_ffn=g_ffn, w_router=w_router, b_router=b_router, w_sh_gate=w_sh_gate, w_sh_up=w_sh_up,
                     w_sh_down=w_sh_down)

    x = _pad_rows(jnp.concatenate([x_prompt.reshape(N_PROMPT, D), x_sample.reshape(N_SAMPLE, D)], axis=0), NP)
    c_all = _pad_rows(jnp.concatenate([c_prompt, c_sample], axis=0), 16)
    mod = _mod_all(c_all, w_ada, b_ada)
    bias_p = _bias_bank(rel_bias, *_prompt_bias_tables())
    bias_s = _bias_bank(rel_bias, *_sample_bias_tables())[0]
    cache_k = cache_a_k.reshape(DEPTH * B_S, BUF, W_A)
    cache_v = cache_a_v.reshape(DEPTH * B_S, BUF, W_A)

    outs = []
    for l in range(DEPTH):
        lw = {name: w[l] for name, w in per_layer.items()}
        x, o = _layer(x, l, lw, (w_gate, w_up, w_down), mod[l], bias_p, bias_s, cache_k, cache_v,
                      state_c_re[l], state_c_im[l])
        outs.append(o)
    y = _finalnorm(x, g_final)
    stacked = [jnp.stack([o[j] for o in outs]) for j in range(9)]
    return (y[:N_PROMPT].reshape(B_P, T_P, D), y[N_PROMPT:N_REAL].reshape(B_S, T_S, D), *stacked)
```

```python
import functools
import math

import jax
import jax.numpy as jnp
import numpy as np
from jax import lax
from jax.experimental import pallas as pl
from jax.experimental.pallas import tpu as pltpu

F32 = jnp.float32
BF16 = jnp.bfloat16
I32 = jnp.int32

D = 2048
B_P, T_P = 2, 4096
B_S, T_S = 8, 8
DEPTH = 2
HEAD_DIM = 64
W_A = 1024
H_A = 16
W_B = 512
G_B = 8
W_C = 512
SSM_GROUP = 16
G_C = 32
SSM_STATE = 64
IN_COLS = 3 * W_A + 2 * W_B + W_C
DILATIONS = (1, 4, 16)
NK = 128
ATTN_UNITS = (8, 4, 4)
BUF = 2048
CHUNK = 128
N_BUCKETS = 32
REL_MAX_DIST = 2048
N_EXPERTS = 64
TOP_K = 6
D_EXPERT = 512
D_SHARED = 1024
ROUTE_SCALE = 2.5
EPS = 1e-6

N_PROMPT = B_P * T_P
N_SAMPLE = B_S * T_S
N_REAL = N_PROMPT + N_SAMPLE
TM = 512
NT = N_PROMPT // TM + 1
NP = NT * TM
TD = 256
N_DTILES = -(-N_REAL // TD)
BM = 256
BM_SHIFT = 8
NB = -(-(N_REAL * TOP_K + N_EXPERTS * (BM - 1)) // BM)
NBP = -(-NB // 8) * 8
CAP = NB * BM
TC_ = 128
KF_ROWS = 2176
S5_L_P = 16
S5_L_S = T_S
BIG = 1e9
assert BM == 1 << BM_SHIFT

VMEM_LIMIT_BYTES = 56 * 1024 * 1024


def _params(dims):
    return pltpu.CompilerParams(dimension_semantics=dims, vmem_limit_bytes=VMEM_LIMIT_BYTES)


def _dot(a, b):
    return jnp.dot(a, b, preferred_element_type=F32)


def _dot_nt(a, b, precision=None):
    return lax.dot_general(a, b, (((1,), (1,)), ((), ())), preferred_element_type=F32,
                           precision=precision)


def _rms(x, g):
    return x * lax.rsqrt(jnp.mean(x * x, axis=-1, keepdims=True) + EPS) * g


def _mod_kernel(c_ref, w_ref, b_ref, o_ref):
    c = c_ref[...]
    s = (c * jax.nn.sigmoid(c)).astype(BF16)
    o_ref[...] = _dot(s, w_ref[...].astype(BF16)) + b_ref[...]


def _mod_all(c_all, w_ada, b_ada):
    tn = 1024
    return pl.pallas_call(
        _mod_kernel,
        grid=(DEPTH, 6 * D // tn),
        in_specs=[pl.BlockSpec((16, D), lambda l, j: (0, 0)),
                  pl.BlockSpec((None, D, tn), lambda l, j: (l, 0, j)),
                  pl.BlockSpec((None, 1, tn), lambda l, j: (l, 0, j))],
        out_specs=pl.BlockSpec((None, 16, tn), lambda l, j: (l, 0, j)),
        out_shape=jax.ShapeDtypeStruct((DEPTH, 16, 6 * D), F32),
        compiler_params=_params(("arbitrary", "arbitrary")),
        name="mod",
    )(c_all, w_ada, b_ada.reshape(DEPTH, 1, 6 * D))


def _tile_tabs(m):
    row = jnp.concatenate([jnp.repeat(m[0:B_P], NT // B_P, axis=0), jnp.zeros((1, m.shape[1]), F32)], 0)
    tok = jnp.concatenate([jnp.repeat(m[B_P:B_P + B_S], T_S, axis=0),
                           jnp.zeros((TM - N_SAMPLE, m.shape[1]), F32)], 0)
    return row[:, None, :], tok


def _pick(last, tok_ref, row_ref):
    return jnp.where(last, tok_ref[...], row_ref[...])


def _modnorm_kernel(x_ref, g_ref, scr_ref, sct_ref, shr_ref, sht_ref, *o_refs):
    last = pl.program_id(0) == NT - 1
    y = _rms(x_ref[...], g_ref[...])
    y = y * (1.0 + _pick(last, sct_ref, scr_ref)) + _pick(last, sht_ref, shr_ref)
    for o in o_refs:
        o[...] = y.astype(o.dtype)


def _modnorm(x, g, sc, sh, out_dtypes):
    row = pl.BlockSpec((None, 1, D), lambda i: (i, 0, 0))
    tok = pl.BlockSpec((TM, D), lambda i: (0, 0))
    til = pl.BlockSpec((TM, D), lambda i: (i, 0))
    return pl.pallas_call(
        _modnorm_kernel,
        grid=(NT,),
        in_specs=[til, pl.BlockSpec((1, D), lambda i: (0, 0)), row, tok, row, tok],
        out_specs=[til for _ in out_dtypes],
        out_shape=[jax.ShapeDtypeStruct((NP, D), dt) for dt in out_dtypes],
        compiler_params=_params(("arbitrary",)),
        name="modnorm",
    )(x, g.reshape(1, D), sc[0], sc[1], sh[0], sh[1])


def _finalnorm_kernel(x_ref, g_ref, o_ref):
    o_ref[...] = _rms(x_ref[...], g_ref[...])


def _finalnorm(x, g):
    til = pl.BlockSpec((TM, D), lambda i: (i, 0))
    return pl.pallas_call(
        _finalnorm_kernel,
        grid=(NT,),
        in_specs=[til, pl.BlockSpec((1, D), lambda i: (0, 0))],
        out_specs=til,
        out_shape=jax.ShapeDtypeStruct((NP, D), F32),
        compiler_params=_params(("arbitrary",)),
        name="finalnorm",
    )(x, g.reshape(1, D))


def _mm_kernel(a_ref, b_ref, o_ref):
    o_ref[...] = _dot(a_ref[...], b_ref[...])


def _matmul(a, b, tn):
    k, n = b.shape
    return pl.pallas_call(
        _mm_kernel,
        grid=(n // tn, NT),
        in_specs=[pl.BlockSpec((TM, k), lambda c, i: (i, 0)),
                  pl.BlockSpec((k, tn), lambda c, i: (0, c))],
        out_specs=pl.BlockSpec((TM, tn), lambda c, i: (i, c)),
        out_shape=jax.ShapeDtypeStruct((NP, n), F32),
        compiler_params=_params(("arbitrary", "arbitrary")),
        name="matmul",
    )(a, b)


def _mm_res_kernel(a_ref, b_ref, x_ref, gr_ref, gt_ref, o_ref):
    last = pl.program_id(1) == NT - 1
    o_ref[...] = x_ref[...] + _pick(last, gt_ref, gr_ref) * _dot(a_ref[...], b_ref[...])


def _matmul_residual(a, b, x, gate, tn):
    k, n = b.shape
    return pl.pallas_call(
        _mm_res_kernel,
        grid=(n // tn, NT),
        in_specs=[pl.BlockSpec((TM, k), lambda c, i: (i, 0)),
                  pl.BlockSpec((k, tn), lambda c, i: (0, c)),
                  pl.BlockSpec((TM, tn), lambda c, i: (i, c)),
                  pl.BlockSpec((None, 1, tn), lambda c, i: (i, 0, c)),
                  pl.BlockSpec((TM, tn), lambda c, i: (0, c))],
        out_specs=pl.BlockSpec((TM, tn), lambda c, i: (i, c)),
        out_shape=jax.ShapeDtypeStruct((NP, n), F32),
        compiler_params=_params(("arbitrary", "arbitrary")),
        name="matmul_residual",
    )(a, b, x, gate[0], gate[1])


def _t5_bucket(dist):
    max_exact = N_BUCKETS // 2
    dist = jnp.maximum(dist, 0)
    ratio = jnp.log(jnp.maximum(dist, 1).astype(F32) / max_exact) / math.log(REL_MAX_DIST / max_exact)
    large = jnp.minimum(max_exact + (ratio * (N_BUCKETS - max_exact)).astype(I32), N_BUCKETS - 1)
    return jnp.where(dist < max_exact, dist, large)


def _bias_kernel(rb_ref, bucket_ref, add_ref, o_ref):
    h = pl.program_id(1)
    bucket = bucket_ref[...]
    acc = jnp.full(bucket.shape, -jnp.inf, F32)
    for b in range(N_BUCKETS):
        acc = jnp.where(bucket == b, rb_ref[b * H_A + h], acc)
    o_ref[...] = acc + add_ref[...]


def _bias_bank(rel_bias, bucket, add):
    n, r, c = bucket.shape
    blk = pl.BlockSpec((None, r, c), lambda i, h, rb: (i, 0, 0))
    return pl.pallas_call(
        _bias_kernel,
        grid_spec=pltpu.PrefetchScalarGridSpec(
            num_scalar_prefetch=1, grid=(n, H_A), in_specs=[blk, blk],
            out_specs=pl.BlockSpec((None, None, r, c), lambda i, h, rb: (i, h, 0, 0))),
        out_shape=jax.ShapeDtypeStruct((n, H_A, r, c), F32),
        compiler_params=_params(("arbitrary", "arbitrary")),
        name="bias_bank",
    )(rel_bias.reshape(-1), bucket, add)


def _prompt_bias_tables():
    qi = jnp.arange(NK)[:, None]
    kj = jnp.arange(2 * NK)[None, :]
    rel = qi + NK - kj
    band = (rel >= 0) & (rel <= NK)
    bucket = jnp.stack([jnp.where(band, _t5_bucket(rel * d), -1) for d in DILATIONS]).astype(I32)
    return bucket, jnp.zeros(bucket.shape, F32)


def _sample_bias_tables():
    dist_np = BUF + np.arange(T_S)[:, None] - np.arange(KF_ROWS)[None, :]
    mult = np.zeros(dist_np.shape, np.float64)
    for d in DILATIONS:
        mult += (dist_np >= 0) & (dist_np % d == 0) & (dist_np // d <= NK)
    mult = np.where(np.arange(KF_ROWS)[None, :] < BUF + T_S, mult, 0.0)
    valid = mult > 0
    add = np.where(valid, np.log(np.maximum(mult, 1.0)), 0.0).astype(np.float32)
    bucket = jnp.where(jnp.asarray(valid), _t5_bucket(jnp.asarray(dist_np, I32)), -1).astype(I32)
    return bucket[None], jnp.asarray(add)[None]


def _attn_prompt_kernel(q_ref, k_ref, v_ref, bias_ref, o_ref, acc_ref, m_ref, l_ref):
    lane = lax.broadcasted_iota(I32, (NK, 128), 1)
    head0 = lane < HEAD_DIM
    prev_cols = lax.broadcasted_iota(I32, (NK, 2 * NK), 1) < NK
    last = len(DILATIONS) - 1
    for di, d in enumerate(DILATIONS):
        shift = int(math.log2(d))
        n_units = ATTN_UNITS[di]

        def group(gi, carry, di=di, d=d, shift=shift, n_units=n_units):
            def idx(start):
                return pl.ds(start, NK) if d == 1 else pl.ds(start, NK, stride=d)

            units = []
            for j in range(n_units):
                u = gi * n_units + j
                r = u & (d - 1)
                n = u >> shift
                cur = r + d * NK * n
                prv = r + d * NK * jnp.maximum(n - 1, 0)
                unit = dict(
                    cur=cur,
                    pen=jnp.where(n == 0, -jnp.inf, 0.0).astype(F32),
                    q=q_ref[idx(cur), :] * (HEAD_DIM ** -0.5),
                    k2=jnp.concatenate([k_ref[idx(prv), :], k_ref[idx(cur), :]], axis=0).astype(BF16),
                    v2=jnp.concatenate([jnp.concatenate([v_ref[idx(prv), :], v_ref[idx(cur), :]], axis=0),
                                        jnp.ones((2 * NK, 128), F32)], axis=1).astype(BF16))
                if di > 0:
                    unit.update(m=m_ref[idx(cur), :], l=l_ref[idx(cur), :], a=acc_ref[idx(cur), :])
                units.append(unit)
            results = []
            for unit in units:
                pv, rs, mn = [], [], []
                for h in range(2):
                    qh = jnp.where(head0 if h == 0 else ~head0, unit["q"], 0.0).astype(BF16)
                    s = _dot_nt(qh, unit["k2"]) + bias_ref[di, h] + jnp.where(prev_cols, unit["pen"], 0.0)
                    m_h = jnp.max(s, axis=-1, keepdims=True)
                    if di > 0:
                        m_h = jnp.maximum(m_h, unit["m"][:, h * HEAD_DIM:h * HEAD_DIM + 1])
                    pv_rs = _dot(jnp.exp(s - m_h).astype(BF16), unit["v2"])
                    pv.append(pv_rs[:, 0:128])
                    rs.append(pv_rs[:, 128:256])
                    mn.append(m_h)
                m_new = jnp.where(head0, mn[0], mn[1])
                l_new = jnp.where(head0, rs[0], rs[1])
                a_new = jnp.where(head0, pv[0], pv[1])
                if di > 0:
                    alpha = jnp.exp(unit["m"] - m_new)
                    l_new = alpha * unit["l"] + l_new
                    a_new = alpha * unit["a"] + a_new
                results.append((unit["cur"], m_new, l_new, a_new))
            for cur, m_new, l_new, a_new in results:
                if di == last:
                    o_ref[idx(cur), :] = a_new / l_new
                else:
                    m_ref[idx(cur), :] = m_new
                    l_ref[idx(cur), :] = l_new
                    acc_ref[idx(cur), :] = a_new
            return carry

        lax.fori_loop(0, T_P // NK // n_units, group, 0)


def _attn_prompt(z, bias):
    blk = lambda off: pl.BlockSpec((T_P, 128), lambda b, h, off=off: (b, off + h))
    return pl.pallas_call(
        _attn_prompt_kernel,
        grid=(B_P, H_A // 2),
        in_specs=[blk(0), blk(W_A // 128), blk(2 * W_A // 128),
                  pl.BlockSpec((len(DILATIONS), 2, NK, 2 * NK), lambda b, h: (0, h, 0, 0))],
        out_specs=pl.BlockSpec((T_P, 128), lambda b, h: (b, h)),
        out_shape=jax.ShapeDtypeStruct((N_PROMPT, W_A), F32),
        scratch_shapes=[pltpu.VMEM((T_P, 128), F32)] * 3,
        compiler_params=_params(("arbitrary", "arbitrary")),
        name="attn_prompt",
    )(z, z, z, bias)


def _attn_sample_kernel(q_ref, kn_ref, vn_ref, ck_ref, cv_ref, bias_ref, o_ref, kf_ref, vf_ref):
    for f_ref, c_ref, n_ref in ((kf_ref, ck_ref, kn_ref), (vf_ref, cv_ref, vn_ref)):
        f_ref[0:BUF, :] = c_ref[...]
        f_ref[BUF:BUF + T_S, :] = n_ref[...]
        f_ref[BUF + T_S:KF_ROWS, :] = jnp.zeros((KF_ROWS - BUF - T_S, 128), F32)
    head0 = lax.broadcasted_iota(I32, (T_S, 128), 1) < HEAD_DIM
    q = q_ref[...] * (HEAD_DIM ** -0.5)
    q2 = jnp.concatenate([jnp.where(head0, q, 0.0), jnp.where(head0, 0.0, q)], axis=0).astype(BF16)
    bias2 = jnp.concatenate([bias_ref[0], bias_ref[1]], axis=0)
    s = _dot_nt(q2, kf_ref[...].astype(BF16)) + bias2
    m = jnp.max(s, axis=-1, keepdims=True)
    p = jnp.exp(s - m)
    l = jnp.sum(p, axis=-1, keepdims=True)
    o2 = _dot(p.astype(BF16), vf_ref[...].astype(BF16)) / l
    o_ref[...] = jnp.where(head0, o2[0:T_S], o2[T_S:2 * T_S])


def _attn_sample(z, cache_k, cache_v, layer, bias):
    row0 = N_PROMPT // T_S
    blk = lambda off: pl.BlockSpec((T_S, 128), lambda b, h, off=off: (row0 + b, off + h))
    cache = pl.BlockSpec((None, BUF, 128), lambda b, h: (layer * B_S + b, 0, h))
    return pl.pallas_call(
        _attn_sample_kernel,
        grid=(B_S, H_A // 2),
        in_specs=[blk(0), blk(W_A // 128), blk(2 * W_A // 128), cache, cache,
                  pl.BlockSpec((2, T_S, KF_ROWS), lambda b, h: (h, 0, 0))],
        out_specs=pl.BlockSpec((T_S, 128), lambda b, h: (b, h)),
        out_shape=jax.ShapeDtypeStruct((N_SAMPLE, W_A), F32),
        scratch_shapes=[pltpu.VMEM((KF_ROWS, 128), F32)] * 2,
        compiler_params=_params(("arbitrary", "arbitrary")),
        name="attn_sample",
    )(z, z, z, cache_k, cache_v, bias)


def _gmlp_kernel(ub_ref, vb_ref, gbv_ref, bbv_ref, ws_ref, bs_ref, gmix_ref, bo_ref, vn_ref):
    v = jax.nn.gelu(vb_ref[...])
    mu = jnp.mean(v, axis=-1, keepdims=True)
    var = jnp.mean(jnp.square(v - mu), axis=-1, keepdims=True)
    vn = (v - mu) * lax.rsqrt(var + EPS) * gbv_ref[...] + bbv_ref[...]
    vn_ref[...] = vn
    vnb = vn.astype(BF16)
    causal = (lax.broadcasted_iota(I32, (CHUNK, CHUNK), 1) <= lax.broadcasted_iota(I32, (CHUNK, CHUNK), 0))
    first = lax.broadcasted_iota(I32, (CHUNK, 128), 1) < W_B // G_B
    parts = []
    for gp in range(G_B // 2):
        v2 = vnb[:, gp * 128:(gp + 1) * 128]
        m0 = _dot(jnp.where(causal, ws_ref[2 * gp], 0.0).astype(BF16), v2)
        m1 = _dot(jnp.where(causal, ws_ref[2 * gp + 1], 0.0).astype(BF16), v2)
        parts.append(jnp.where(first, m0, m1))
    mix = jnp.concatenate(parts, axis=1) + bs_ref[...]
    bo_ref[...] = _rms(jax.nn.gelu(ub_ref[...]) * mix, gmix_ref[...])


def _gmlp(z, n_chunks, col_u, col_v, g_bv, b_bv, w_s, b_s, g_mix_b):
    bs_wide = jnp.repeat(b_s.T, W_B // G_B, axis=1)
    vec = pl.BlockSpec((1, W_B), lambda i: (0, 0))
    out = pl.BlockSpec((CHUNK, W_B), lambda i: (i, 0))
    return pl.pallas_call(
        _gmlp_kernel,
        grid=(n_chunks,),
        in_specs=[pl.BlockSpec((CHUNK, W_B), lambda i: (i, col_u)),
                  pl.BlockSpec((CHUNK, W_B), lambda i: (i, col_v)),
                  vec, vec,
                  pl.BlockSpec((G_B, CHUNK, CHUNK), lambda i: (0, 0, 0)),
                  pl.BlockSpec((CHUNK, W_B), lambda i: (0, 0)),
                  vec],
        out_specs=[out, out],
        out_shape=[jax.ShapeDtypeStruct((n_chunks * CHUNK, W_B), F32)] * 2,
        compiler_params=_params(("arbitrary",)),
        name="gmlp",
    )(z, z, g_bv.reshape(1, W_B), b_bv.reshape(1, W_B), w_s, bs_wide, g_mix_b.reshape(1, W_B))


def _cmul(a, b):
    return a[0] * b[0] - a[1] * b[1], a[0] * b[1] + a[1] * b[0]


def _s5_param_kernel(lr_ref, li_ref, ldt_ref, btr_ref, bti_ref, cr_ref, ci_ref,
                     tz_ref, minr_ref, mini_ref, cor_ref, coi_ref, al_ref, *, L):
    hi = lax.Precision.HIGHEST
    lr, li = lr_ref[...], li_ref[...]
    dt = jnp.exp(ldt_ref[...])
    mag = jnp.exp(lr * dt)
    a = (mag * jnp.cos(li * dt), mag * jnp.sin(li * dt))
    den = lr * lr + li * li
    qr = ((a[0] - 1.0) * lr + a[1] * li) / den
    qi = (a[1] * lr - (a[0] - 1.0) * li) / den
    bb = _cmul((qr, qi), (btr_ref[...], bti_ref[...]))
    pw = [(jnp.ones_like(lr), jnp.zeros_like(lr))]
    for _ in range(L):
        pw.append(_cmul(pw[-1], a))
    c = (cr_ref[...], ci_ref[...])
    ck = [_cmul(c, p) for p in pw]
    n = SSM_GROUP * L
    cstack = [jnp.concatenate([x[j] for x in ck[0:L]], axis=0) for j in range(2)]
    btile = [jnp.concatenate([bb[j]] * L, axis=0) for j in range(2)]
    kw = _dot_nt(cstack[0], btile[0], hi) - _dot_nt(cstack[1], btile[1], hi)
    colb = lax.broadcasted_iota(I32, (n, n), 1) >> int(math.log2(SSM_GROUP))
    tz = jnp.zeros((n, n), F32)
    for s in range(L):
        sh = kw if s == 0 else jnp.concatenate([jnp.zeros((SSM_GROUP * s, n), F32), kw[:n - SSM_GROUP * s]], axis=0)
        tz = jnp.where(colb == s, sh, tz)
    tz_ref[...] = tz
    mins = [_cmul(pw[L - 1 - s], bb) for s in range(L)]
    minr_ref[...] = jnp.concatenate([x[0] for x in mins], axis=0)
    mini_ref[...] = jnp.concatenate([x[1] for x in mins], axis=0)
    cor_ref[...] = jnp.concatenate([x[0] for x in ck[1:L + 1]], axis=0)
    coi_ref[...] = jnp.concatenate([x[1] for x in ck[1:L + 1]], axis=0)
    al_ref[...] = jnp.concatenate([pw[L][0], pw[L][1]], axis=0)


def _s5_params(lam_re, lam_im, log_dt, b_re, b_im, c_re, c_im, L):
    n = SSM_GROUP * L
    p = SSM_STATE
    vec = pl.BlockSpec((None, 1, p), lambda g: (g, 0, 0))
    mat = pl.BlockSpec((None, SSM_GROUP, p), lambda g: (g, 0, 0))
    big = pl.BlockSpec((None, n, p), lambda g: (g, 0, 0))
    return pl.pallas_call(
        functools.partial(_s5_param_kernel, L=L),
        grid=(G_C,),
        in_specs=[vec, vec, vec, mat, mat, mat, mat],
        out_specs=[pl.BlockSpec((None, n, n), lambda g: (g, 0, 0)), big, big, big, big,
                   pl.BlockSpec((None, 2, p), lambda g: (g, 0, 0))],
        out_shape=[jax.ShapeDtypeStruct((G_C, n, n), F32)] + [jax.ShapeDtypeStruct((G_C, n, p), F32)] * 4
                  + [jax.ShapeDtypeStruct((G_C, 2, p), F32)],
        compiler_params=_params(("arbitrary",)),
        name="s5_params",
    )(lam_re.reshape(G_C, 1, p), lam_im.reshape(G_C, 1, p),
      jnp.broadcast_to(log_dt[:, None, None], (G_C, 1, p)),
      b_re.transpose(0, 2, 1), b_im.transpose(0, 2, 1), c_re, c_im)


def _s5_main_kernel(u_ref, tz_ref, minr_ref, mini_ref, cor_ref, coi_ref, al_ref, d_ref, h0r_ref, h0i_ref,
                    y_ref, hr_ref, hi_ref, er_ref, ei_ref, gr_ref, gi_ref, *, nj, nsc, nb):
    u = u_ref[...]
    ub = u.astype(BF16)
    y = _dot_nt(ub, tz_ref[...].astype(BF16)) + d_ref[...] * u
    s = (_dot(ub, minr_ref[...].astype(BF16)), _dot(ub, mini_ref[...].astype(BF16)))
    al = (al_ref[0:1, :], al_ref[1:2, :])
    r = nsc * nb
    xl = [(s[0][0:r], s[1][0:r])]
    for j in range(1, nj):
        t = _cmul(al, xl[-1])
        xl.append((t[0] + s[0][j * r:(j + 1) * r], t[1] + s[1][j * r:(j + 1) * r]))
    alp = [al]
    for j in range(1, nj):
        alp.append(_cmul(alp[-1], al))
    er_ref[...] = xl[-1][0]
    ei_ref[...] = xl[-1][1]
    g = (h0r_ref[...], h0i_ref[...])
    for sc in range(nsc):
        gr_ref[sc * nb:(sc + 1) * nb, :] = g[0]
        gi_ref[sc * nb:(sc + 1) * nb, :] = g[1]
        t = _cmul(alp[nj - 1], g)
        g = (t[0] + er_ref[sc * nb:(sc + 1) * nb, :], t[1] + ei_ref[sc * nb:(sc + 1) * nb, :])
    gcat = (gr_ref[...], gi_ref[...])
    hprev = [gcat]
    for j in range(1, nj):
        t = _cmul(alp[j - 1], gcat)
        hprev.append((xl[j - 1][0] + t[0], xl[j - 1][1] + t[1]))
    hpr = jnp.concatenate([x[0] for x in hprev], axis=0).astype(BF16)
    hpi = jnp.concatenate([x[1] for x in hprev], axis=0).astype(BF16)
    y_ref[...] = y + _dot_nt(hpr, cor_ref[...].astype(BF16)) - _dot_nt(hpi, coi_ref[...].astype(BF16))
    hr_ref[...] = g[0]
    hi_ref[...] = g[1]


def _s5_main(u, params, d_tile, h0r, h0i, nj, nsc, nb):
    tz, minr, mini, cor, coi, al = params
    m, n = u.shape[1], u.shape[2]
    p = SSM_STATE
    big = pl.BlockSpec((None, n, p), lambda g: (g, 0, 0))
    st = pl.BlockSpec((None, nb, p), lambda g: (g, 0, 0))
    return pl.pallas_call(
        functools.partial(_s5_main_kernel, nj=nj, nsc=nsc, nb=nb),
        grid=(G_C,),
        in_specs=[pl.BlockSpec((None, m, n), lambda g: (g, 0, 0)),
                  pl.BlockSpec((None, n, n), lambda g: (g, 0, 0)), big, big, big, big,
                  pl.BlockSpec((None, 2, p), lambda g: (g, 0, 0)),
                  pl.BlockSpec((None, 1, n), lambda g: (g, 0, 0)), st, st],
        out_specs=[pl.BlockSpec((None, m, n), lambda g: (g, 0, 0)), st, st],
        out_shape=[jax.ShapeDtypeStruct((G_C, m, n), F32)] + [jax.ShapeDtypeStruct((G_C, nb, p), F32)] * 2,
        scratch_shapes=[pltpu.VMEM((nsc * nb, p), F32)] * 4,
        compiler_params=_params(("arbitrary",)),
        name="s5_main",
    )(u, tz, minr, mini, cor, coi, al, d_tile, h0r, h0i)


def _s5_post_kernel(y_ref, w_ref, b_ref, g_ref, o_ref):
    zz = jax.nn.gelu(y_ref[...])
    out = zz * jax.nn.sigmoid(_dot(zz.astype(BF16), w_ref[...]) + b_ref[...])
    o_ref[...] = _rms(out, g_ref[...])


def _s5_post(y, w_glu, b_glu, g_mix_c):
    n = y.shape[0]
    tt = min(n, TM)
    vec = pl.BlockSpec((1, W_C), lambda i: (0, 0))
    return pl.pallas_call(
        _s5_post_kernel,
        grid=(n // tt,),
        in_specs=[pl.BlockSpec((tt, W_C), lambda i: (i, 0)), pl.BlockSpec((W_C, W_C), lambda i: (0, 0)), vec, vec],
        out_specs=pl.BlockSpec((tt, W_C), lambda i: (i, 0)),
        out_shape=jax.ShapeDtypeStruct((n, W_C), F32),
        compiler_params=_params(("arbitrary",)),
        name="s5_post",
    )(y, w_glu.astype(BF16), b_glu.reshape(1, W_C), g_mix_c.reshape(1, W_C))


def _mix_kernel(ap_ref, bp_ref, cp_ref, as_ref, bs_ref, cs_ref, g_ref, o_ref):
    def emit(a_ref, b_ref, c_ref):
        o_ref[:, 0:W_A] = _rms(a_ref[...], g_ref[...]).astype(BF16)
        o_ref[:, W_A:W_A + W_B] = b_ref[...].astype(BF16)
        o_ref[:, W_A + W_B:D] = c_ref[...].astype(BF16)

    last = pl.program_id(0) == NT - 1

    @pl.when(jnp.logical_not(last))
    def _():
        emit(ap_ref, bp_ref, cp_ref)

    @pl.when(last)
    def _():
        emit(as_ref, bs_ref, cs_ref)


def _mix(a_p, b_p, c_p, a_s, b_s, c_s, g_mix_a):
    pr = lambda w: pl.BlockSpec((TM, w), lambda i: (jnp.minimum(i, NT - 2), 0))
    sm = lambda w: pl.BlockSpec((TM, w), lambda i: (0, 0))
    return pl.pallas_call(
        _mix_kernel,
        grid=(NT,),
        in_specs=[pr(W_A), pr(W_B), pr(W_C), sm(W_A), sm(W_B), sm(W_C), pl.BlockSpec((1, W_A), lambda i: (0, 0))],
        out_specs=pl.BlockSpec((TM, D), lambda i: (i, 0)),
        out_shape=jax.ShapeDtypeStruct((NP, D), BF16),
        compiler_params=_params(("arbitrary",)),
        name="mix",
    )(a_p, b_p, c_p, a_s, b_s, c_s, g_mix_a.reshape(1, W_A))


def _router_kernel(h_ref, w_ref, b_ref, ind_ref, wd_ref, cnt_ref):
    i = pl.program_id(0)
    scores = jax.nn.sigmoid(_dot(h_ref[...], w_ref[...]))
    lane = lax.broadcasted_iota(I32, (TD, N_EXPERTS), 1).astype(F32)
    row = lax.broadcasted_iota(I32, (TD, N_EXPERTS), 0) + i * TD
    sel = scores + b_ref[...]
    ind = jnp.zeros((TD, N_EXPERTS), F32)
    for _ in range(TOP_K):
        m = jnp.max(sel, axis=-1, keepdims=True)
        idx = jnp.min(jnp.where(sel == m, lane, float(N_EXPERTS)), axis=-1, keepdims=True)
        hit = lane == idx
        ind = jnp.where(hit, 1.0, ind)
        sel = jnp.where(hit, -jnp.inf, sel)
    ind = jnp.where(row < N_REAL, ind, 0.0)
    wsel = ind * scores
    wd_ref[...] = wsel / jnp.sum(wsel + (1.0 - jnp.max(ind, axis=-1, keepdims=True)) / N_EXPERTS,
                                 axis=-1, keepdims=True) * ROUTE_SCALE
    ind_ref[...] = ind

    @pl.when(i == 0)
    def _():
        cnt_ref[...] = jnp.zeros((8, N_EXPERTS), F32)

    cnt_ref[...] += jnp.sum(ind, axis=0, keepdims=True)


def _router(h2b, w_router, b_router):
    til = pl.BlockSpec((TD, N_EXPERTS), lambda i: (i, 0))
    return pl.pallas_call(
        _router_kernel,
        grid=(NP // TD,),
        in_specs=[pl.BlockSpec((TD, D), lambda i: (i, 0)), pl.BlockSpec((D, N_EXPERTS), lambda i: (0, 0)),
                  pl.BlockSpec((1, N_EXPERTS), lambda i: (0, 0))],
        out_specs=[til, til, pl.BlockSpec((8, N_EXPERTS), lambda i: (0, 0))],
        out_shape=[jax.ShapeDtypeStruct((NP, N_EXPERTS), F32)] * 2 + [jax.ShapeDtypeStruct((8, N_EXPERTS), F32)],
        compiler_params=_params(("arbitrary",)),
        name="router",
    )(h2b, w_router.astype(BF16), b_router.reshape(1, N_EXPERTS))


def _group_starts(cnt):
    padded = jnp.floor((cnt + (BM - 1)) / BM) * BM
    tri = (lax.broadcasted_iota(I32, (N_EXPERTS, N_EXPERTS), 0)
           <= lax.broadcasted_iota(I32, (N_EXPERTS, N_EXPERTS), 1)).astype(F32)
    pend = jnp.dot(padded, tri, preferred_element_type=F32, precision=lax.Precision.HIGHEST)
    return padded, pend - padded, pend


def _plan_kernel(cnt_ref, be_ref, misc_ref):
    cnt = cnt_ref[...]
    _, pstart, pend = _group_starts(cnt)
    rowi = (lax.broadcasted_iota(I32, (NBP, N_EXPERTS), 0) * BM).astype(F32)
    be = jnp.minimum(jnp.sum(jnp.where(pend[0:1] <= rowi, 1.0, 0.0), axis=-1, keepdims=True), N_EXPERTS - 1.0)
    be_ref[...] = jnp.broadcast_to(be, (NBP, 128)).astype(I32)
    misc_ref[0:8, :] = pstart.astype(I32)
    misc_ref[8:16, :] = cnt.astype(I32)
    misc_ref[16:24, :] = (pend / BM).astype(I32)


def _plan(cnt):
    return pl.pallas_call(
        _plan_kernel,
        out_shape=[jax.ShapeDtypeStruct((NBP, 128), I32), jax.ShapeDtypeStruct((24, N_EXPERTS), I32)],
        name="moe_plan",
    )(cnt)


def _slots_kernel(ind_ref, wd_ref, cnt_ref, dest_ref, w_ref, run_ref):
    @pl.when(pl.program_id(0) == 0)
    def _():
        run_ref[...] = jnp.zeros((8, N_EXPERTS), F32)

    ind = ind_ref[...]
    _, pstart, _ = _group_starts(cnt_ref[...])
    ltri = (lax.broadcasted_iota(I32, (TD, TD), 1) < lax.broadcasted_iota(I32, (TD, TD), 0)).astype(BF16)
    pos = _dot(ltri, ind.astype(BF16)) + run_ref[0:1, :]
    run_ref[...] += jnp.sum(ind, axis=0, keepdims=True)
    cur = jnp.where(ind > 0.0, pstart[0:1] + pos, BIG)
    lane = lax.broadcasted_iota(I32, (TD, 128), 1)
    dest = jnp.full((TD, 128), -1.0, F32)
    wts = jnp.zeros((TD, 128), F32)
    wd = wd_ref[...]
    for k in range(TOP_K):
        m = jnp.min(cur, axis=-1, keepdims=True)
        hit = cur == m
        wk = jnp.sum(jnp.where(hit, wd, 0.0), axis=-1, keepdims=True)
        dest = jnp.where(lane == k, jnp.where(m < BIG, m, -1.0), dest)
        wts = jnp.where(lane == k, jnp.where(m < BIG, wk, 0.0), wts)
        cur = jnp.where(hit, BIG, cur)
    dest_ref[...] = dest.astype(I32)
    w_ref[...] = wts


def _slots(ind, wd, cnt):
    til = pl.BlockSpec((TD, N_EXPERTS), lambda i: (i, 0))
    out = pl.BlockSpec((TD, 128), lambda i: (i, 0))
    return pl.pallas_call(
        _slots_kernel,
        grid=(NP // TD,),
        in_specs=[til, til, pl.BlockSpec((8, N_EXPERTS), lambda i: (0, 0))],
        out_specs=[out, out],
        out_shape=[jax.ShapeDtypeStruct((NP, 128), I32), jax.ShapeDtypeStruct((NP, 128), F32)],
        scratch_shapes=[pltpu.VMEM((8, N_EXPERTS), F32)],
        compiler_params=_params(("arbitrary",)),
        name="moe_slots",
    )(ind, wd, cnt)


def _dispatch_kernel(ps_ref, cnt_ref, nbu_ref, dest_ref, x_ref, xs_ref, zero_ref, sem):
    i = pl.program_id(0)

    def row_copy(r, k):
        d = dest_ref[r * 8 + k]
        return pltpu.make_async_copy(x_ref.at[pl.ds(r, 1), :], xs_ref.at[pl.ds(d, 1), :], sem)

    def zero_row(d):
        return pltpu.make_async_copy(zero_ref.at[pl.ds(0, 1), :], xs_ref.at[pl.ds(d, 1), :], sem)

    def zero_block(b):
        return pltpu.make_async_copy(zero_ref, xs_ref.at[pl.ds(b * BM, BM), :], sem)

    @pl.when(i < N_DTILES)
    def _():
        n_rows = jnp.minimum(TD, N_REAL - i * TD)

        def issue(r, c):
            for k in range(TOP_K):
                row_copy(r, k).start(priority=k % 2)
            return c

        def drain(r, c):
            for k in range(TOP_K):
                row_copy(r, k).wait()
            return c

        lax.fori_loop(0, n_rows, issue, 0)
        lax.fori_loop(0, n_rows, drain, 0)

    @pl.when(i == N_DTILES)
    def _():
        zero_ref[...] = jnp.zeros(zero_ref.shape, F32)

        def per_expert(e, c):
            cnt = cnt_ref[e]
            base = ps_ref[e]
            end = ((cnt + (BM - 1)) >> BM_SHIFT) << BM_SHIFT
            lax.fori_loop(cnt, end, lambda j, c2: (zero_row(base + j).start(), c2)[1], 0)
            lax.fori_loop(cnt, end, lambda j, c2: (zero_row(base + j).wait(), c2)[1], 0)
            return c

        lax.fori_loop(0, N_EXPERTS, per_expert, 0)
        lax.fori_loop(nbu_ref[0], NB, lambda b, c: (zero_block(b).start(), c)[1], 0)
        lax.fori_loop(nbu_ref[0], NB, lambda b, c: (zero_block(b).wait(), c)[1], 0)


def _dispatch(pstart_i, cnt_i, nbu, dest_flat, x):
    return pl.pallas_call(
        _dispatch_kernel,
        grid_spec=pltpu.PrefetchScalarGridSpec(
            num_scalar_prefetch=3, grid=(N_DTILES + 1,),
            in_specs=[pl.BlockSpec((TD * 8,), lambda i, *_: (jnp.minimum(i, N_DTILES - 1),),
                                   memory_space=pltpu.SMEM),
                      pl.BlockSpec((TD, D), lambda i, *_: (jnp.minimum(i, N_DTILES - 1), 0))],
            out_specs=pl.BlockSpec(memory_space=pl.ANY),
            scratch_shapes=[pltpu.VMEM((BM, D), F32), pltpu.SemaphoreType.DMA(())]),
        out_shape=jax.ShapeDtypeStruct((CAP, D), F32),
        compiler_params=_params(("arbitrary",)),
        name="moe_dispatch",
    )(pstart_i, cnt_i, nbu, dest_flat, x)


def _expert_kernel(be_ref, nbu_ref, x_ref, wg_ref, wu_ref, wd_ref, o_ref):
    i = pl.program_id(0)

    @pl.when(i < nbu_ref[0])
    def _():
        x = x_ref[...].astype(BF16)
        g = _dot(x, wg_ref[...].astype(BF16))
        u = _dot(x, wu_ref[...].astype(BF16))
        a = (g * jax.nn.sigmoid(g) * u).astype(BF16)
        o_ref[...] = _dot(a, wd_ref[...].astype(BF16))

    @pl.when(i >= nbu_ref[0])
    def _():
        o_ref[...] = jnp.zeros(o_ref.shape, F32)


def _experts(layer, be, nbu, xs, w_gate, w_up, w_down):
    blk = lambda i, be, nbu: (jnp.minimum(i, nbu[0] - 1), 0)
    wsel = lambda i, be, nbu: (layer, be[jnp.minimum(i, nbu[0] - 1)], 0, 0)
    return pl.pallas_call(
        _expert_kernel,
        grid_spec=pltpu.PrefetchScalarGridSpec(
            num_scalar_prefetch=2, grid=(NB,),
            in_specs=[pl.BlockSpec((BM, D), blk),
                      pl.BlockSpec((None, None, D, D_EXPERT), wsel),
                      pl.BlockSpec((None, None, D, D_EXPERT), wsel),
                      pl.BlockSpec((None, None, D_EXPERT, D), wsel)],
            out_specs=pl.BlockSpec((BM, D), lambda i, be, nbu: (i, 0))),
        out_shape=jax.ShapeDtypeStruct((CAP, D), F32),
        compiler_params=_params(("arbitrary",)),
        name="moe_experts",
    )(be, nbu, xs, w_gate, w_up, w_down)


def _shared_kernel(h_ref, wg_ref, wu_ref, wd_ref, o_ref):
    h = h_ref[...]
    g = _dot(h, wg_ref[...])
    u = _dot(h, wu_ref[...])
    o_ref[...] = _dot((g * jax.nn.sigmoid(g) * u).astype(BF16), wd_ref[...])


def _shared(h2b, wg, wu, wd):
    return pl.pallas_call(
        _shared_kernel,
        grid=(NT,),
        in_specs=[pl.BlockSpec((TM, D), lambda i: (i, 0)),
                  pl.BlockSpec((D, D_SHARED), lambda i: (0, 0)),
                  pl.BlockSpec((D, D_SHARED), lambda i: (0, 0)),
                  pl.BlockSpec((D_SHARED, D), lambda i: (0, 0))],
        out_specs=pl.BlockSpec((TM, D), lambda i: (i, 0)),
        out_shape=jax.ShapeDtypeStruct((NP, D), F32),
        compiler_params=_params(("arbitrary",)),
        name="moe_shared",
    )(h2b, wg.astype(BF16), wu.astype(BF16), wd.astype(BF16))


def _combine_kernel(dest_ref, nxt_ref, w_ref, ys_ref, sh_ref, x_ref, gr_ref, gt_ref, o_ref, rows_ref, sem):
    i = pl.program_id(0)
    n_tiles = NP // TC_
    slot = i % 2

    def gather(d_ref, s, start):
        def body(r, c):
            for k in range(TOP_K):
                d = jnp.maximum(d_ref[r * 8 + k], 0)
                cp = pltpu.make_async_copy(ys_ref.at[pl.ds(d, 1), :], rows_ref.at[s, k, pl.ds(r, 1), :], sem.at[s])
                if start:
                    cp.start(priority=k % 2)
                else:
                    cp.wait()
            return c

        lax.fori_loop(0, TC_, body, 0)

    @pl.when(i == 0)
    def _():
        gather(dest_ref, 0, True)

    @pl.when(i + 1 < n_tiles)
    def _():
        gather(nxt_ref, 1 - slot, True)

    gather(dest_ref, slot, False)
    w = w_ref[...]
    routed = w[:, 0:1] * rows_ref[slot, 0]
    for k in range(1, TOP_K):
        routed = routed + w[:, k:k + 1] * rows_ref[slot, k]
    last = i >= N_PROMPT // TC_
    o_ref[...] = x_ref[...] + _pick(last, gt_ref, gr_ref) * (routed + sh_ref[...])


def _combine(dest_flat, w6, ys, shared, x, gate):
    n_tiles = NP // TC_
    n_p = N_PROMPT // TC_
    per = TM // TC_
    til = pl.BlockSpec((TC_, D), lambda i: (i, 0))
    return pl.pallas_call(
        _combine_kernel,
        grid=(n_tiles,),
        in_specs=[pl.BlockSpec((TC_ * 8,), lambda i: (i,), memory_space=pltpu.SMEM),
                  pl.BlockSpec((TC_ * 8,), lambda i: (jnp.minimum(i + 1, n_tiles - 1),), memory_space=pltpu.SMEM),
                  pl.BlockSpec((TC_, 128), lambda i: (i, 0)),
                  pl.BlockSpec(memory_space=pl.ANY), til, til,
                  pl.BlockSpec((None, 1, D), lambda i: (i // per, 0, 0)),
                  pl.BlockSpec((TC_, D), lambda i: (jnp.maximum(i - n_p, 0), 0))],
        out_specs=til,
        out_shape=jax.ShapeDtypeStruct((NP, D), F32),
        scratch_shapes=[pltpu.VMEM((2, TOP_K, TC_, D), F32), pltpu.SemaphoreType.DMA((2,))],
        compiler_params=_params(("arbitrary",)),
        name="moe_combine",
    )(dest_flat, dest_flat, w6, ys, shared, x, gate[0], gate[1])


def _pad_rows(a, rows):
    return jnp.concatenate([a, jnp.zeros((rows - a.shape[0],) + a.shape[1:], a.dtype)], axis=0)


def _s5_layer(z, lw, s5p_p, s5p_s, h0r_s, h0i_s):
    g_mix_c = lw["g_mix"][W_A + W_B:]
    nj = nsc = 16
    uc_p = z[:N_PROMPT, IN_COLS - W_C:]
    u_p = uc_p.reshape(B_P, nsc, nj, S5_L_P, G_C, SSM_GROUP).transpose(4, 2, 1, 0, 3, 5)
    u_p = u_p.reshape(G_C, nj * nsc * B_P, S5_L_P * SSM_GROUP)
    zeros_h = jnp.zeros((G_C, B_P, SSM_STATE), F32)
    d_p = jnp.tile(lw["ssm_d"], (1, S5_L_P))[:, None, :]
    y_p, hr_p, hi_p = _s5_main(u_p, s5p_p, d_p, zeros_h, zeros_h, nj, nsc, B_P)
    y_p = y_p.reshape(G_C, nj, nsc, B_P, S5_L_P, SSM_GROUP).transpose(3, 2, 1, 4, 0, 5).reshape(N_PROMPT, W_C)
    c_p = _s5_post(y_p, lw["w_glu"], lw["b_glu"], g_mix_c)

    uc_s = z[N_PROMPT:N_REAL, IN_COLS - W_C:]
    u_s = uc_s.reshape(B_S, T_S, G_C, SSM_GROUP).transpose(2, 0, 1, 3).reshape(G_C, B_S, S5_L_S * SSM_GROUP)
    d_s = jnp.tile(lw["ssm_d"], (1, S5_L_S))[:, None, :]
    y_s, hr_s, hi_s = _s5_main(u_s, s5p_s, d_s, h0r_s.transpose(1, 0, 2), h0i_s.transpose(1, 0, 2), 1, 1, B_S)
    y_s = y_s.reshape(G_C, B_S, T_S, SSM_GROUP).transpose(1, 2, 0, 3).reshape(N_SAMPLE, W_C)
    c_s = _s5_post(y_s, lw["w_glu"], lw["b_glu"], g_mix_c)
    return (c_p, c_s, hr_p.transpose(1, 0, 2), hi_p.transpose(1, 0, 2),
            hr_s.transpose(1, 0, 2), hi_s.transpose(1, 0, 2))


def _layer(x, l, lw, experts, mod, bias_p, bias_s, cache_k, cache_v, h0r_s, h0i_s):
    sh1, sc1, g1, sh2, sc2, g2 = [_tile_tabs(m) for m in jnp.split(mod, 6, axis=-1)]
    (h,) = _modnorm(x, lw["g_attn"], sc1, sh1, (BF16,))
    z = _matmul(h, lw["w_in"].astype(BF16), 1536)

    a_p = _attn_prompt(z, bias_p)
    a_s = _attn_sample(z, cache_k, cache_v, l, bias_s)

    g_mix_b = lw["g_mix"][W_A:W_A + W_B]
    gm = (lw["g_bv"], lw["b_bv"], lw["w_s"], lw["b_s"], g_mix_b)
    b_p, _ = _gmlp(z, N_PROMPT // CHUNK, 3 * W_A // W_B, 3 * W_A // W_B + 1, *gm)
    uv_s = z[N_PROMPT:N_REAL, 3 * W_A:3 * W_A + 2 * W_B].reshape(B_S, T_S, 2 * W_B)
    uv_s = jnp.pad(uv_s, ((0, 0), (0, CHUNK - T_S), (0, 0))).reshape(B_S * CHUNK, 2 * W_B)
    b_s, vn_s = _gmlp(uv_s, B_S, 0, 1, *gm)
    b_s = b_s.reshape(B_S, CHUNK, W_B)[:, :T_S].reshape(N_SAMPLE, W_B)
    new_vb = vn_s.reshape(B_S, CHUNK, W_B)[:, :T_S]

    s5_in = (lw["lam_re"], lw["lam_im"], lw["log_dt"], lw["ssm_b_re"], lw["ssm_b_im"], lw["ssm_c_re"],
             lw["ssm_c_im"])
    c_p, c_s, hr_p, hi_p, hr_s, hi_s = _s5_layer(z, lw, _s5_params(*s5_in, S5_L_P), _s5_params(*s5_in, S5_L_S),
                                                  h0r_s, h0i_s)

    mixed = _mix(a_p, b_p, c_p, _pad_rows(a_s, TM), _pad_rows(b_s, TM), _pad_rows(c_s, TM), lw["g_mix"][:W_A])
    x = _matmul_residual(mixed, lw["w_out"].astype(BF16), x, g1, 1024)

    h2b, h2 = _modnorm(x, lw["g_ffn"], sc2, sh2, (BF16, F32))
    ind, wd, cnt = _router(h2b, lw["w_router"], lw["b_router"])
    be, misc = _plan(cnt)
    dest, w6 = _slots(ind, wd, cnt)
    dest_flat = dest[:, :8].reshape(-1)
    xs = _dispatch(misc[0], misc[8], misc[16, N_EXPERTS - 1:], dest_flat, h2)
    ys = _experts(l, be[:NB, 0], misc[16, N_EXPERTS - 1:], xs, *experts)
    shared = _shared(h2b, lw["w_sh_gate"], lw["w_sh_up"], lw["w_sh_down"])
    x = _combine(dest_flat, w6, ys, shared, x, g2)

    kv = lambda lo: (z[:N_PROMPT, lo:lo + W_A].reshape(B_P, T_P, H_A, HEAD_DIM)[:, T_P - BUF:],
                     z[N_PROMPT:N_REAL, lo:lo + W_A].reshape(B_S, T_S, H_A, HEAD_DIM))
    (k_p, k_s), (v_p, v_s) = kv(W_A), kv(2 * W_A)
    return x, (k_p, v_p, k_s, v_s, new_vb, hr_p, hi_p, hr_s, hi_s)


def kernel(x_prompt, x_sample, cache_a_k, cache_a_v, state_c_re, state_c_im, c_prompt, c_sample, rel_bias, w_ada, b_ada, g_attn, w_in, g_bv, b_bv, w_s, b_s, lam_re, lam_im, log_dt, ssm_b_re, ssm_b_im, ssm_c_re, ssm_c_im, ssm_d, w_glu, b_glu, g_mix, w_out, g_ffn, w_router, b_router, w_gate, w_up, w_down, w_sh_gate, w_sh_up, w_sh_down, g_final):
    assert x_prompt.shape == (B_P, T_P, D) and x_sample.shape == (B_S, T_S, D)
    assert cache_a_k.shape == (DEPTH, B_S, BUF, H_A, HEAD_DIM)
    per_layer = dict(g_attn=g_attn, w_in=w_in, g_bv=g_bv, b_bv=b_bv, w_s=w_s, b_s=b_s, lam_re=lam_re,
                     lam_im=lam_im, log_dt=log_dt, ssm_b_re=ssm_b_re, ssm_b_im=ssm_b_im, ssm_c_re=ssm_c_re,
                     ssm_c_im=ssm_c_im, ssm_d=ssm_d, w_glu=w_glu, b_glu=b_glu, g_mix=g_mix, w_out=w_out,
                     g_ffn=g_ffn, w_router=w_router, b_router=b_router, w_sh_gate=w_sh_gate, w_sh_up=w_sh_up,
                     w_sh_down=w_sh_down)

    x = _pad_rows(jnp.concatenate([x_prompt.reshape(N_PROMPT, D), x_sample.reshape(N_SAMPLE, D)], axis=0), NP)
    c_all = _pad_rows(jnp.concatenate([c_prompt, c_sample], axis=0), 16)
    mod = _mod_all(c_all, w_ada, b_ada)
    bias_p = _bias_bank(rel_bias, *_prompt_bias_tables())
    bias_s = _bias_bank(rel_bias, *_sample_bias_tables())[0]
    cache_k = cache_a_k.reshape(DEPTH * B_S, BUF, W_A)
    cache_v = cache_a_v.reshape(DEPTH * B_S, BUF, W_A)

    outs = []
    for l in range(DEPTH):
        lw = {name: w[l] for name, w in per_layer.items()}
        x, o = _layer(x, l, lw, (w_gate, w_up, w_down), mod[l], bias_p, bias_s, cache_k, cache_v,
                      state_c_re[l], state_c_im[l])
        outs.append(o)
    y = _finalnorm(x, g_final)
    stacked = [jnp.stack([o[j] for o in outs]) for j in range(9)]
    return (y[:N_PROMPT].reshape(B_P, T_P, D), y[N_PROMPT:N_REAL].reshape(B_S, T_S, D), *stacked)
```

```python
import functools
import math

import jax
import jax.numpy as jnp
import numpy as np
from jax import lax
from jax.experimental import pallas as pl
from jax.experimental.pallas import tpu as pltpu

F32 = jnp.float32
BF16 = jnp.bfloat16
I32 = jnp.int32

D = 2048
B_P, T_P = 2, 4096
B_S, T_S = 8, 8
DEPTH = 2
HEAD_DIM = 64
W_A = 1024
H_A = 16
W_B = 512
G_B = 8
W_C = 512
SSM_GROUP = 16
G_C = 32
SSM_STATE = 64
IN_COLS = 3 * W_A + 2 * W_B + W_C
DILATIONS = (1, 4, 16)
NK = 128
ATTN_UNITS = (8, 4, 4)
BUF = 2048
CHUNK = 128
N_BUCKETS = 32
REL_MAX_DIST = 2048
N_EXPERTS = 64
TOP_K = 6
D_EXPERT = 512
D_SHARED = 1024
ROUTE_SCALE = 2.5
EPS = 1e-6

N_PROMPT = B_P * T_P
N_SAMPLE = B_S * T_S
N_REAL = N_PROMPT + N_SAMPLE
TM = 512
NT = N_PROMPT // TM + 1
NP = NT * TM
TD = 256
N_DTILES = -(-N_REAL // TD)
BM = 256
BM_SHIFT = 8
NB = -(-(N_REAL * TOP_K + N_EXPERTS * (BM - 1)) // BM)
NBP = -(-NB // 8) * 8
CAP = NB * BM
TC_ = 128
KF_ROWS = 2176
S5_L_P = 16
S5_L_S = T_S
S5_GPB = 128 // SSM_GROUP
S5_NLB = G_C // S5_GPB
S5_SW = S5_GPB * SSM_STATE
BIG = 1e9
assert BM == 1 << BM_SHIFT

VMEM_LIMIT_BYTES = 56 * 1024 * 1024


def _params(dims):
    return pltpu.CompilerParams(dimension_semantics=dims, vmem_limit_bytes=VMEM_LIMIT_BYTES)


def _dot(a, b):
    return jnp.dot(a, b, preferred_element_type=F32)


def _dot_nt(a, b, precision=None):
    return lax.dot_general(a, b, (((1,), (1,)), ((), ())), preferred_element_type=F32,
                           precision=precision)


def _rms(x, g):
    return x * lax.rsqrt(jnp.mean(x * x, axis=-1, keepdims=True) + EPS) * g


def _mod_kernel(c_ref, w_ref, b_ref, o_ref):
    c = c_ref[...]
    s = (c * jax.nn.sigmoid(c)).astype(BF16)
    o_ref[...] = _dot(s, w_ref[...].astype(BF16)) + b_ref[...]


def _mod_all(c_all, w_ada, b_ada):
    tn = 1024
    return pl.pallas_call(
        _mod_kernel,
        grid=(DEPTH, 6 * D // tn),
        in_specs=[pl.BlockSpec((16, D), lambda l, j: (0, 0)),
                  pl.BlockSpec((None, D, tn), lambda l, j: (l, 0, j)),
                  pl.BlockSpec((None, 1, tn), lambda l, j: (l, 0, j))],
        out_specs=pl.BlockSpec((None, 16, tn), lambda l, j: (l, 0, j)),
        out_shape=jax.ShapeDtypeStruct((DEPTH, 16, 6 * D), F32),
        compiler_params=_params(("arbitrary", "arbitrary")),
        name="mod",
    )(c_all, w_ada, b_ada.reshape(DEPTH, 1, 6 * D))


def _tile_tabs(m):
    row = jnp.concatenate([jnp.repeat(m[0:B_P], NT // B_P, axis=0), jnp.zeros((1, m.shape[1]), F32)], 0)
    tok = jnp.concatenate([jnp.repeat(m[B_P:B_P + B_S], T_S, axis=0),
                           jnp.zeros((TM - N_SAMPLE, m.shape[1]), F32)], 0)
    return row[:, None, :], tok


def _pick(last, tok_ref, row_ref):
    return jnp.where(last, tok_ref[...], row_ref[...])


def _modnorm_kernel(x_ref, g_ref, scr_ref, sct_ref, shr_ref, sht_ref, *o_refs):
    last = pl.program_id(0) == NT - 1
    y = _rms(x_ref[...], g_ref[...])
    y = y * (1.0 + _pick(last, sct_ref, scr_ref)) + _pick(last, sht_ref, shr_ref)
    for o in o_refs:
        o[...] = y.astype(o.dtype)


def _modnorm(x, g, sc, sh, out_dtypes):
    row = pl.BlockSpec((None, 1, D), lambda i: (i, 0, 0))
    tok = pl.BlockSpec((TM, D), lambda i: (0, 0))
    til = pl.BlockSpec((TM, D), lambda i: (i, 0))
    return pl.pallas_call(
        _modnorm_kernel,
        grid=(NT,),
        in_specs=[til, pl.BlockSpec((1, D), lambda i: (0, 0)), row, tok, row, tok],
        out_specs=[til for _ in out_dtypes],
        out_shape=[jax.ShapeDtypeStruct((NP, D), dt) for dt in out_dtypes],
        compiler_params=_params(("arbitrary",)),
        name="modnorm",
    )(x, g.reshape(1, D), sc[0], sc[1], sh[0], sh[1])


def _finalnorm_kernel(x_ref, g_ref, o_ref):
    o_ref[...] = _rms(x_ref[...], g_ref[...])


def _finalnorm(x, g):
    til = pl.BlockSpec((TM, D), lambda i: (i, 0))
    return pl.pallas_call(
        _finalnorm_kernel,
        grid=(NT,),
        in_specs=[til, pl.BlockSpec((1, D), lambda i: (0, 0))],
        out_specs=til,
        out_shape=jax.ShapeDtypeStruct((NP, D), F32),
        compiler_params=_params(("arbitrary",)),
        name="finalnorm",
    )(x, g.reshape(1, D))


def _mm_kernel(a_ref, b_ref, o_ref):
    o_ref[...] = _dot(a_ref[...], b_ref[...])


def _matmul(a, b, tn):
    k, n = b.shape
    return pl.pallas_call(
        _mm_kernel,
        grid=(n // tn, NT),
        in_specs=[pl.BlockSpec((TM, k), lambda c, i: (i, 0)),
                  pl.BlockSpec((k, tn), lambda c, i: (0, c))],
        out_specs=pl.BlockSpec((TM, tn), lambda c, i: (i, c)),
        out_shape=jax.ShapeDtypeStruct((NP, n), F32),
        compiler_params=_params(("arbitrary", "arbitrary")),
        name="matmul",
    )(a, b)


def _mm_res_kernel(a_ref, b_ref, x_ref, gr_ref, gt_ref, o_ref):
    last = pl.program_id(1) == NT - 1
    o_ref[...] = x_ref[...] + _pick(last, gt_ref, gr_ref) * _dot(a_ref[...], b_ref[...])


def _matmul_residual(a, b, x, gate, tn):
    k, n = b.shape
    return pl.pallas_call(
        _mm_res_kernel,
        grid=(n // tn, NT),
        in_specs=[pl.BlockSpec((TM, k), lambda c, i: (i, 0)),
                  pl.BlockSpec((k, tn), lambda c, i: (0, c)),
                  pl.BlockSpec((TM, tn), lambda c, i: (i, c)),
                  pl.BlockSpec((None, 1, tn), lambda c, i: (i, 0, c)),
                  pl.BlockSpec((TM, tn), lambda c, i: (0, c))],
        out_specs=pl.BlockSpec((TM, tn), lambda c, i: (i, c)),
        out_shape=jax.ShapeDtypeStruct((NP, n), F32),
        compiler_params=_params(("arbitrary", "arbitrary")),
        name="matmul_residual",
    )(a, b, x, gate[0], gate[1])


def _t5_bucket(dist):
    max_exact = N_BUCKETS // 2
    dist = jnp.maximum(dist, 0)
    ratio = jnp.log(jnp.maximum(dist, 1).astype(F32) / max_exact) / math.log(REL_MAX_DIST / max_exact)
    large = jnp.minimum(max_exact + (ratio * (N_BUCKETS - max_exact)).astype(I32), N_BUCKETS - 1)
    return jnp.where(dist < max_exact, dist, large)


def _bias_kernel(rb_ref, bucket_ref, add_ref, o_ref):
    h = pl.program_id(1)
    bucket = bucket_ref[...]
    acc = jnp.full(bucket.shape, -jnp.inf, F32)
    for b in range(N_BUCKETS):
        acc = jnp.where(bucket == b, rb_ref[b * H_A + h], acc)
    o_ref[...] = acc + add_ref[...]


def _bias_bank(rel_bias, bucket, add):
    n, r, c = bucket.shape
    blk = pl.BlockSpec((None, r, c), lambda i, h, rb: (i, 0, 0))
    return pl.pallas_call(
        _bias_kernel,
        grid_spec=pltpu.PrefetchScalarGridSpec(
            num_scalar_prefetch=1, grid=(n, H_A), in_specs=[blk, blk],
            out_specs=pl.BlockSpec((None, None, r, c), lambda i, h, rb: (i, h, 0, 0))),
        out_shape=jax.ShapeDtypeStruct((n, H_A, r, c), F32),
        compiler_params=_params(("arbitrary", "arbitrary")),
        name="bias_bank",
    )(rel_bias.reshape(-1), bucket, add)


def _prompt_bias_tables():
    qi = jnp.arange(NK)[:, None]
    kj = jnp.arange(2 * NK)[None, :]
    rel = qi + NK - kj
    band = (rel >= 0) & (rel <= NK)
    bucket = jnp.stack([jnp.where(band, _t5_bucket(rel * d), -1) for d in DILATIONS]).astype(I32)
    return bucket, jnp.zeros(bucket.shape, F32)


def _sample_bias_tables():
    dist_np = BUF + np.arange(T_S)[:, None] - np.arange(KF_ROWS)[None, :]
    mult = np.zeros(dist_np.shape, np.float64)
    for d in DILATIONS:
        mult += (dist_np >= 0) & (dist_np % d == 0) & (dist_np // d <= NK)
    mult = np.where(np.arange(KF_ROWS)[None, :] < BUF + T_S, mult, 0.0)
    valid = mult > 0
    add = np.where(valid, np.log(np.maximum(mult, 1.0)), 0.0).astype(np.float32)
    bucket = jnp.where(jnp.asarray(valid), _t5_bucket(jnp.asarray(dist_np, I32)), -1).astype(I32)
    return bucket[None], jnp.asarray(add)[None]


def _attn_prompt_kernel(q_ref, k_ref, v_ref, bias_ref, o_ref, acc_ref, m_ref, l_ref):
    lane = lax.broadcasted_iota(I32, (NK, 128), 1)
    head0 = lane < HEAD_DIM
    prev_cols = lax.broadcasted_iota(I32, (NK, 2 * NK), 1) < NK
    ones = jnp.ones((NK, 128), F32)
    order = tuple(range(len(DILATIONS)))
    for di in order:
        d = DILATIONS[di]
        n_blocks = T_P // (d * NK)
        n_units = ATTN_UNITS[di]
        run = min(n_units, n_blocks)
        assert n_units % run == 0 and n_blocks % run == 0
        first, final = di == order[0], di == order[-1]

        def group(gi, carry, di=di, d=d, n_blocks=n_blocks, n_units=n_units, run=run, first=first, final=final):
            def idx(start):
                return pl.ds(start, NK) if d == 1 else pl.ds(start, NK, stride=d)

            def keys(start):
                return k_ref[idx(start), :].astype(BF16)

            def values(start):
                return jnp.concatenate([v_ref[idx(start), :], ones], axis=1).astype(BF16)

            units = []
            for j in range(n_units):
                u = gi * n_units + j
                r = u >> int(math.log2(n_blocks))
                n = u & (n_blocks - 1)
                cur = r + d * NK * n
                if j % run == 0:
                    prv = r + d * NK * jnp.maximum(n - 1, 0)
                    k_prev, v_prev = keys(prv), values(prv)
                    pen = jnp.where(n == 0, -jnp.inf, 0.0).astype(F32)
                else:
                    k_prev, v_prev, pen = k_cur, v_cur, None
                k_cur, v_cur = keys(cur), values(cur)
                unit = dict(cur=cur, pen=pen, q=q_ref[idx(cur), :] * (HEAD_DIM ** -0.5),
                            k2=jnp.concatenate([k_prev, k_cur], axis=0),
                            v2=jnp.concatenate([v_prev, v_cur], axis=0))
                if not first:
                    unit.update(m=m_ref[idx(cur), :], l=l_ref[idx(cur), :], a=acc_ref[idx(cur), :])
                units.append(unit)
            results = []
            for unit in units:
                pv, rs, mn = [], [], []
                for h in range(2):
                    qh = jnp.where(head0 if h == 0 else ~head0, unit["q"], 0.0).astype(BF16)
                    s = _dot_nt(qh, unit["k2"]) + bias_ref[di, h]
                    if unit["pen"] is not None:
                        s = s + jnp.where(prev_cols, unit["pen"], 0.0)
                    m_h = jnp.max(s, axis=-1, keepdims=True)
                    if not first:
                        m_h = jnp.maximum(m_h, unit["m"][:, h * HEAD_DIM:h * HEAD_DIM + 1])
                    pv_rs = _dot(jnp.exp(s - m_h).astype(BF16), unit["v2"])
                    pv.append(pv_rs[:, 0:128])
                    rs.append(pv_rs[:, 128:256])
                    mn.append(m_h)
                m_new = jnp.where(head0, mn[0], mn[1])
                l_new = jnp.where(head0, rs[0], rs[1])
                a_new = jnp.where(head0, pv[0], pv[1])
                if not first:
                    alpha = jnp.exp(unit["m"] - m_new)
                    l_new = alpha * unit["l"] + l_new
                    a_new = alpha * unit["a"] + a_new
                results.append((unit["cur"], m_new, l_new, a_new))
            for cur, m_new, l_new, a_new in results:
                if final:
                    o_ref[idx(cur), :] = a_new / l_new
                else:
                    m_ref[idx(cur), :] = m_new
                    l_ref[idx(cur), :] = l_new
                    acc_ref[idx(cur), :] = a_new
            return carry

        lax.fori_loop(0, T_P // NK // n_units, group, 0)


def _attn_prompt(z, bias):
    blk = lambda off: pl.BlockSpec((T_P, 128), lambda b, h, off=off: (b, off + h))
    return pl.pallas_call(
        _attn_prompt_kernel,
        grid=(B_P, H_A // 2),
        in_specs=[blk(0), blk(W_A // 128), blk(2 * W_A // 128),
                  pl.BlockSpec((len(DILATIONS), 2, NK, 2 * NK), lambda b, h: (0, h, 0, 0))],
        out_specs=pl.BlockSpec((T_P, 128), lambda b, h: (b, h)),
        out_shape=jax.ShapeDtypeStruct((N_PROMPT, W_A), F32),
        scratch_shapes=[pltpu.VMEM((T_P, 128), F32)] * 3,
        compiler_params=_params(("arbitrary", "arbitrary")),
        name="attn_prompt",
    )(z, z, z, bias)


def _attn_sample_kernel(q_ref, kn_ref, vn_ref, ck_ref, cv_ref, bias_ref, o_ref, kf_ref, vf_ref):
    for f_ref, c_ref, n_ref in ((kf_ref, ck_ref, kn_ref), (vf_ref, cv_ref, vn_ref)):
        f_ref[0:BUF, :] = c_ref[...]
        f_ref[BUF:BUF + T_S, :] = n_ref[...]
        f_ref[BUF + T_S:KF_ROWS, :] = jnp.zeros((KF_ROWS - BUF - T_S, 128), F32)
    head0 = lax.broadcasted_iota(I32, (T_S, 128), 1) < HEAD_DIM
    q = q_ref[...] * (HEAD_DIM ** -0.5)
    q2 = jnp.concatenate([jnp.where(head0, q, 0.0), jnp.where(head0, 0.0, q)], axis=0).astype(BF16)
    bias2 = jnp.concatenate([bias_ref[0], bias_ref[1]], axis=0)
    s = _dot_nt(q2, kf_ref[...].astype(BF16)) + bias2
    m = jnp.max(s, axis=-1, keepdims=True)
    p = jnp.exp(s - m)
    l = jnp.sum(p, axis=-1, keepdims=True)
    o2 = _dot(p.astype(BF16), vf_ref[...].astype(BF16)) / l
    o_ref[...] = jnp.where(head0, o2[0:T_S], o2[T_S:2 * T_S])


def _attn_sample(z, cache_k, cache_v, layer, bias):
    row0 = N_PROMPT // T_S
    blk = lambda off: pl.BlockSpec((T_S, 128), lambda b, h, off=off: (row0 + b, off + h))
    cache = pl.BlockSpec((None, BUF, 128), lambda b, h: (layer * B_S + b, 0, h))
    return pl.pallas_call(
        _attn_sample_kernel,
        grid=(B_S, H_A // 2),
        in_specs=[blk(0), blk(W_A // 128), blk(2 * W_A // 128), cache, cache,
                  pl.BlockSpec((2, T_S, KF_ROWS), lambda b, h: (h, 0, 0))],
        out_specs=pl.BlockSpec((T_S, 128), lambda b, h: (b, h)),
        out_shape=jax.ShapeDtypeStruct((N_SAMPLE, W_A), F32),
        scratch_shapes=[pltpu.VMEM((KF_ROWS, 128), F32)] * 2,
        compiler_params=_params(("arbitrary", "arbitrary")),
        name="attn_sample",
    )(z, z, z, cache_k, cache_v, bias)


def _gmlp_kernel(ub_ref, vb_ref, gbv_ref, bbv_ref, ws_ref, bs_ref, gmix_ref, bo_ref, vn_ref):
    v = jax.nn.gelu(vb_ref[...])
    mu = jnp.mean(v, axis=-1, keepdims=True)
    var = jnp.mean(jnp.square(v - mu), axis=-1, keepdims=True)
    vn = (v - mu) * lax.rsqrt(var + EPS) * gbv_ref[...] + bbv_ref[...]
    vn_ref[...] = vn
    vnb = vn.astype(BF16)
    causal = (lax.broadcasted_iota(I32, (CHUNK, CHUNK), 1) <= lax.broadcasted_iota(I32, (CHUNK, CHUNK), 0))
    first = lax.broadcasted_iota(I32, (CHUNK, 128), 1) < W_B // G_B
    parts = []
    for gp in range(G_B // 2):
        v2 = vnb[:, gp * 128:(gp + 1) * 128]
        m0 = _dot(jnp.where(causal, ws_ref[2 * gp], 0.0).astype(BF16), v2)
        m1 = _dot(jnp.where(causal, ws_ref[2 * gp + 1], 0.0).astype(BF16), v2)
        parts.append(jnp.where(first, m0, m1))
    mix = jnp.concatenate(parts, axis=1) + bs_ref[...]
    bo_ref[...] = _rms(jax.nn.gelu(ub_ref[...]) * mix, gmix_ref[...])


def _gmlp(z, n_chunks, col_u, col_v, g_bv, b_bv, w_s, b_s, g_mix_b):
    bs_wide = jnp.repeat(b_s.T, W_B // G_B, axis=1)
    vec = pl.BlockSpec((1, W_B), lambda i: (0, 0))
    out = pl.BlockSpec((CHUNK, W_B), lambda i: (i, 0))
    return pl.pallas_call(
        _gmlp_kernel,
        grid=(n_chunks,),
        in_specs=[pl.BlockSpec((CHUNK, W_B), lambda i: (i, col_u)),
                  pl.BlockSpec((CHUNK, W_B), lambda i: (i, col_v)),
                  vec, vec,
                  pl.BlockSpec((G_B, CHUNK, CHUNK), lambda i: (0, 0, 0)),
                  pl.BlockSpec((CHUNK, W_B), lambda i: (0, 0)),
                  vec],
        out_specs=[out, out],
        out_shape=[jax.ShapeDtypeStruct((n_chunks * CHUNK, W_B), F32)] * 2,
        compiler_params=_params(("arbitrary",)),
        name="gmlp",
    )(z, z, g_bv.reshape(1, W_B), b_bv.reshape(1, W_B), w_s, bs_wide, g_mix_b.reshape(1, W_B))


def _cmul(a, b):
    return a[0] * b[0] - a[1] * b[1], a[0] * b[1] + a[1] * b[0]


def _s5_kernel(u_ref, lr_ref, li_ref, ldt_ref, btr_ref, bti_ref, cr_ref, ci_ref, d_ref, h0r_ref, h0i_ref,
               y_ref, hr_ref, hi_ref, er_ref, ei_ref, gr_ref, gi_ref, *, L, nj, nsc, nb):
    hi = lax.Precision.HIGHEST
    lr, li = lr_ref[...], li_ref[...]
    dt = jnp.exp(ldt_ref[...])
    mag = jnp.exp(lr * dt)
    a = (mag * jnp.cos(li * dt), mag * jnp.sin(li * dt))
    den = lr * lr + li * li
    q = (((a[0] - 1.0) * lr + a[1] * li) / den, (a[1] * lr - (a[0] - 1.0) * li) / den)
    same = ((lax.broadcasted_iota(I32, (128, S5_SW), 0) >> int(math.log2(SSM_GROUP)))
            == (lax.broadcasted_iota(I32, (128, S5_SW), 1) >> int(math.log2(SSM_STATE))))
    bb = _cmul(q, (jnp.where(same, btr_ref[...], 0.0), jnp.where(same, bti_ref[...], 0.0)))
    c = (jnp.where(same, cr_ref[...], 0.0), jnp.where(same, ci_ref[...], 0.0))
    pw = [(jnp.ones_like(lr), jnp.zeros_like(lr))]
    for _ in range(L):
        pw.append(_cmul(pw[-1], a))
    ck = [_cmul(c, p) for p in pw]
    kbd = [(_dot_nt(bb[0], ck[k][0], hi) - _dot_nt(bb[1], ck[k][1], hi)).astype(BF16) for k in range(L)]
    mins = [tuple(x.astype(BF16) for x in _cmul(pw[L - 1 - s], bb)) for s in range(L)]
    cob = [tuple(x.astype(BF16) for x in ck[i + 1]) for i in range(L)]

    us = [u_ref[:, s * 128:(s + 1) * 128] for s in range(L)]
    ub = [x.astype(BF16) for x in us]
    sr = _dot(ub[0], mins[0][0])
    si = _dot(ub[0], mins[0][1])
    for s in range(1, L):
        sr = sr + _dot(ub[s], mins[s][0])
        si = si + _dot(ub[s], mins[s][1])
    al = pw[L]
    r = nsc * nb
    xl = [(sr[0:r], si[0:r])]
    for j in range(1, nj):
        t = _cmul(al, xl[-1])
        xl.append((t[0] + sr[j * r:(j + 1) * r], t[1] + si[j * r:(j + 1) * r]))
    alp = [al]
    for j in range(1, nj):
        alp.append(_cmul(alp[-1], al))
    er_ref[...] = xl[-1][0]
    ei_ref[...] = xl[-1][1]
    g = (h0r_ref[...], h0i_ref[...])
    for sc in range(nsc):
        gr_ref[sc * nb:(sc + 1) * nb, :] = g[0]
        gi_ref[sc * nb:(sc + 1) * nb, :] = g[1]
        t = _cmul(alp[nj - 1], g)
        g = (t[0] + er_ref[sc * nb:(sc + 1) * nb, :], t[1] + ei_ref[sc * nb:(sc + 1) * nb, :])
    hr_ref[...] = g[0]
    hi_ref[...] = g[1]
    gcat = (gr_ref[...], gi_ref[...])
    hprev = [gcat]
    for j in range(1, nj):
        t = _cmul(alp[j - 1], gcat)
        hprev.append((xl[j - 1][0] + t[0], xl[j - 1][1] + t[1]))
    hpr = jnp.concatenate([x[0] for x in hprev], axis=0).astype(BF16)
    hpi = jnp.concatenate([x[1] for x in hprev], axis=0).astype(BF16)
    dsk = d_ref[...]
    for i in range(L):
        y = dsk * us[i] + _dot_nt(hpr, cob[i][0]) - _dot_nt(hpi, cob[i][1])
        for s in range(i + 1):
            y = y + _dot(ub[s], kbd[i - s])
        y_ref[:, i * 128:(i + 1) * 128] = y


def _s5(u, lw, h0r, h0i, L, nj, nsc, nb):
    m, n = u.shape[1], u.shape[2]
    lanes = lambda x: x.reshape(S5_NLB, 1, S5_SW)
    rep = lambda x: jnp.tile(x.reshape(S5_NLB, 128, SSM_STATE), (1, 1, S5_GPB))
    vec = pl.BlockSpec((None, 1, S5_SW), lambda g: (g, 0, 0))
    mat = pl.BlockSpec((None, 128, S5_SW), lambda g: (g, 0, 0))
    st = pl.BlockSpec((None, nb, S5_SW), lambda g: (g, 0, 0))
    return pl.pallas_call(
        functools.partial(_s5_kernel, L=L, nj=nj, nsc=nsc, nb=nb),
        grid=(S5_NLB,),
        in_specs=[pl.BlockSpec((None, m, n), lambda g: (g, 0, 0)), vec, vec, vec, mat, mat, mat, mat,
                  pl.BlockSpec((None, 1, 128), lambda g: (g, 0, 0)), st, st],
        out_specs=[pl.BlockSpec((None, m, n), lambda g: (g, 0, 0)), st, st],
        out_shape=[jax.ShapeDtypeStruct((S5_NLB, m, n), F32)] + [jax.ShapeDtypeStruct((S5_NLB, nb, S5_SW), F32)] * 2,
        scratch_shapes=[pltpu.VMEM((nsc * nb, S5_SW), F32)] * 4,
        compiler_params=_params(("arbitrary",)),
        name="s5",
    )(u, lanes(lw["lam_re"]), lanes(lw["lam_im"]), lanes(jnp.repeat(lw["log_dt"], SSM_STATE)),
      rep(lw["ssm_b_re"].transpose(0, 2, 1)), rep(lw["ssm_b_im"].transpose(0, 2, 1)),
      rep(lw["ssm_c_re"]), rep(lw["ssm_c_im"]), lw["ssm_d"].reshape(S5_NLB, 1, 128), h0r, h0i)


def _s5_post_kernel(y_ref, w_ref, b_ref, g_ref, o_ref):
    zz = jax.nn.gelu(y_ref[...])
    out = zz * jax.nn.sigmoid(_dot(zz.astype(BF16), w_ref[...]) + b_ref[...])
    o_ref[...] = _rms(out, g_ref[...])


def _s5_post(y, w_glu, b_glu, g_mix_c):
    n = y.shape[0]
    tt = min(n, TM)
    vec = pl.BlockSpec((1, W_C), lambda i: (0, 0))
    return pl.pallas_call(
        _s5_post_kernel,
        grid=(n // tt,),
        in_specs=[pl.BlockSpec((tt, W_C), lambda i: (i, 0)), pl.BlockSpec((W_C, W_C), lambda i: (0, 0)), vec, vec],
        out_specs=pl.BlockSpec((tt, W_C), lambda i: (i, 0)),
        out_shape=jax.ShapeDtypeStruct((n, W_C), F32),
        compiler_params=_params(("arbitrary",)),
        name="s5_post",
    )(y, w_glu.astype(BF16), b_glu.reshape(1, W_C), g_mix_c.reshape(1, W_C))


def _mix_kernel(ap_ref, bp_ref, cp_ref, as_ref, bs_ref, cs_ref, g_ref, o_ref):
    def emit(a_ref, b_ref, c_ref):
        o_ref[:, 0:W_A] = _rms(a_ref[...], g_ref[...]).astype(BF16)
        o_ref[:, W_A:W_A + W_B] = b_ref[...].astype(BF16)
        o_ref[:, W_A + W_B:D] = c_ref[...].astype(BF16)

    last = pl.program_id(0) == NT - 1

    @pl.when(jnp.logical_not(last))
    def _():
        emit(ap_ref, bp_ref, cp_ref)

    @pl.when(last)
    def _():
        emit(as_ref, bs_ref, cs_ref)


def _mix(a_p, b_p, c_p, a_s, b_s, c_s, g_mix_a):
    pr = lambda w: pl.BlockSpec((TM, w), lambda i: (jnp.minimum(i, NT - 2), 0))
    sm = lambda w: pl.BlockSpec((TM, w), lambda i: (0, 0))
    return pl.pallas_call(
        _mix_kernel,
        grid=(NT,),
        in_specs=[pr(W_A), pr(W_B), pr(W_C), sm(W_A), sm(W_B), sm(W_C), pl.BlockSpec((1, W_A), lambda i: (0, 0))],
        out_specs=pl.BlockSpec((TM, D), lambda i: (i, 0)),
        out_shape=jax.ShapeDtypeStruct((NP, D), BF16),
        compiler_params=_params(("arbitrary",)),
        name="mix",
    )(a_p, b_p, c_p, a_s, b_s, c_s, g_mix_a.reshape(1, W_A))


def _router_kernel(h_ref, w_ref, b_ref, ind_ref, wd_ref, cnt_ref):
    i = pl.program_id(0)
    scores = jax.nn.sigmoid(_dot(h_ref[...], w_ref[...]))
    lane = lax.broadcasted_iota(I32, (TD, N_EXPERTS), 1).astype(F32)
    row = lax.broadcasted_iota(I32, (TD, N_EXPERTS), 0) + i * TD
    sel = scores + b_ref[...]
    ind = jnp.zeros((TD, N_EXPERTS), F32)
    for _ in range(TOP_K):
        m = jnp.max(sel, axis=-1, keepdims=True)
        idx = jnp.min(jnp.where(sel == m, lane, float(N_EXPERTS)), axis=-1, keepdims=True)
        hit = lane == idx
        ind = jnp.where(hit, 1.0, ind)
        sel = jnp.where(hit, -jnp.inf, sel)
    ind = jnp.where(row < N_REAL, ind, 0.0)
    wsel = ind * scores
    wd_ref[...] = wsel / jnp.sum(wsel + (1.0 - jnp.max(ind, axis=-1, keepdims=True)) / N_EXPERTS,
                                 axis=-1, keepdims=True) * ROUTE_SCALE
    ind_ref[...] = ind

    @pl.when(i == 0)
    def _():
        cnt_ref[...] = jnp.zeros((8, N_EXPERTS), F32)

    cnt_ref[...] += jnp.sum(ind, axis=0, keepdims=True)


def _router(h2b, w_router, b_router):
    til = pl.BlockSpec((TD, N_EXPERTS), lambda i: (i, 0))
    return pl.pallas_call(
        _router_kernel,
        grid=(NP // TD,),
        in_specs=[pl.BlockSpec((TD, D), lambda i: (i, 0)), pl.BlockSpec((D, N_EXPERTS), lambda i: (0, 0)),
                  pl.BlockSpec((1, N_EXPERTS), lambda i: (0, 0))],
        out_specs=[til, til, pl.BlockSpec((8, N_EXPERTS), lambda i: (0, 0))],
        out_shape=[jax.ShapeDtypeStruct((NP, N_EXPERTS), F32)] * 2 + [jax.ShapeDtypeStruct((8, N_EXPERTS), F32)],
        compiler_params=_params(("arbitrary",)),
        name="router",
    )(h2b, w_router.astype(BF16), b_router.reshape(1, N_EXPERTS))


def _group_starts(cnt):
    padded = jnp.floor((cnt + (BM - 1)) / BM) * BM
    tri = (lax.broadcasted_iota(I32, (N_EXPERTS, N_EXPERTS), 0)
           <= lax.broadcasted_iota(I32, (N_EXPERTS, N_EXPERTS), 1)).astype(F32)
    pend = jnp.dot(padded, tri, preferred_element_type=F32, precision=lax.Precision.HIGHEST)
    return padded, pend - padded, pend


def _plan_kernel(cnt_ref, be_ref, misc_ref):
    cnt = cnt_ref[...]
    _, pstart, pend = _group_starts(cnt)
    rowi = (lax.broadcasted_iota(I32, (NBP, N_EXPERTS), 0) * BM).astype(F32)
    be = jnp.minimum(jnp.sum(jnp.where(pend[0:1] <= rowi, 1.0, 0.0), axis=-1, keepdims=True), N_EXPERTS - 1.0)
    be_ref[...] = jnp.broadcast_to(be, (NBP, 128)).astype(I32)
    misc_ref[0:8, :] = pstart.astype(I32)
    misc_ref[8:16, :] = cnt.astype(I32)
    misc_ref[16:24, :] = (pend / BM).astype(I32)


def _plan(cnt):
    return pl.pallas_call(
        _plan_kernel,
        out_shape=[jax.ShapeDtypeStruct((NBP, 128), I32), jax.ShapeDtypeStruct((24, N_EXPERTS), I32)],
        name="moe_plan",
    )(cnt)


def _slots_kernel(ind_ref, wd_ref, cnt_ref, dest_ref, w_ref, run_ref):
    @pl.when(pl.program_id(0) == 0)
    def _():
        run_ref[...] = jnp.zeros((8, N_EXPERTS), F32)

    ind = ind_ref[...]
    _, pstart, _ = _group_starts(cnt_ref[...])
    ltri = (lax.broadcasted_iota(I32, (TD, TD), 1) < lax.broadcasted_iota(I32, (TD, TD), 0)).astype(BF16)
    pos = _dot(ltri, ind.astype(BF16)) + run_ref[0:1, :]
    run_ref[...] += jnp.sum(ind, axis=0, keepdims=True)
    cur = jnp.where(ind > 0.0, pstart[0:1] + pos, BIG)
    lane = lax.broadcasted_iota(I32, (TD, 128), 1)
    dest = jnp.full((TD, 128), -1.0, F32)
    wts = jnp.zeros((TD, 128), F32)
    wd = wd_ref[...]
    for k in range(TOP_K):
        m = jnp.min(cur, axis=-1, keepdims=True)
        hit = cur == m
        wk = jnp.sum(jnp.where(hit, wd, 0.0), axis=-1, keepdims=True)
        dest = jnp.where(lane == k, jnp.where(m < BIG, m, -1.0), dest)
        wts = jnp.where(lane == k, jnp.where(m < BIG, wk, 0.0), wts)
        cur = jnp.where(hit, BIG, cur)
    dest_ref[...] = dest.astype(I32)
    w_ref[...] = wts


def _slots(ind, wd, cnt):
    til = pl.BlockSpec((TD, N_EXPERTS), lambda i: (i, 0))
    out = pl.BlockSpec((TD, 128), lambda i: (i, 0))
    return pl.pallas_call(
        _slots_kernel,
        grid=(NP // TD,),
        in_specs=[til, til, pl.BlockSpec((8, N_EXPERTS), lambda i: (0, 0))],
        out_specs=[out, out],
        out_shape=[jax.ShapeDtypeStruct((NP, 128), I32), jax.ShapeDtypeStruct((NP, 128), F32)],
        scratch_shapes=[pltpu.VMEM((8, N_EXPERTS), F32)],
        compiler_params=_params(("arbitrary",)),
        name="moe_slots",
    )(ind, wd, cnt)


def _dispatch_kernel(ps_ref, cnt_ref, nbu_ref, dest_ref, x_ref, xs_ref, zero_ref, sem):
    i = pl.program_id(0)

    def row_copy(r, k):
        d = dest_ref[r * 8 + k]
        return pltpu.make_async_copy(x_ref.at[pl.ds(r, 1), :], xs_ref.at[pl.ds(d, 1), :], sem)

    def zero_row(d):
        return pltpu.make_async_copy(zero_ref.at[pl.ds(0, 1), :], xs_ref.at[pl.ds(d, 1), :], sem)

    def zero_block(b):
        return pltpu.make_async_copy(zero_ref, xs_ref.at[pl.ds(b * BM, BM), :], sem)

    @pl.when(i < N_DTILES)
    def _():
        n_rows = jnp.minimum(TD, N_REAL - i * TD)

        def issue(r, c):
            for k in range(TOP_K):
                row_copy(r, k).start(priority=k % 2)
            return c

        def drain(r, c):
            for k in range(TOP_K):
                row_copy(r, k).wait()
            return c

        lax.fori_loop(0, n_rows, issue, 0)
        lax.fori_loop(0, n_rows, drain, 0)

    @pl.when(i == N_DTILES)
    def _():
        zero_ref[...] = jnp.zeros(zero_ref.shape, F32)

        def per_expert(e, c):
            cnt = cnt_ref[e]
            base = ps_ref[e]
            end = ((cnt + (BM - 1)) >> BM_SHIFT) << BM_SHIFT
            lax.fori_loop(cnt, end, lambda j, c2: (zero_row(base + j).start(), c2)[1], 0)
            lax.fori_loop(cnt, end, lambda j, c2: (zero_row(base + j).wait(), c2)[1], 0)
            return c

        lax.fori_loop(0, N_EXPERTS, per_expert, 0)
        lax.fori_loop(nbu_ref[0], NB, lambda b, c: (zero_block(b).start(), c)[1], 0)
        lax.fori_loop(nbu_ref[0], NB, lambda b, c: (zero_block(b).wait(), c)[1], 0)


def _dispatch(pstart_i, cnt_i, nbu, dest_flat, x):
    return pl.pallas_call(
        _dispatch_kernel,
        grid_spec=pltpu.PrefetchScalarGridSpec(
            num_scalar_prefetch=3, grid=(N_DTILES + 1,),
            in_specs=[pl.BlockSpec((TD * 8,), lambda i, *_: (jnp.minimum(i, N_DTILES - 1),),
                                   memory_space=pltpu.SMEM),
                      pl.BlockSpec((TD, D), lambda i, *_: (jnp.minimum(i, N_DTILES - 1), 0))],
            out_specs=pl.BlockSpec(memory_space=pl.ANY),
            scratch_shapes=[pltpu.VMEM((BM, D), F32), pltpu.SemaphoreType.DMA(())]),
        out_shape=jax.ShapeDtypeStruct((CAP, D), F32),
        compiler_params=_params(("arbitrary",)),
        name="moe_dispatch",
    )(pstart_i, cnt_i, nbu, dest_flat, x)


def _expert_kernel(be_ref, nbu_ref, x_ref, wg_ref, wu_ref, wd_ref, o_ref):
    i = pl.program_id(0)

    @pl.when(i < nbu_ref[0])
    def _():
        x = x_ref[...].astype(BF16)
        g = _dot(x, wg_ref[...].astype(BF16))
        u = _dot(x, wu_ref[...].astype(BF16))
        a = (g * jax.nn.sigmoid(g) * u).astype(BF16)
        o_ref[...] = _dot(a, wd_ref[...].astype(BF16))

    @pl.when(i >= nbu_ref[0])
    def _():
        o_ref[...] = jnp.zeros(o_ref.shape, F32)


def _experts(layer, be, nbu, xs, w_gate, w_up, w_down):
    blk = lambda i, be, nbu: (jnp.minimum(i, nbu[0] - 1), 0)
    wsel = lambda i, be, nbu: (layer, be[jnp.minimum(i, nbu[0] - 1)], 0, 0)
    return pl.pallas_call(
        _expert_kernel,
        grid_spec=pltpu.PrefetchScalarGridSpec(
            num_scalar_prefetch=2, grid=(NB,),
            in_specs=[pl.BlockSpec((BM, D), blk),
                      pl.BlockSpec((None, None, D, D_EXPERT), wsel),
                      pl.BlockSpec((None, None, D, D_EXPERT), wsel),
                      pl.BlockSpec((None, None, D_EXPERT, D), wsel)],
            out_specs=pl.BlockSpec((BM, D), lambda i, be, nbu: (i, 0))),
        out_shape=jax.ShapeDtypeStruct((CAP, D), F32),
        compiler_params=_params(("arbitrary",)),
        name="moe_experts",
    )(be, nbu, xs, w_gate, w_up, w_down)


def _shared_kernel(h_ref, wg_ref, wu_ref, wd_ref, o_ref):
    h = h_ref[...]
    g = _dot(h, wg_ref[...])
    u = _dot(h, wu_ref[...])
    o_ref[...] = _dot((g * jax.nn.sigmoid(g) * u).astype(BF16), wd_ref[...])


def _shared(h2b, wg, wu, wd):
    return pl.pallas_call(
        _shared_kernel,
        grid=(NT,),
        in_specs=[pl.BlockSpec((TM, D), lambda i: (i, 0)),
                  pl.BlockSpec((D, D_SHARED), lambda i: (0, 0)),
                  pl.BlockSpec((D, D_SHARED), lambda i: (0, 0)),
                  pl.BlockSpec((D_SHARED, D), lambda i: (0, 0))],
        out_specs=pl.BlockSpec((TM, D), lambda i: (i, 0)),
        out_shape=jax.ShapeDtypeStruct((NP, D), F32),
        compiler_params=_params(("arbitrary",)),
        name="moe_shared",
    )(h2b, wg.astype(BF16), wu.astype(BF16), wd.astype(BF16))


def _combine_kernel(dest_ref, nxt_ref, w_ref, ys_ref, sh_ref, x_ref, gr_ref, gt_ref, o_ref, rows_ref, sem):
    i = pl.program_id(0)
    n_tiles = NP // TC_
    slot = i % 2

    def gather(d_ref, s, start):
        def body(r, c):
            for k in range(TOP_K):
                d = jnp.maximum(d_ref[r * 8 + k], 0)
                cp = pltpu.make_async_copy(ys_ref.at[pl.ds(d, 1), :], rows_ref.at[s, k, pl.ds(r, 1), :], sem.at[s])
                if start:
                    cp.start(priority=k % 2)
                else:
                    cp.wait()
            return c

        lax.fori_loop(0, TC_, body, 0)

    @pl.when(i == 0)
    def _():
        gather(dest_ref, 0, True)

    @pl.when(i + 1 < n_tiles)
    def _():
        gather(nxt_ref, 1 - slot, True)

    gather(dest_ref, slot, False)
    w = w_ref[...]
    routed = w[:, 0:1] * rows_ref[slot, 0]
    for k in range(1, TOP_K):
        routed = routed + w[:, k:k + 1] * rows_ref[slot, k]
    last = i >= N_PROMPT // TC_
    o_ref[...] = x_ref[...] + _pick(last, gt_ref, gr_ref) * (routed + sh_ref[...])


def _combine(dest_flat, w6, ys, shared, x, gate):
    n_tiles = NP // TC_
    n_p = N_PROMPT // TC_
    per = TM // TC_
    til = pl.BlockSpec((TC_, D), lambda i: (i, 0))
    return pl.pallas_call(
        _combine_kernel,
        grid=(n_tiles,),
        in_specs=[pl.BlockSpec((TC_ * 8,), lambda i: (i,), memory_space=pltpu.SMEM),
                  pl.BlockSpec((TC_ * 8,), lambda i: (jnp.minimum(i + 1, n_tiles - 1),), memory_space=pltpu.SMEM),
                  pl.BlockSpec((TC_, 128), lambda i: (i, 0)),
                  pl.BlockSpec(memory_space=pl.ANY), til, til,
                  pl.BlockSpec((None, 1, D), lambda i: (i // per, 0, 0)),
                  pl.BlockSpec((TC_, D), lambda i: (jnp.maximum(i - n_p, 0), 0))],
        out_specs=til,
        out_shape=jax.ShapeDtypeStruct((NP, D), F32),
        scratch_shapes=[pltpu.VMEM((2, TOP_K, TC_, D), F32), pltpu.SemaphoreType.DMA((2,))],
        compiler_params=_params(("arbitrary",)),
        name="moe_combine",
    )(dest_flat, dest_flat, w6, ys, shared, x, gate[0], gate[1])


def _pad_rows(a, rows):
    return jnp.concatenate([a, jnp.zeros((rows - a.shape[0],) + a.shape[1:], a.dtype)], axis=0)


def _s5_layer(z, lw, h0r_s, h0i_s):
    g_mix_c = lw["g_mix"][W_A + W_B:]
    nj = nsc = 16
    state = lambda h, b: h.reshape(S5_NLB, b, S5_GPB, SSM_STATE).transpose(1, 0, 2, 3).reshape(b, G_C, SSM_STATE)
    uc_p = z[:N_PROMPT, IN_COLS - W_C:]
    u_p = uc_p.reshape(B_P, nsc, nj, S5_L_P, S5_NLB, 128).transpose(4, 2, 1, 0, 3, 5)
    u_p = u_p.reshape(S5_NLB, nj * nsc * B_P, S5_L_P * 128)
    zeros_h = jnp.zeros((S5_NLB, B_P, S5_SW), F32)
    y_p, hr_p, hi_p = _s5(u_p, lw, zeros_h, zeros_h, S5_L_P, nj, nsc, B_P)
    y_p = y_p.reshape(S5_NLB, nj, nsc, B_P, S5_L_P, 128).transpose(3, 2, 1, 4, 0, 5).reshape(N_PROMPT, W_C)
    c_p = _s5_post(y_p, lw["w_glu"], lw["b_glu"], g_mix_c)

    uc_s = z[N_PROMPT:N_REAL, IN_COLS - W_C:]
    u_s = uc_s.reshape(B_S, T_S, S5_NLB, 128).transpose(2, 0, 1, 3).reshape(S5_NLB, B_S, S5_L_S * 128)
    h0 = lambda h: h.reshape(B_S, S5_NLB, S5_SW).transpose(1, 0, 2)
    y_s, hr_s, hi_s = _s5(u_s, lw, h0(h0r_s), h0(h0i_s), S5_L_S, 1, 1, B_S)
    y_s = y_s.reshape(S5_NLB, B_S, T_S, 128).transpose(1, 2, 0, 3).reshape(N_SAMPLE, W_C)
    c_s = _s5_post(y_s, lw["w_glu"], lw["b_glu"], g_mix_c)
    return c_p, c_s, state(hr_p, B_P), state(hi_p, B_P), state(hr_s, B_S), state(hi_s, B_S)


def _layer(x, l, lw, experts, mod, bias_p, bias_s, cache_k, cache_v, h0r_s, h0i_s):
    sh1, sc1, g1, sh2, sc2, g2 = [_tile_tabs(m) for m in jnp.split(mod, 6, axis=-1)]
    (h,) = _modnorm(x, lw["g_attn"], sc1, sh1, (BF16,))
    z = _matmul(h, lw["w_in"].astype(BF16), 1536)

    a_p = _attn_prompt(z, bias_p)
    a_s = _attn_sample(z, cache_k, cache_v, l, bias_s)

    g_mix_b = lw["g_mix"][W_A:W_A + W_B]
    gm = (lw["g_bv"], lw["b_bv"], lw["w_s"], lw["b_s"], g_mix_b)
    b_p, _ = _gmlp(z, N_PROMPT // CHUNK, 3 * W_A // W_B, 3 * W_A // W_B + 1, *gm)
    uv_s = z[N_PROMPT:N_REAL, 3 * W_A:3 * W_A + 2 * W_B].reshape(B_S, T_S, 2 * W_B)
    uv_s = jnp.pad(uv_s, ((0, 0), (0, CHUNK - T_S), (0, 0))).reshape(B_S * CHUNK, 2 * W_B)
    b_s, vn_s = _gmlp(uv_s, B_S, 0, 1, *gm)
    b_s = b_s.reshape(B_S, CHUNK, W_B)[:, :T_S].reshape(N_SAMPLE, W_B)
    new_vb = vn_s.reshape(B_S, CHUNK, W_B)[:, :T_S]

    c_p, c_s, hr_p, hi_p, hr_s, hi_s = _s5_layer(z, lw, h0r_s, h0i_s)

    mixed = _mix(a_p, b_p, c_p, _pad_rows(a_s, TM), _pad_rows(b_s, TM), _pad_rows(c_s, TM), lw["g_mix"][:W_A])
    x = _matmul_residual(mixed, lw["w_out"].astype(BF16), x, g1, 1024)

    h2b, h2 = _modnorm(x, lw["g_ffn"], sc2, sh2, (BF16, F32))
    ind, wd, cnt = _router(h2b, lw["w_router"], lw["b_router"])
    be, misc = _plan(cnt)
    dest, w6 = _slots(ind, wd, cnt)
    dest_flat = dest[:, :8].reshape(-1)
    xs = _dispatch(misc[0], misc[8], misc[16, N_EXPERTS - 1:], dest_flat, h2)
    ys = _experts(l, be[:NB, 0], misc[16, N_EXPERTS - 1:], xs, *experts)
    shared = _shared(h2b, lw["w_sh_gate"], lw["w_sh_up"], lw["w_sh_down"])
    x = _combine(dest_flat, w6, ys, shared, x, g2)

    kv = lambda lo: (z[:N_PROMPT, lo:lo + W_A].reshape(B_P, T_P, H_A, HEAD_DIM)[:, T_P - BUF:],
                     z[N_PROMPT:N_REAL, lo:lo + W_A].reshape(B_S, T_S, H_A, HEAD_DIM))
    (k_p, k_s), (v_p, v_s) = kv(W_A), kv(2 * W_A)
    return x, (k_p, v_p, k_s, v_s, new_vb, hr_p, hi_p, hr_s, hi_s)


def kernel(x_prompt, x_sample, cache_a_k, cache_a_v, state_c_re, state_c_im, c_prompt, c_sample, rel_bias, w_ada, b_ada, g_attn, w_in, g_bv, b_bv, w_s, b_s, lam_re, lam_im, log_dt, ssm_b_re, ssm_b_im, ssm_c_re, ssm_c_im, ssm_d, w_glu, b_glu, g_mix, w_out, g_ffn, w_router, b_router, w_gate, w_up, w_down, w_sh_gate, w_sh_up, w_sh_down, g_final):
    assert x_prompt.shape == (B_P, T_P, D) and x_sample.shape == (B_S, T_S, D)
    assert cache_a_k.shape == (DEPTH, B_S, BUF, H_A, HEAD_DIM)
    per_layer = dict(g_attn=g_attn, w_in=w_in, g_bv=g_bv, b_bv=b_bv, w_s=w_s, b_s=b_s, lam_re=lam_re,
                     lam_im=lam_im, log_dt=log_dt, ssm_b_re=ssm_b_re, ssm_b_im=ssm_b_im, ssm_c_re=ssm_c_re,
                     ssm_c_im=ssm_c_im, ssm_d=ssm_d, w_glu=w_glu, b_glu=b_glu, g_mix=g_mix, w_out=w_out,
                     g_ffn=g_ffn, w_router=w_router, b_router=b_router, w_sh_gate=w_sh_gate, w_sh_up=w_sh_up,
                     w_sh_down=w_sh_down)

    x = _pad_rows(jnp.concatenate([x_prompt.reshape(N_PROMPT, D), x_sample.reshape(N_SAMPLE, D)], axis=0), NP)
    c_all = _pad_rows(jnp.concatenate([c_prompt, c_sample], axis=0), 16)
    mod = _mod_all(c_all, w_ada, b_ada)
    bias_p = _bias_bank(rel_bias, *_prompt_bias_tables())
    bias_s = _bias_bank(rel_bias, *_sample_bias_tables())[0]
    cache_k = cache_a_k.reshape(DEPTH * B_S, BUF, W_A)
    cache_v = cache_a_v.reshape(DEPTH * B_S, BUF, W_A)

    outs = []
    for l in range(DEPTH):
        lw = {name: w[l] for name, w in per_layer.items()}
        x, o = _layer(x, l, lw, (w_gate, w_up, w_down), mod[l], bias_p, bias_s, cache_k, cache_v,
                      state_c_re[l], state_c_im[l])
        outs.append(o)
    y = _finalnorm(x, g_final)
    stacked = [jnp.stack([o[j] for o in outs]) for j in range(9)]
    return (y[:N_PROMPT].reshape(B_P, T_P, D), y[N_PROMPT:N_REAL].reshape(B_S, T_S, D), *stacked)
```

```python
import functools
import math

import jax
import jax.numpy as jnp
import numpy as np
from jax import lax
from jax.experimental import pallas as pl
from jax.experimental.pallas import tpu as pltpu

F32 = jnp.float32
BF16 = jnp.bfloat16
I32 = jnp.int32

D = 2048
B_P, T_P = 2, 4096
B_S, T_S = 8, 8
DEPTH = 2
HEAD_DIM = 64
W_A = 1024
H_A = 16
W_B = 512
G_B = 8
W_C = 512
SSM_GROUP = 16
G_C = 32
SSM_STATE = 64
IN_COLS = 3 * W_A + 2 * W_B + W_C
DILATIONS = (1, 4, 16)
NK = 128
ATTN_UNITS = (8, 4, 4)
BUF = 2048
CHUNK = 128
N_BUCKETS = 32
REL_MAX_DIST = 2048
N_EXPERTS = 64
TOP_K = 6
D_EXPERT = 512
D_SHARED = 1024
ROUTE_SCALE = 2.5
EPS = 1e-6

N_PROMPT = B_P * T_P
N_SAMPLE = B_S * T_S
N_REAL = N_PROMPT + N_SAMPLE
TM = 512
NT = N_PROMPT // TM + 1
NP = NT * TM
TD = 256
N_DTILES = -(-N_REAL // TD)
ROW_TILES = D // 128
BM = 256
BM_SHIFT = 8
NB = -(-(N_REAL * TOP_K + N_EXPERTS * (BM - 1)) // BM)
NBP = -(-NB // 8) * 8
CAP = NB * BM
TC_ = 128
KF_ROWS = 2176
S5_L_P = 16
S5_L_S = T_S
S5_GPB = 128 // SSM_GROUP
S5_NLB = G_C // S5_GPB
S5_SW = S5_GPB * SSM_STATE
BIG = 1e9
assert BM == 1 << BM_SHIFT

VMEM_LIMIT_BYTES = 56 * 1024 * 1024


def _params(dims):
    return pltpu.CompilerParams(dimension_semantics=dims, vmem_limit_bytes=VMEM_LIMIT_BYTES)


def _dot(a, b):
    return jnp.dot(a, b, preferred_element_type=F32)


def _dot_nt(a, b, precision=None):
    return lax.dot_general(a, b, (((1,), (1,)), ((), ())), preferred_element_type=F32,
                           precision=precision)


def _rms(x, g):
    return x * lax.rsqrt(jnp.mean(x * x, axis=-1, keepdims=True) + EPS) * g


def _mod_kernel(c_ref, w_ref, b_ref, o_ref):
    c = c_ref[...]
    s = (c * jax.nn.sigmoid(c)).astype(BF16)
    o_ref[...] = _dot(s, w_ref[...].astype(BF16)) + b_ref[...]


def _mod_all(c_all, w_ada, b_ada):
    tn = 1024
    return pl.pallas_call(
        _mod_kernel,
        grid=(DEPTH, 6 * D // tn),
        in_specs=[pl.BlockSpec((16, D), lambda l, j: (0, 0)),
                  pl.BlockSpec((None, D, tn), lambda l, j: (l, 0, j)),
                  pl.BlockSpec((None, 1, tn), lambda l, j: (l, 0, j))],
        out_specs=pl.BlockSpec((None, 16, tn), lambda l, j: (l, 0, j)),
        out_shape=jax.ShapeDtypeStruct((DEPTH, 16, 6 * D), F32),
        compiler_params=_params(("arbitrary", "arbitrary")),
        name="mod",
    )(c_all, w_ada, b_ada.reshape(DEPTH, 1, 6 * D))


def _tile_tabs(m):
    row = jnp.concatenate([jnp.repeat(m[0:B_P], NT // B_P, axis=0), jnp.zeros((1, m.shape[1]), F32)], 0)
    tok = jnp.concatenate([jnp.repeat(m[B_P:B_P + B_S], T_S, axis=0),
                           jnp.zeros((TM - N_SAMPLE, m.shape[1]), F32)], 0)
    return row[:, None, :], tok


def _pick(last, tok_ref, row_ref):
    return jnp.where(last, tok_ref[...], row_ref[...])


def _modnorm_kernel(x_ref, g_ref, scr_ref, sct_ref, shr_ref, sht_ref, *o_refs):
    last = pl.program_id(0) == NT - 1
    y = _rms(x_ref[...], g_ref[...])
    y = y * (1.0 + _pick(last, sct_ref, scr_ref)) + _pick(last, sht_ref, shr_ref)
    o_refs[0][...] = y.astype(BF16)
    if len(o_refs) > 1:
        _store_token_rows(o_refs[1], y)


def _load_token_rows(ref, n):
    return jnp.concatenate([ref[pl.ds(j, n, stride=ROW_TILES), :] for j in range(ROW_TILES)], axis=1)


def _store_token_rows(ref, val):
    for j in range(ROW_TILES):
        ref[pl.ds(j, val.shape[0], stride=ROW_TILES), :] = val[:, j * 128:(j + 1) * 128]


def _modnorm(x, g, sc, sh, with_token_rows):
    row = pl.BlockSpec((None, 1, D), lambda i: (i, 0, 0))
    tok = pl.BlockSpec((TM, D), lambda i: (0, 0))
    til = pl.BlockSpec((TM, D), lambda i: (i, 0))
    out_specs, out_shape = [til], [jax.ShapeDtypeStruct((NP, D), BF16)]
    if with_token_rows:
        out_specs.append(pl.BlockSpec((TM * ROW_TILES, 128), lambda i: (i, 0)))
        out_shape.append(jax.ShapeDtypeStruct((NP * ROW_TILES, 128), F32))
    return pl.pallas_call(
        _modnorm_kernel,
        grid=(NT,),
        in_specs=[til, pl.BlockSpec((1, D), lambda i: (0, 0)), row, tok, row, tok],
        out_specs=out_specs,
        out_shape=out_shape,
        compiler_params=_params(("arbitrary",)),
        name="modnorm",
    )(x, g.reshape(1, D), sc[0], sc[1], sh[0], sh[1])


def _finalnorm_kernel(x_ref, g_ref, o_ref):
    o_ref[...] = _rms(x_ref[...], g_ref[...])


def _finalnorm(x, g):
    til = pl.BlockSpec((TM, D), lambda i: (i, 0))
    return pl.pallas_call(
        _finalnorm_kernel,
        grid=(NT,),
        in_specs=[til, pl.BlockSpec((1, D), lambda i: (0, 0))],
        out_specs=til,
        out_shape=jax.ShapeDtypeStruct((NP, D), F32),
        compiler_params=_params(("arbitrary",)),
        name="finalnorm",
    )(x, g.reshape(1, D))


def _mm_kernel(a_ref, b_ref, o_ref):
    o_ref[...] = _dot(a_ref[...], b_ref[...])


def _matmul(a, b, tn):
    k, n = b.shape
    return pl.pallas_call(
        _mm_kernel,
        grid=(n // tn, NT),
        in_specs=[pl.BlockSpec((TM, k), lambda c, i: (i, 0)),
                  pl.BlockSpec((k, tn), lambda c, i: (0, c))],
        out_specs=pl.BlockSpec((TM, tn), lambda c, i: (i, c)),
        out_shape=jax.ShapeDtypeStruct((NP, n), F32),
        compiler_params=_params(("arbitrary", "arbitrary")),
        name="matmul",
    )(a, b)


def _mm_res_kernel(a_ref, b_ref, x_ref, gr_ref, gt_ref, o_ref):
    last = pl.program_id(1) == NT - 1
    o_ref[...] = x_ref[...] + _pick(last, gt_ref, gr_ref) * _dot(a_ref[...], b_ref[...])


def _matmul_residual(a, b, x, gate, tn):
    k, n = b.shape
    return pl.pallas_call(
        _mm_res_kernel,
        grid=(n // tn, NT),
        in_specs=[pl.BlockSpec((TM, k), lambda c, i: (i, 0)),
                  pl.BlockSpec((k, tn), lambda c, i: (0, c)),
                  pl.BlockSpec((TM, tn), lambda c, i: (i, c)),
                  pl.BlockSpec((None, 1, tn), lambda c, i: (i, 0, c)),
                  pl.BlockSpec((TM, tn), lambda c, i: (0, c))],
        out_specs=pl.BlockSpec((TM, tn), lambda c, i: (i, c)),
        out_shape=jax.ShapeDtypeStruct((NP, n), F32),
        compiler_params=_params(("arbitrary", "arbitrary")),
        name="matmul_residual",
    )(a, b, x, gate[0], gate[1])


def _t5_bucket(dist):
    max_exact = N_BUCKETS // 2
    dist = jnp.maximum(dist, 0)
    ratio = jnp.log(jnp.maximum(dist, 1).astype(F32) / max_exact) / math.log(REL_MAX_DIST / max_exact)
    large = jnp.minimum(max_exact + (ratio * (N_BUCKETS - max_exact)).astype(I32), N_BUCKETS - 1)
    return jnp.where(dist < max_exact, dist, large)


def _bias_kernel(rb_ref, bucket_ref, add_ref, o_ref):
    h = pl.program_id(1)
    bucket = bucket_ref[...]
    acc = jnp.full(bucket.shape, -jnp.inf, F32)
    for b in range(N_BUCKETS):
        acc = jnp.where(bucket == b, rb_ref[b * H_A + h], acc)
    o_ref[...] = acc + add_ref[...]


def _bias_bank(rel_bias, bucket, add):
    n, r, c = bucket.shape
    blk = pl.BlockSpec((None, r, c), lambda i, h, rb: (i, 0, 0))
    return pl.pallas_call(
        _bias_kernel,
        grid_spec=pltpu.PrefetchScalarGridSpec(
            num_scalar_prefetch=1, grid=(n, H_A), in_specs=[blk, blk],
            out_specs=pl.BlockSpec((None, None, r, c), lambda i, h, rb: (i, h, 0, 0))),
        out_shape=jax.ShapeDtypeStruct((n, H_A, r, c), F32),
        compiler_params=_params(("arbitrary", "arbitrary")),
        name="bias_bank",
    )(rel_bias.reshape(-1), bucket, add)


def _prompt_bias_tables():
    qi = jnp.arange(NK)[:, None]
    kj = jnp.arange(2 * NK)[None, :]
    rel = qi + NK - kj
    band = (rel >= 0) & (rel <= NK)
    bucket = jnp.stack([jnp.where(band, _t5_bucket(rel * d), -1) for d in DILATIONS]).astype(I32)
    return bucket, jnp.zeros(bucket.shape, F32)


def _sample_bias_tables():
    dist_np = BUF + np.arange(T_S)[:, None] - np.arange(KF_ROWS)[None, :]
    mult = np.zeros(dist_np.shape, np.float64)
    for d in DILATIONS:
        mult += (dist_np >= 0) & (dist_np % d == 0) & (dist_np // d <= NK)
    mult = np.where(np.arange(KF_ROWS)[None, :] < BUF + T_S, mult, 0.0)
    valid = mult > 0
    add = np.where(valid, np.log(np.maximum(mult, 1.0)), 0.0).astype(np.float32)
    bucket = jnp.where(jnp.asarray(valid), _t5_bucket(jnp.asarray(dist_np, I32)), -1).astype(I32)
    return bucket[None], jnp.asarray(add)[None]


def _attn_prompt_kernel(q_ref, k_ref, v_ref, bias_ref, o_ref, acc_ref, m_ref, l_ref):
    lane = lax.broadcasted_iota(I32, (NK, 128), 1)
    head0 = lane < HEAD_DIM
    prev_cols = lax.broadcasted_iota(I32, (NK, 2 * NK), 1) < NK
    ones = jnp.ones((NK, 128), F32)
    order = tuple(range(len(DILATIONS)))
    for di in order:
        d = DILATIONS[di]
        n_blocks = T_P // (d * NK)
        n_units = ATTN_UNITS[di]
        run = min(n_units, n_blocks)
        assert n_units % run == 0 and n_blocks % run == 0
        first, final = di == order[0], di == order[-1]

        def group(gi, carry, di=di, d=d, n_blocks=n_blocks, n_units=n_units, run=run, first=first, final=final):
            def idx(start):
                return pl.ds(start, NK) if d == 1 else pl.ds(start, NK, stride=d)

            def keys(start):
                return k_ref[idx(start), :].astype(BF16)

            def values(start):
                return jnp.concatenate([v_ref[idx(start), :], ones], axis=1).astype(BF16)

            units = []
            for j in range(n_units):
                u = gi * n_units + j
                r = u >> int(math.log2(n_blocks))
                n = u & (n_blocks - 1)
                cur = r + d * NK * n
                if j % run == 0:
                    prv = r + d * NK * jnp.maximum(n - 1, 0)
                    k_prev, v_prev = keys(prv), values(prv)
                    pen = jnp.where(n == 0, -jnp.inf, 0.0).astype(F32)
                else:
                    k_prev, v_prev, pen = k_cur, v_cur, None
                k_cur, v_cur = keys(cur), values(cur)
                unit = dict(cur=cur, pen=pen, q=q_ref[idx(cur), :] * (HEAD_DIM ** -0.5),
                            k2=jnp.concatenate([k_prev, k_cur], axis=0),
                            v2=jnp.concatenate([v_prev, v_cur], axis=0))
                if not first:
                    unit.update(m=m_ref[idx(cur), :], l=l_ref[idx(cur), :], a=acc_ref[idx(cur), :])
                units.append(unit)
            results = []
            for unit in units:
                pv, rs, mn = [], [], []
                for h in range(2):
                    qh = jnp.where(head0 if h == 0 else ~head0, unit["q"], 0.0).astype(BF16)
                    s = _dot_nt(qh, unit["k2"]) + bias_ref[di, h]
                    if unit["pen"] is not None:
                        s = s + jnp.where(prev_cols, unit["pen"], 0.0)
                    m_h = jnp.max(s, axis=-1, keepdims=True)
                    if not first:
                        m_h = jnp.maximum(m_h, unit["m"][:, h * HEAD_DIM:h * HEAD_DIM + 1])
                    pv_rs = _dot(jnp.exp(s - m_h).astype(BF16), unit["v2"])
                    pv.append(pv_rs[:, 0:128])
                    rs.append(pv_rs[:, 128:256])
                    mn.append(m_h)
                m_new = jnp.where(head0, mn[0], mn[1])
                l_new = jnp.where(head0, rs[0], rs[1])
                a_new = jnp.where(head0, pv[0], pv[1])
                if not first:
                    alpha = jnp.exp(unit["m"] - m_new)
                    l_new = alpha * unit["l"] + l_new
                    a_new = alpha * unit["a"] + a_new
                results.append((unit["cur"], m_new, l_new, a_new))
            for cur, m_new, l_new, a_new in results:
                if final:
                    o_ref[idx(cur), :] = a_new / l_new
                else:
                    m_ref[idx(cur), :] = m_new
                    l_ref[idx(cur), :] = l_new
                    acc_ref[idx(cur), :] = a_new
            return carry

        lax.fori_loop(0, T_P // NK // n_units, group, 0)


def _attn_prompt(z, bias):
    blk = lambda off: pl.BlockSpec((T_P, 128), lambda b, h, off=off: (b, off + h))
    return pl.pallas_call(
        _attn_prompt_kernel,
        grid=(B_P, H_A // 2),
        in_specs=[blk(0), blk(W_A // 128), blk(2 * W_A // 128),
                  pl.BlockSpec((len(DILATIONS), 2, NK, 2 * NK), lambda b, h: (0, h, 0, 0))],
        out_specs=pl.BlockSpec((T_P, 128), lambda b, h: (b, h)),
        out_shape=jax.ShapeDtypeStruct((N_PROMPT, W_A), F32),
        scratch_shapes=[pltpu.VMEM((T_P, 128), F32)] * 3,
        compiler_params=_params(("arbitrary", "arbitrary")),
        name="attn_prompt",
    )(z, z, z, bias)


def _attn_sample_kernel(q_ref, kn_ref, vn_ref, ck_ref, cv_ref, bias_ref, o_ref, kf_ref, vf_ref):
    for f_ref, c_ref, n_ref in ((kf_ref, ck_ref, kn_ref), (vf_ref, cv_ref, vn_ref)):
        f_ref[0:BUF, :] = c_ref[...]
        f_ref[BUF:BUF + T_S, :] = n_ref[...]
        f_ref[BUF + T_S:KF_ROWS, :] = jnp.zeros((KF_ROWS - BUF - T_S, 128), F32)
    head0 = lax.broadcasted_iota(I32, (T_S, 128), 1) < HEAD_DIM
    q = q_ref[...] * (HEAD_DIM ** -0.5)
    q2 = jnp.concatenate([jnp.where(head0, q, 0.0), jnp.where(head0, 0.0, q)], axis=0).astype(BF16)
    bias2 = jnp.concatenate([bias_ref[0], bias_ref[1]], axis=0)
    s = _dot_nt(q2, kf_ref[...].astype(BF16)) + bias2
    m = jnp.max(s, axis=-1, keepdims=True)
    p = jnp.exp(s - m)
    l = jnp.sum(p, axis=-1, keepdims=True)
    o2 = _dot(p.astype(BF16), vf_ref[...].astype(BF16)) / l
    o_ref[...] = jnp.where(head0, o2[0:T_S], o2[T_S:2 * T_S])


def _attn_sample(z, cache_k, cache_v, layer, bias):
    row0 = N_PROMPT // T_S
    blk = lambda off: pl.BlockSpec((T_S, 128), lambda b, h, off=off: (row0 + b, off + h))
    cache = pl.BlockSpec((None, BUF, 128), lambda b, h: (layer * B_S + b, 0, h))
    return pl.pallas_call(
        _attn_sample_kernel,
        grid=(B_S, H_A // 2),
        in_specs=[blk(0), blk(W_A // 128), blk(2 * W_A // 128), cache, cache,
                  pl.BlockSpec((2, T_S, KF_ROWS), lambda b, h: (h, 0, 0))],
        out_specs=pl.BlockSpec((T_S, 128), lambda b, h: (b, h)),
        out_shape=jax.ShapeDtypeStruct((N_SAMPLE, W_A), F32),
        scratch_shapes=[pltpu.VMEM((KF_ROWS, 128), F32)] * 2,
        compiler_params=_params(("arbitrary", "arbitrary")),
        name="attn_sample",
    )(z, z, z, cache_k, cache_v, bias)


def _gmlp_kernel(ub_ref, vb_ref, gbv_ref, bbv_ref, ws_ref, bs_ref, gmix_ref, bo_ref, vn_ref):
    v = jax.nn.gelu(vb_ref[...])
    mu = jnp.mean(v, axis=-1, keepdims=True)
    var = jnp.mean(jnp.square(v - mu), axis=-1, keepdims=True)
    vn = (v - mu) * lax.rsqrt(var + EPS) * gbv_ref[...] + bbv_ref[...]
    vn_ref[...] = vn
    vnb = vn.astype(BF16)
    causal = (lax.broadcasted_iota(I32, (CHUNK, CHUNK), 1) <= lax.broadcasted_iota(I32, (CHUNK, CHUNK), 0))
    first = lax.broadcasted_iota(I32, (CHUNK, 128), 1) < W_B // G_B
    parts = []
    for gp in range(G_B // 2):
        v2 = vnb[:, gp * 128:(gp + 1) * 128]
        m0 = _dot(jnp.where(causal, ws_ref[2 * gp], 0.0).astype(BF16), v2)
        m1 = _dot(jnp.where(causal, ws_ref[2 * gp + 1], 0.0).astype(BF16), v2)
        parts.append(jnp.where(first, m0, m1))
    mix = jnp.concatenate(parts, axis=1) + bs_ref[...]
    bo_ref[...] = _rms(jax.nn.gelu(ub_ref[...]) * mix, gmix_ref[...])


def _gmlp(z, n_chunks, col_u, col_v, g_bv, b_bv, w_s, b_s, g_mix_b):
    bs_wide = jnp.repeat(b_s.T, W_B // G_B, axis=1)
    vec = pl.BlockSpec((1, W_B), lambda i: (0, 0))
    out = pl.BlockSpec((CHUNK, W_B), lambda i: (i, 0))
    return pl.pallas_call(
        _gmlp_kernel,
        grid=(n_chunks,),
        in_specs=[pl.BlockSpec((CHUNK, W_B), lambda i: (i, col_u)),
                  pl.BlockSpec((CHUNK, W_B), lambda i: (i, col_v)),
                  vec, vec,
                  pl.BlockSpec((G_B, CHUNK, CHUNK), lambda i: (0, 0, 0)),
                  pl.BlockSpec((CHUNK, W_B), lambda i: (0, 0)),
                  vec],
        out_specs=[out, out],
        out_shape=[jax.ShapeDtypeStruct((n_chunks * CHUNK, W_B), F32)] * 2,
        compiler_params=_params(("arbitrary",)),
        name="gmlp",
    )(z, z, g_bv.reshape(1, W_B), b_bv.reshape(1, W_B), w_s, bs_wide, g_mix_b.reshape(1, W_B))


def _cmul(a, b):
    return a[0] * b[0] - a[1] * b[1], a[0] * b[1] + a[1] * b[0]


def _s5_kernel(u_ref, lr_ref, li_ref, ldt_ref, btr_ref, bti_ref, cr_ref, ci_ref, d_ref, h0r_ref, h0i_ref,
               y_ref, hr_ref, hi_ref, er_ref, ei_ref, gr_ref, gi_ref, *, L, nj, nsc, nb):
    hi = lax.Precision.HIGHEST
    lr, li = lr_ref[...], li_ref[...]
    dt = jnp.exp(ldt_ref[...])
    mag = jnp.exp(lr * dt)
    a = (mag * jnp.cos(li * dt), mag * jnp.sin(li * dt))
    den = lr * lr + li * li
    q = (((a[0] - 1.0) * lr + a[1] * li) / den, (a[1] * lr - (a[0] - 1.0) * li) / den)
    same = ((lax.broadcasted_iota(I32, (128, S5_SW), 0) >> int(math.log2(SSM_GROUP)))
            == (lax.broadcasted_iota(I32, (128, S5_SW), 1) >> int(math.log2(SSM_STATE))))
    bb = _cmul(q, (jnp.where(same, btr_ref[...], 0.0), jnp.where(same, bti_ref[...], 0.0)))
    c = (jnp.where(same, cr_ref[...], 0.0), jnp.where(same, ci_ref[...], 0.0))
    pw = [(jnp.ones_like(lr), jnp.zeros_like(lr))]
    for _ in range(L):
        pw.append(_cmul(pw[-1], a))
    ck = [_cmul(c, p) for p in pw]
    kbd = [(_dot_nt(bb[0], ck[k][0], hi) - _dot_nt(bb[1], ck[k][1], hi)).astype(BF16) for k in range(L)]
    mins = [tuple(x.astype(BF16) for x in _cmul(pw[L - 1 - s], bb)) for s in range(L)]
    cob = [tuple(x.astype(BF16) for x in ck[i + 1]) for i in range(L)]

    us = [u_ref[:, s * 128:(s + 1) * 128] for s in range(L)]
    ub = [x.astype(BF16) for x in us]
    sr = _dot(ub[0], mins[0][0])
    si = _dot(ub[0], mins[0][1])
    for s in range(1, L):
        sr = sr + _dot(ub[s], mins[s][0])
        si = si + _dot(ub[s], mins[s][1])
    al = pw[L]
    r = nsc * nb
    xl = [(sr[0:r], si[0:r])]
    for j in range(1, nj):
        t = _cmul(al, xl[-1])
        xl.append((t[0] + sr[j * r:(j + 1) * r], t[1] + si[j * r:(j + 1) * r]))
    alp = [al]
    for j in range(1, nj):
        alp.append(_cmul(alp[-1], al))
    er_ref[...] = xl[-1][0]
    ei_ref[...] = xl[-1][1]
    g = (h0r_ref[...], h0i_ref[...])
    for sc in range(nsc):
        gr_ref[sc * nb:(sc + 1) * nb, :] = g[0]
        gi_ref[sc * nb:(sc + 1) * nb, :] = g[1]
        t = _cmul(alp[nj - 1], g)
        g = (t[0] + er_ref[sc * nb:(sc + 1) * nb, :], t[1] + ei_ref[sc * nb:(sc + 1) * nb, :])
    hr_ref[...] = g[0]
    hi_ref[...] = g[1]
    gcat = (gr_ref[...], gi_ref[...])
    hprev = [gcat]
    for j in range(1, nj):
        t = _cmul(alp[j - 1], gcat)
        hprev.append((xl[j - 1][0] + t[0], xl[j - 1][1] + t[1]))
    hpr = jnp.concatenate([x[0] for x in hprev], axis=0).astype(BF16)
    hpi = jnp.concatenate([x[1] for x in hprev], axis=0).astype(BF16)
    dsk = d_ref[...]
    for i in range(L):
        y = dsk * us[i] + _dot_nt(hpr, cob[i][0]) - _dot_nt(hpi, cob[i][1])
        for s in range(i + 1):
            y = y + _dot(ub[s], kbd[i - s])
        y_ref[:, i * 128:(i + 1) * 128] = y


def _s5(u, lw, h0r, h0i, L, nj, nsc, nb):
    m, n = u.shape[1], u.shape[2]
    lanes = lambda x: x.reshape(S5_NLB, 1, S5_SW)
    rep = lambda x: jnp.tile(x.reshape(S5_NLB, 128, SSM_STATE), (1, 1, S5_GPB))
    vec = pl.BlockSpec((None, 1, S5_SW), lambda g: (g, 0, 0))
    mat = pl.BlockSpec((None, 128, S5_SW), lambda g: (g, 0, 0))
    st = pl.BlockSpec((None, nb, S5_SW), lambda g: (g, 0, 0))
    return pl.pallas_call(
        functools.partial(_s5_kernel, L=L, nj=nj, nsc=nsc, nb=nb),
        grid=(S5_NLB,),
        in_specs=[pl.BlockSpec((None, m, n), lambda g: (g, 0, 0)), vec, vec, vec, mat, mat, mat, mat,
                  pl.BlockSpec((None, 1, 128), lambda g: (g, 0, 0)), st, st],
        out_specs=[pl.BlockSpec((None, m, n), lambda g: (g, 0, 0)), st, st],
        out_shape=[jax.ShapeDtypeStruct((S5_NLB, m, n), F32)] + [jax.ShapeDtypeStruct((S5_NLB, nb, S5_SW), F32)] * 2,
        scratch_shapes=[pltpu.VMEM((nsc * nb, S5_SW), F32)] * 4,
        compiler_params=_params(("arbitrary",)),
        name="s5",
    )(u, lanes(lw["lam_re"]), lanes(lw["lam_im"]), lanes(jnp.repeat(lw["log_dt"], SSM_STATE)),
      rep(lw["ssm_b_re"].transpose(0, 2, 1)), rep(lw["ssm_b_im"].transpose(0, 2, 1)),
      rep(lw["ssm_c_re"]), rep(lw["ssm_c_im"]), lw["ssm_d"].reshape(S5_NLB, 1, 128), h0r, h0i)


def _s5_post_kernel(y_ref, w_ref, b_ref, g_ref, o_ref):
    zz = jax.nn.gelu(y_ref[...])
    out = zz * jax.nn.sigmoid(_dot(zz.astype(BF16), w_ref[...]) + b_ref[...])
    o_ref[...] = _rms(out, g_ref[...])


def _s5_post(y, w_glu, b_glu, g_mix_c):
    n = y.shape[0]
    tt = min(n, TM)
    vec = pl.BlockSpec((1, W_C), lambda i: (0, 0))
    return pl.pallas_call(
        _s5_post_kernel,
        grid=(n // tt,),
        in_specs=[pl.BlockSpec((tt, W_C), lambda i: (i, 0)), pl.BlockSpec((W_C, W_C), lambda i: (0, 0)), vec, vec],
        out_specs=pl.BlockSpec((tt, W_C), lambda i: (i, 0)),
        out_shape=jax.ShapeDtypeStruct((n, W_C), F32),
        compiler_params=_params(("arbitrary",)),
        name="s5_post",
    )(y, w_glu.astype(BF16), b_glu.reshape(1, W_C), g_mix_c.reshape(1, W_C))


def _mix_kernel(ap_ref, bp_ref, cp_ref, as_ref, bs_ref, cs_ref, g_ref, o_ref):
    def emit(a_ref, b_ref, c_ref):
        o_ref[:, 0:W_A] = _rms(a_ref[...], g_ref[...]).astype(BF16)
        o_ref[:, W_A:W_A + W_B] = b_ref[...].astype(BF16)
        o_ref[:, W_A + W_B:D] = c_ref[...].astype(BF16)

    last = pl.program_id(0) == NT - 1

    @pl.when(jnp.logical_not(last))
    def _():
        emit(ap_ref, bp_ref, cp_ref)

    @pl.when(last)
    def _():
        emit(as_ref, bs_ref, cs_ref)


def _mix(a_p, b_p, c_p, a_s, b_s, c_s, g_mix_a):
    pr = lambda w: pl.BlockSpec((TM, w), lambda i: (jnp.minimum(i, NT - 2), 0))
    sm = lambda w: pl.BlockSpec((TM, w), lambda i: (0, 0))
    return pl.pallas_call(
        _mix_kernel,
        grid=(NT,),
        in_specs=[pr(W_A), pr(W_B), pr(W_C), sm(W_A), sm(W_B), sm(W_C), pl.BlockSpec((1, W_A), lambda i: (0, 0))],
        out_specs=pl.BlockSpec((TM, D), lambda i: (i, 0)),
        out_shape=jax.ShapeDtypeStruct((NP, D), BF16),
        compiler_params=_params(("arbitrary",)),
        name="mix",
    )(a_p, b_p, c_p, a_s, b_s, c_s, g_mix_a.reshape(1, W_A))


def _router_kernel(h_ref, w_ref, b_ref, ind_ref, wd_ref, cnt_ref):
    i = pl.program_id(0)
    scores = jax.nn.sigmoid(_dot(h_ref[...], w_ref[...]))
    lane = lax.broadcasted_iota(I32, (TD, N_EXPERTS), 1).astype(F32)
    row = lax.broadcasted_iota(I32, (TD, N_EXPERTS), 0) + i * TD
    sel = scores + b_ref[...]
    ind = jnp.zeros((TD, N_EXPERTS), F32)
    for _ in range(TOP_K):
        m = jnp.max(sel, axis=-1, keepdims=True)
        idx = jnp.min(jnp.where(sel == m, lane, float(N_EXPERTS)), axis=-1, keepdims=True)
        hit = lane == idx
        ind = jnp.where(hit, 1.0, ind)
        sel = jnp.where(hit, -jnp.inf, sel)
    ind = jnp.where(row < N_REAL, ind, 0.0)
    wsel = ind * scores
    wd_ref[...] = wsel / jnp.sum(wsel + (1.0 - jnp.max(ind, axis=-1, keepdims=True)) / N_EXPERTS,
                                 axis=-1, keepdims=True) * ROUTE_SCALE
    ind_ref[...] = ind

    @pl.when(i == 0)
    def _():
        cnt_ref[...] = jnp.zeros((8, N_EXPERTS), F32)

    cnt_ref[...] += jnp.sum(ind, axis=0, keepdims=True)


def _router(h2b, w_router, b_router):
    til = pl.BlockSpec((TD, N_EXPERTS), lambda i: (i, 0))
    return pl.pallas_call(
        _router_kernel,
        grid=(NP // TD,),
        in_specs=[pl.BlockSpec((TD, D), lambda i: (i, 0)), pl.BlockSpec((D, N_EXPERTS), lambda i: (0, 0)),
                  pl.BlockSpec((1, N_EXPERTS), lambda i: (0, 0))],
        out_specs=[til, til, pl.BlockSpec((8, N_EXPERTS), lambda i: (0, 0))],
        out_shape=[jax.ShapeDtypeStruct((NP, N_EXPERTS), F32)] * 2 + [jax.ShapeDtypeStruct((8, N_EXPERTS), F32)],
        compiler_params=_params(("arbitrary",)),
        name="router",
    )(h2b, w_router.astype(BF16), b_router.reshape(1, N_EXPERTS))


def _group_starts(cnt):
    padded = jnp.floor((cnt + (BM - 1)) / BM) * BM
    tri = (lax.broadcasted_iota(I32, (N_EXPERTS, N_EXPERTS), 0)
           <= lax.broadcasted_iota(I32, (N_EXPERTS, N_EXPERTS), 1)).astype(F32)
    pend = jnp.dot(padded, tri, preferred_element_type=F32, precision=lax.Precision.HIGHEST)
    return padded, pend - padded, pend


def _plan_kernel(cnt_ref, be_ref, misc_ref):
    cnt = cnt_ref[...]
    _, pstart, pend = _group_starts(cnt)
    rowi = (lax.broadcasted_iota(I32, (NBP, N_EXPERTS), 0) * BM).astype(F32)
    be = jnp.minimum(jnp.sum(jnp.where(pend[0:1] <= rowi, 1.0, 0.0), axis=-1, keepdims=True), N_EXPERTS - 1.0)
    be_ref[...] = jnp.broadcast_to(be, (NBP, 128)).astype(I32)
    misc_ref[0:8, :] = pstart.astype(I32)
    misc_ref[8:16, :] = cnt.astype(I32)
    misc_ref[16:24, :] = (pend / BM).astype(I32)


def _plan(cnt):
    return pl.pallas_call(
        _plan_kernel,
        out_shape=[jax.ShapeDtypeStruct((NBP, 128), I32), jax.ShapeDtypeStruct((24, N_EXPERTS), I32)],
        name="moe_plan",
    )(cnt)


def _slots_kernel(ind_ref, wd_ref, cnt_ref, dest_ref, w_ref, run_ref):
    @pl.when(pl.program_id(0) == 0)
    def _():
        run_ref[...] = jnp.zeros((8, N_EXPERTS), F32)

    ind = ind_ref[...]
    _, pstart, _ = _group_starts(cnt_ref[...])
    ltri = (lax.broadcasted_iota(I32, (TD, TD), 1) < lax.broadcasted_iota(I32, (TD, TD), 0)).astype(BF16)
    pos = _dot(ltri, ind.astype(BF16)) + run_ref[0:1, :]
    run_ref[...] += jnp.sum(ind, axis=0, keepdims=True)
    cur = jnp.where(ind > 0.0, pstart[0:1] + pos, BIG)
    lane = lax.broadcasted_iota(I32, (TD, 128), 1)
    dest = jnp.full((TD, 128), -1.0, F32)
    wts = jnp.zeros((TD, 128), F32)
    wd = wd_ref[...]
    for k in range(TOP_K):
        m = jnp.min(cur, axis=-1, keepdims=True)
        hit = cur == m
        wk = jnp.sum(jnp.where(hit, wd, 0.0), axis=-1, keepdims=True)
        dest = jnp.where(lane == k, jnp.where(m < BIG, m, -1.0), dest)
        wts = jnp.where(lane == k, jnp.where(m < BIG, wk, 0.0), wts)
        cur = jnp.where(hit, BIG, cur)
    dest_ref[...] = dest.astype(I32)
    w_ref[...] = wts


def _slots(ind, wd, cnt):
    til = pl.BlockSpec((TD, N_EXPERTS), lambda i: (i, 0))
    out = pl.BlockSpec((TD, 128), lambda i: (i, 0))
    return pl.pallas_call(
        _slots_kernel,
        grid=(NP // TD,),
        in_specs=[til, til, pl.BlockSpec((8, N_EXPERTS), lambda i: (0, 0))],
        out_specs=[out, out],
        out_shape=[jax.ShapeDtypeStruct((NP, 128), I32), jax.ShapeDtypeStruct((NP, 128), F32)],
        scratch_shapes=[pltpu.VMEM((8, N_EXPERTS), F32)],
        compiler_params=_params(("arbitrary",)),
        name="moe_slots",
    )(ind, wd, cnt)


def _dispatch_kernel(ps_ref, cnt_ref, nbu_ref, dest_ref, x_ref, xs_ref, zero_ref, sem):
    i = pl.program_id(0)

    def token(ref, t):
        start = t * ROW_TILES if isinstance(t, int) else pl.multiple_of(t * ROW_TILES, ROW_TILES)
        return ref.at[pl.ds(start, ROW_TILES), :]

    def row_copy(r, k):
        return pltpu.make_async_copy(token(x_ref, r), token(xs_ref, dest_ref[r * 8 + k]), sem)

    def zero_row(d):
        return pltpu.make_async_copy(token(zero_ref, 0), token(xs_ref, d), sem)

    def zero_block(b):
        rows = BM * ROW_TILES
        return pltpu.make_async_copy(zero_ref, xs_ref.at[pl.ds(pl.multiple_of(b * rows, rows), rows), :], sem)

    @pl.when(i < N_DTILES)
    def _():
        n_rows = jnp.minimum(TD, N_REAL - i * TD)

        def issue(r, c):
            for k in range(TOP_K):
                row_copy(r, k).start(priority=k % 2)
            return c

        def drain(r, c):
            for k in range(TOP_K):
                row_copy(r, k).wait()
            return c

        lax.fori_loop(0, n_rows, issue, 0)
        lax.fori_loop(0, n_rows, drain, 0)

    @pl.when(i == N_DTILES)
    def _():
        zero_ref[...] = jnp.zeros(zero_ref.shape, F32)

        def per_expert(e, c):
            cnt = cnt_ref[e]
            base = ps_ref[e]
            end = ((cnt + (BM - 1)) >> BM_SHIFT) << BM_SHIFT
            lax.fori_loop(cnt, end, lambda j, c2: (zero_row(base + j).start(), c2)[1], 0)
            lax.fori_loop(cnt, end, lambda j, c2: (zero_row(base + j).wait(), c2)[1], 0)
            return c

        lax.fori_loop(0, N_EXPERTS, per_expert, 0)
        lax.fori_loop(nbu_ref[0], NB, lambda b, c: (zero_block(b).start(), c)[1], 0)
        lax.fori_loop(nbu_ref[0], NB, lambda b, c: (zero_block(b).wait(), c)[1], 0)


def _dispatch(pstart_i, cnt_i, nbu, dest_flat, x):
    return pl.pallas_call(
        _dispatch_kernel,
        grid_spec=pltpu.PrefetchScalarGridSpec(
            num_scalar_prefetch=3, grid=(N_DTILES + 1,),
            in_specs=[pl.BlockSpec((TD * 8,), lambda i, *_: (jnp.minimum(i, N_DTILES - 1),),
                                   memory_space=pltpu.SMEM),
                      pl.BlockSpec((TD * ROW_TILES, 128), lambda i, *_: (jnp.minimum(i, N_DTILES - 1), 0))],
            out_specs=pl.BlockSpec(memory_space=pl.ANY),
            scratch_shapes=[pltpu.VMEM((BM * ROW_TILES, 128), F32), pltpu.SemaphoreType.DMA(())]),
        out_shape=jax.ShapeDtypeStruct((CAP * ROW_TILES, 128), F32),
        compiler_params=_params(("arbitrary",)),
        name="moe_dispatch",
    )(pstart_i, cnt_i, nbu, dest_flat, x)


def _expert_kernel(be_ref, nbu_ref, x_ref, wg_ref, wu_ref, wd_ref, o_ref):
    i = pl.program_id(0)

    @pl.when(i < nbu_ref[0])
    def _():
        x = _load_token_rows(x_ref, BM).astype(BF16)
        g = _dot(x, wg_ref[...].astype(BF16))
        u = _dot(x, wu_ref[...].astype(BF16))
        a = (g * jax.nn.sigmoid(g) * u).astype(BF16)
        _store_token_rows(o_ref, _dot(a, wd_ref[...].astype(BF16)))

    @pl.when(i >= nbu_ref[0])
    def _():
        o_ref[...] = jnp.zeros(o_ref.shape, F32)


def _experts(layer, be, nbu, xs, w_gate, w_up, w_down):
    blk = lambda i, be, nbu: (jnp.minimum(i, nbu[0] - 1), 0)
    wsel = lambda i, be, nbu: (layer, be[jnp.minimum(i, nbu[0] - 1)], 0, 0)
    return pl.pallas_call(
        _expert_kernel,
        grid_spec=pltpu.PrefetchScalarGridSpec(
            num_scalar_prefetch=2, grid=(NB,),
            in_specs=[pl.BlockSpec((BM * ROW_TILES, 128), blk),
                      pl.BlockSpec((None, None, D, D_EXPERT), wsel),
                      pl.BlockSpec((None, None, D, D_EXPERT), wsel),
                      pl.BlockSpec((None, None, D_EXPERT, D), wsel)],
            out_specs=pl.BlockSpec((BM * ROW_TILES, 128), lambda i, be, nbu: (i, 0))),
        out_shape=jax.ShapeDtypeStruct((CAP * ROW_TILES, 128), F32),
        compiler_params=_params(("arbitrary",)),
        name="moe_experts",
    )(be, nbu, xs, w_gate, w_up, w_down)


def _shared_kernel(h_ref, wg_ref, wu_ref, wd_ref, o_ref):
    h = h_ref[...]
    g = _dot(h, wg_ref[...])
    u = _dot(h, wu_ref[...])
    o_ref[...] = _dot((g * jax.nn.sigmoid(g) * u).astype(BF16), wd_ref[...])


def _shared(h2b, wg, wu, wd):
    return pl.pallas_call(
        _shared_kernel,
        grid=(NT,),
        in_specs=[pl.BlockSpec((TM, D), lambda i: (i, 0)),
                  pl.BlockSpec((D, D_SHARED), lambda i: (0, 0)),
                  pl.BlockSpec((D, D_SHARED), lambda i: (0, 0)),
                  pl.BlockSpec((D_SHARED, D), lambda i: (0, 0))],
        out_specs=pl.BlockSpec((TM, D), lambda i: (i, 0)),
        out_shape=jax.ShapeDtypeStruct((NP, D), F32),
        compiler_params=_params(("arbitrary",)),
        name="moe_shared",
    )(h2b, wg.astype(BF16), wu.astype(BF16), wd.astype(BF16))


def _combine_kernel(dest_ref, nxt_ref, w_ref, ys_ref, sh_ref, x_ref, gr_ref, gt_ref, o_ref, rows_ref, acc_ref, sem):
    i = pl.program_id(0)
    n_tiles = NP // TC_
    slot = i % 2

    def gather(d_ref, s, start):
        def body(r, c):
            for k in range(TOP_K):
                d = jnp.maximum(d_ref[r * 8 + k], 0)
                cp = pltpu.make_async_copy(
                    ys_ref.at[pl.ds(pl.multiple_of(d * ROW_TILES, ROW_TILES), ROW_TILES), :],
                    rows_ref.at[s, k, pl.ds(pl.multiple_of(r * ROW_TILES, ROW_TILES), ROW_TILES), :], sem.at[s])
                if start:
                    cp.start(priority=k % 2)
                else:
                    cp.wait()
            return c

        lax.fori_loop(0, TC_, body, 0)

    @pl.when(i == 0)
    def _():
        gather(dest_ref, 0, True)

    @pl.when(i + 1 < n_tiles)
    def _():
        gather(nxt_ref, 1 - slot, True)

    gather(dest_ref, slot, False)
    def weigh(t, c):
        rows = pl.ds(pl.multiple_of(t * ROW_TILES, ROW_TILES), ROW_TILES)
        acc = w_ref[t * 8] * rows_ref[slot, 0, rows, :]
        for k in range(1, TOP_K):
            acc = acc + w_ref[t * 8 + k] * rows_ref[slot, k, rows, :]
        acc_ref[rows, :] = acc
        return c

    lax.fori_loop(0, TC_, weigh, 0, unroll=4)
    routed = _load_token_rows(acc_ref, TC_)
    last = i >= N_PROMPT // TC_
    o_ref[...] = x_ref[...] + _pick(last, gt_ref, gr_ref) * (routed + sh_ref[...])


def _combine(dest_flat, w6, ys, shared, x, gate):
    n_tiles = NP // TC_
    n_p = N_PROMPT // TC_
    per = TM // TC_
    til = pl.BlockSpec((TC_, D), lambda i: (i, 0))
    return pl.pallas_call(
        _combine_kernel,
        grid=(n_tiles,),
        in_specs=[pl.BlockSpec((TC_ * 8,), lambda i: (i,), memory_space=pltpu.SMEM),
                  pl.BlockSpec((TC_ * 8,), lambda i: (jnp.minimum(i + 1, n_tiles - 1),), memory_space=pltpu.SMEM),
                  pl.BlockSpec((TC_ * 8,), lambda i: (i,), memory_space=pltpu.SMEM),
                  pl.BlockSpec(memory_space=pl.ANY), til, til,
                  pl.BlockSpec((None, 1, D), lambda i: (i // per, 0, 0)),
                  pl.BlockSpec((TC_, D), lambda i: (jnp.maximum(i - n_p, 0), 0))],
        out_specs=til,
        out_shape=jax.ShapeDtypeStruct((NP, D), F32),
        scratch_shapes=[pltpu.VMEM((2, TOP_K, TC_ * ROW_TILES, 128), F32), pltpu.VMEM((TC_ * ROW_TILES, 128), F32),
                        pltpu.SemaphoreType.DMA((2,))],
        compiler_params=_params(("arbitrary",)),
        name="moe_combine",
    )(dest_flat, dest_flat, w6, ys, shared, x, gate[0], gate[1])


def _pad_rows(a, rows):
    return jnp.concatenate([a, jnp.zeros((rows - a.shape[0],) + a.shape[1:], a.dtype)], axis=0)


def _s5_layer(z, lw, h0r_s, h0i_s):
    g_mix_c = lw["g_mix"][W_A + W_B:]
    nj = nsc = 16
    state = lambda h, b: h.reshape(S5_NLB, b, S5_GPB, SSM_STATE).transpose(1, 0, 2, 3).reshape(b, G_C, SSM_STATE)
    uc_p = z[:N_PROMPT, IN_COLS - W_C:]
    u_p = uc_p.reshape(B_P, nsc, nj, S5_L_P, S5_NLB, 128).transpose(4, 2, 1, 0, 3, 5)
    u_p = u_p.reshape(S5_NLB, nj * nsc * B_P, S5_L_P * 128)
    zeros_h = jnp.zeros((S5_NLB, B_P, S5_SW), F32)
    y_p, hr_p, hi_p = _s5(u_p, lw, zeros_h, zeros_h, S5_L_P, nj, nsc, B_P)
    y_p = y_p.reshape(S5_NLB, nj, nsc, B_P, S5_L_P, 128).transpose(3, 2, 1, 4, 0, 5).reshape(N_PROMPT, W_C)
    c_p = _s5_post(y_p, lw["w_glu"], lw["b_glu"], g_mix_c)

    uc_s = z[N_PROMPT:N_REAL, IN_COLS - W_C:]
    u_s = uc_s.reshape(B_S, T_S, S5_NLB, 128).transpose(2, 0, 1, 3).reshape(S5_NLB, B_S, S5_L_S * 128)
    h0 = lambda h: h.reshape(B_S, S5_NLB, S5_SW).transpose(1, 0, 2)
    y_s, hr_s, hi_s = _s5(u_s, lw, h0(h0r_s), h0(h0i_s), S5_L_S, 1, 1, B_S)
    y_s = y_s.reshape(S5_NLB, B_S, T_S, 128).transpose(1, 2, 0, 3).reshape(N_SAMPLE, W_C)
    c_s = _s5_post(y_s, lw["w_glu"], lw["b_glu"], g_mix_c)
    return c_p, c_s, state(hr_p, B_P), state(hi_p, B_P), state(hr_s, B_S), state(hi_s, B_S)


def _layer(x, l, lw, experts, mod, bias_p, bias_s, cache_k, cache_v, h0r_s, h0i_s):
    sh1, sc1, g1, sh2, sc2, g2 = [_tile_tabs(m) for m in jnp.split(mod, 6, axis=-1)]
    (h,) = _modnorm(x, lw["g_attn"], sc1, sh1, False)
    z = _matmul(h, lw["w_in"].astype(BF16), 1536)

    a_p = _attn_prompt(z, bias_p)
    a_s = _attn_sample(z, cache_k, cache_v, l, bias_s)

    g_mix_b = lw["g_mix"][W_A:W_A + W_B]
    gm = (lw["g_bv"], lw["b_bv"], lw["w_s"], lw["b_s"], g_mix_b)
    b_p, _ = _gmlp(z, N_PROMPT // CHUNK, 3 * W_A // W_B, 3 * W_A // W_B + 1, *gm)
    uv_s = z[N_PROMPT:N_REAL, 3 * W_A:3 * W_A + 2 * W_B].reshape(B_S, T_S, 2 * W_B)
    uv_s = jnp.pad(uv_s, ((0, 0), (0, CHUNK - T_S), (0, 0))).reshape(B_S * CHUNK, 2 * W_B)
    b_s, vn_s = _gmlp(uv_s, B_S, 0, 1, *gm)
    b_s = b_s.reshape(B_S, CHUNK, W_B)[:, :T_S].reshape(N_SAMPLE, W_B)
    new_vb = vn_s.reshape(B_S, CHUNK, W_B)[:, :T_S]

    c_p, c_s, hr_p, hi_p, hr_s, hi_s = _s5_layer(z, lw, h0r_s, h0i_s)

    mixed = _mix(a_p, b_p, c_p, _pad_rows(a_s, TM), _pad_rows(b_s, TM), _pad_rows(c_s, TM), lw["g_mix"][:W_A])
    x = _matmul_residual(mixed, lw["w_out"].astype(BF16), x, g1, 1024)

    h2b, h2 = _modnorm(x, lw["g_ffn"], sc2, sh2, True)
    ind, wd, cnt = _router(h2b, lw["w_router"], lw["b_router"])
    be, misc = _plan(cnt)
    dest, w6 = _slots(ind, wd, cnt)
    dest_flat = dest[:, :8].reshape(-1)
    xs = _dispatch(misc[0], misc[8], misc[16, N_EXPERTS - 1:], dest_flat, h2)
    ys = _experts(l, be[:NB, 0], misc[16, N_EXPERTS - 1:], xs, *experts)
    shared = _shared(h2b, lw["w_sh_gate"], lw["w_sh_up"], lw["w_sh_down"])
    x = _combine(dest_flat, w6[:, :8].reshape(-1), ys, shared, x, g2)

    kv = lambda lo: (z[:N_PROMPT, lo:lo + W_A].reshape(B_P, T_P, H_A, HEAD_DIM)[:, T_P - BUF:],
                     z[N_PROMPT:N_REAL, lo:lo + W_A].reshape(B_S, T_S, H_A, HEAD_DIM))
    (k_p, k_s), (v_p, v_s) = kv(W_A), kv(2 * W_A)
    return x, (k_p, v_p, k_s, v_s, new_vb, hr_p, hi_p, hr_s, hi_s)


def kernel(x_prompt, x_sample, cache_a_k, cache_a_v, state_c_re, state_c_im, c_prompt, c_sample, rel_bias, w_ada, b_ada, g_attn, w_in, g_bv, b_bv, w_s, b_s, lam_re, lam_im, log_dt, ssm_b_re, ssm_b_im, ssm_c_re, ssm_c_im, ssm_d, w_glu, b_glu, g_mix, w_out, g_ffn, w_router, b_router, w_gate, w_up, w_down, w_sh_gate, w_sh_up, w_sh_down, g_final):
    assert x_prompt.shape == (B_P, T_P, D) and x_sample.shape == (B_S, T_S, D)
    assert cache_a_k.shape == (DEPTH, B_S, BUF, H_A, HEAD_DIM)
    per_layer = dict(g_attn=g_attn, w_in=w_in, g_bv=g_bv, b_bv=b_bv, w_s=w_s, b_s=b_s, lam_re=lam_re,
                     lam_im=lam_im, log_dt=log_dt, ssm_b_re=ssm_b_re, ssm_b_im=ssm_b_im, ssm_c_re=ssm_c_re,
                     ssm_c_im=ssm_c_im, ssm_d=ssm_d, w_glu=w_glu, b_glu=b_glu, g_mix=g_mix, w_out=w_out,
                     g_ffn=g_ffn, w_router=w_router, b_router=b_router, w_sh_gate=w_sh_gate, w_sh_up=w_sh_up,
                     w_sh_down=w_sh_down)

    x = _pad_rows(jnp.concatenate([x_prompt.reshape(N_PROMPT, D), x_sample.reshape(N_SAMPLE, D)], axis=0), NP)
    c_all = _pad_rows(jnp.concatenate([c_prompt, c_sample], axis=0), 16)
    mod = _mod_all(c_all, w_ada, b_ada)
    bias_p = _bias_bank(rel_bias, *_prompt_bias_tables())
    bias_s = _bias_bank(rel_bias, *_sample_bias_tables())[0]
    cache_k = cache_a_k.reshape(DEPTH * B_S, BUF, W_A)
    cache_v = cache_a_v.reshape(DEPTH * B_S, BUF, W_A)

    outs = []
    for l in range(DEPTH):
        lw = {name: w[l] for name, w in per_layer.items()}
        x, o = _layer(x, l, lw, (w_gate, w_up, w_down), mod[l], bias_p, bias_s, cache_k, cache_v,
                      state_c_re[l], state_c_im[l])
        outs.append(o)
    y = _finalnorm(x, g_final)
    stacked = [jnp.stack([o[j] for o in outs]) for j in range(9)]
    return (y[:N_PROMPT].reshape(B_P, T_P, D), y[N_PROMPT:N_REAL].reshape(B_S, T_S, D), *stacked)
```

```python
import functools
import math

import jax
import jax.numpy as jnp
import numpy as np
from jax import lax
from jax.experimental import pallas as pl
from jax.experimental.pallas import tpu as pltpu

F32 = jnp.float32
BF16 = jnp.bfloat16
I32 = jnp.int32

D = 2048
B_P, T_P = 2, 4096
B_S, T_S = 8, 8
DEPTH = 2
HEAD_DIM = 64
W_A = 1024
H_A = 16
W_B = 512
G_B = 8
W_C = 512
SSM_GROUP = 16
G_C = 32
SSM_STATE = 64
IN_COLS = 3 * W_A + 2 * W_B + W_C
DILATIONS = (1, 4, 16)
NK = 128
ATTN_UNITS = (8, 4, 4)
BUF = 2048
CHUNK = 128
N_BUCKETS = 32
REL_MAX_DIST = 2048
N_EXPERTS = 64
TOP_K = 6
D_EXPERT = 512
D_SHARED = 1024
ROUTE_SCALE = 2.5
EPS = 1e-6

N_PROMPT = B_P * T_P
N_SAMPLE = B_S * T_S
N_REAL = N_PROMPT + N_SAMPLE
TM = 512
NT = N_PROMPT // TM + 1
NP = NT * TM
TD = 256
N_DTILES = -(-N_REAL // TD)
ROW_TILES = D // 128
BM = 256
BM_SHIFT = 8
NB = -(-(N_REAL * TOP_K + N_EXPERTS * (BM - 1)) // BM)
NBP = -(-NB // 8) * 8
CAP = NB * BM
TC_ = 128
KF_ROWS = 2176
S5_L_P = 16
S5_L_S = T_S
S5_GPB = 128 // SSM_GROUP
S5_NLB = G_C // S5_GPB
S5_SW = S5_GPB * SSM_STATE
BIG = 1e9
assert BM == 1 << BM_SHIFT

VMEM_LIMIT_BYTES = 56 * 1024 * 1024


def _params(dims):
    return pltpu.CompilerParams(dimension_semantics=dims, vmem_limit_bytes=VMEM_LIMIT_BYTES)


def _dot(a, b):
    return jnp.dot(a, b, preferred_element_type=F32)


def _dot_nt(a, b, precision=None):
    return lax.dot_general(a, b, (((1,), (1,)), ((), ())), preferred_element_type=F32,
                           precision=precision)


def _rms(x, g):
    return x * lax.rsqrt(jnp.mean(x * x, axis=-1, keepdims=True) + EPS) * g


def _mod_kernel(c_ref, w_ref, b_ref, o_ref):
    c = c_ref[...]
    s = (c * jax.nn.sigmoid(c)).astype(BF16)
    o_ref[...] = _dot(s, w_ref[...].astype(BF16)) + b_ref[...]


def _mod_all(c_all, w_ada, b_ada):
    tn = 1024
    return pl.pallas_call(
        _mod_kernel,
        grid=(DEPTH, 6 * D // tn),
        in_specs=[pl.BlockSpec((16, D), lambda l, j: (0, 0)),
                  pl.BlockSpec((None, D, tn), lambda l, j: (l, 0, j)),
                  pl.BlockSpec((None, 1, tn), lambda l, j: (l, 0, j))],
        out_specs=pl.BlockSpec((None, 16, tn), lambda l, j: (l, 0, j)),
        out_shape=jax.ShapeDtypeStruct((DEPTH, 16, 6 * D), F32),
        compiler_params=_params(("arbitrary", "arbitrary")),
        name="mod",
    )(c_all, w_ada, b_ada.reshape(DEPTH, 1, 6 * D))


def _tile_tabs(m):
    row = jnp.concatenate([jnp.repeat(m[0:B_P], NT // B_P, axis=0), jnp.zeros((1, m.shape[1]), F32)], 0)
    tok = jnp.concatenate([jnp.repeat(m[B_P:B_P + B_S], T_S, axis=0),
                           jnp.zeros((TM - N_SAMPLE, m.shape[1]), F32)], 0)
    return row[:, None, :], tok


def _pick(last, tok_ref, row_ref):
    return jnp.where(last, tok_ref[...], row_ref[...])


def _modnorm_kernel(x_ref, g_ref, scr_ref, sct_ref, shr_ref, sht_ref, *o_refs):
    last = pl.program_id(0) == NT - 1
    y = _rms(x_ref[...], g_ref[...])
    y = y * (1.0 + _pick(last, sct_ref, scr_ref)) + _pick(last, sht_ref, shr_ref)
    o_refs[0][...] = y.astype(BF16)
    if len(o_refs) > 1:
        _store_token_rows(o_refs[1], y)


def _load_token_rows(ref, n):
    return jnp.concatenate([ref[pl.ds(j, n, stride=ROW_TILES), :] for j in range(ROW_TILES)], axis=1)


def _store_token_rows(ref, val):
    for j in range(ROW_TILES):
        ref[pl.ds(j, val.shape[0], stride=ROW_TILES), :] = val[:, j * 128:(j + 1) * 128]


def _modnorm(x, g, sc, sh, with_token_rows):
    row = pl.BlockSpec((None, 1, D), lambda i: (i, 0, 0))
    tok = pl.BlockSpec((TM, D), lambda i: (0, 0))
    til = pl.BlockSpec((TM, D), lambda i: (i, 0))
    out_specs, out_shape = [til], [jax.ShapeDtypeStruct((NP, D), BF16)]
    if with_token_rows:
        out_specs.append(pl.BlockSpec((TM * ROW_TILES, 128), lambda i: (i, 0)))
        out_shape.append(jax.ShapeDtypeStruct((NP * ROW_TILES, 128), F32))
    return pl.pallas_call(
        _modnorm_kernel,
        grid=(NT,),
        in_specs=[til, pl.BlockSpec((1, D), lambda i: (0, 0)), row, tok, row, tok],
        out_specs=out_specs,
        out_shape=out_shape,
        compiler_params=_params(("arbitrary",)),
        name="modnorm",
    )(x, g.reshape(1, D), sc[0], sc[1], sh[0], sh[1])


def _finalnorm_kernel(x_ref, g_ref, o_ref):
    o_ref[...] = _rms(x_ref[...], g_ref[...])


def _finalnorm(x, g):
    til = pl.BlockSpec((TM, D), lambda i: (i, 0))
    return pl.pallas_call(
        _finalnorm_kernel,
        grid=(NT,),
        in_specs=[til, pl.BlockSpec((1, D), lambda i: (0, 0))],
        out_specs=til,
        out_shape=jax.ShapeDtypeStruct((NP, D), F32),
        compiler_params=_params(("arbitrary",)),
        name="finalnorm",
    )(x, g.reshape(1, D))


def _mm_kernel(a_ref, b_ref, o_ref):
    o_ref[...] = _dot(a_ref[...], b_ref[...])


def _matmul(a, b, tn):
    k, n = b.shape
    return pl.pallas_call(
        _mm_kernel,
        grid=(n // tn, NT),
        in_specs=[pl.BlockSpec((TM, k), lambda c, i: (i, 0)),
                  pl.BlockSpec((k, tn), lambda c, i: (0, c))],
        out_specs=pl.BlockSpec((TM, tn), lambda c, i: (i, c)),
        out_shape=jax.ShapeDtypeStruct((NP, n), F32),
        compiler_params=_params(("arbitrary", "arbitrary")),
        name="matmul",
    )(a, b)


def _mm_res_kernel(a_ref, b_ref, x_ref, gr_ref, gt_ref, o_ref):
    last = pl.program_id(1) == NT - 1
    o_ref[...] = x_ref[...] + _pick(last, gt_ref, gr_ref) * _dot(a_ref[...], b_ref[...])


def _matmul_residual(a, b, x, gate, tn):
    k, n = b.shape
    return pl.pallas_call(
        _mm_res_kernel,
        grid=(n // tn, NT),
        in_specs=[pl.BlockSpec((TM, k), lambda c, i: (i, 0)),
                  pl.BlockSpec((k, tn), lambda c, i: (0, c)),
                  pl.BlockSpec((TM, tn), lambda c, i: (i, c)),
                  pl.BlockSpec((None, 1, tn), lambda c, i: (i, 0, c)),
                  pl.BlockSpec((TM, tn), lambda c, i: (0, c))],
        out_specs=pl.BlockSpec((TM, tn), lambda c, i: (i, c)),
        out_shape=jax.ShapeDtypeStruct((NP, n), F32),
        compiler_params=_params(("arbitrary", "arbitrary")),
        name="matmul_residual",
    )(a, b, x, gate[0], gate[1])


def _t5_bucket(dist):
    max_exact = N_BUCKETS // 2
    dist = jnp.maximum(dist, 0)
    ratio = jnp.log(jnp.maximum(dist, 1).astype(F32) / max_exact) / math.log(REL_MAX_DIST / max_exact)
    large = jnp.minimum(max_exact + (ratio * (N_BUCKETS - max_exact)).astype(I32), N_BUCKETS - 1)
    return jnp.where(dist < max_exact, dist, large)


def _bias_kernel(rb_ref, bucket_ref, add_ref, o_ref):
    h = pl.program_id(1)
    bucket = bucket_ref[...]
    acc = jnp.full(bucket.shape, -jnp.inf, F32)
    for b in range(N_BUCKETS):
        acc = jnp.where(bucket == b, rb_ref[b * H_A + h], acc)
    o_ref[...] = acc + add_ref[...]


def _bias_bank(rel_bias, bucket, add):
    n, r, c = bucket.shape
    blk = pl.BlockSpec((None, r, c), lambda i, h, rb: (i, 0, 0))
    return pl.pallas_call(
        _bias_kernel,
        grid_spec=pltpu.PrefetchScalarGridSpec(
            num_scalar_prefetch=1, grid=(n, H_A), in_specs=[blk, blk],
            out_specs=pl.BlockSpec((None, None, r, c), lambda i, h, rb: (i, h, 0, 0))),
        out_shape=jax.ShapeDtypeStruct((n, H_A, r, c), F32),
        compiler_params=_params(("arbitrary", "arbitrary")),
        name="bias_bank",
    )(rel_bias.reshape(-1), bucket, add)


def _prompt_bias_tables():
    qi = jnp.arange(NK)[:, None]
    kj = jnp.arange(2 * NK)[None, :]
    rel = qi + NK - kj
    band = (rel >= 0) & (rel <= NK)
    bucket = jnp.stack([jnp.where(band, _t5_bucket(rel * d), -1) for d in DILATIONS]).astype(I32)
    return bucket, jnp.zeros(bucket.shape, F32)


def _sample_bias_tables():
    dist_np = BUF + np.arange(T_S)[:, None] - np.arange(KF_ROWS)[None, :]
    mult = np.zeros(dist_np.shape, np.float64)
    for d in DILATIONS:
        mult += (dist_np >= 0) & (dist_np % d == 0) & (dist_np // d <= NK)
    mult = np.where(np.arange(KF_ROWS)[None, :] < BUF + T_S, mult, 0.0)
    valid = mult > 0
    add = np.where(valid, np.log(np.maximum(mult, 1.0)), 0.0).astype(np.float32)
    bucket = jnp.where(jnp.asarray(valid), _t5_bucket(jnp.asarray(dist_np, I32)), -1).astype(I32)
    return bucket[None], jnp.asarray(add)[None]


def _attn_prompt_kernel(q_ref, k_ref, v_ref, bias_ref, o_ref, *stats):
    lane = lax.broadcasted_iota(I32, (NK, 128), 1)
    head0 = lane < HEAD_DIM
    prev_cols = lax.broadcasted_iota(I32, (NK, 2 * NK), 1) < NK
    ones = jnp.ones((NK, 128), F32)
    n_br = len(DILATIONS)
    m_refs, l_refs, a_refs = stats[0:n_br], stats[n_br:2 * n_br], stats[2 * n_br:3 * n_br]
    for di in range(n_br):
        d = DILATIONS[di]
        n_blocks = T_P // (d * NK)
        n_units = ATTN_UNITS[di]
        run = min(n_units, n_blocks)
        assert n_units % run == 0 and n_blocks % run == 0

        def group(gi, carry, di=di, d=d, n_blocks=n_blocks, n_units=n_units, run=run):
            def idx(start):
                return pl.ds(start, NK) if d == 1 else pl.ds(start, NK, stride=d)

            def keys(start):
                return k_ref[idx(start), :].astype(BF16)

            def values(start):
                return jnp.concatenate([v_ref[idx(start), :], ones], axis=1).astype(BF16)

            units = []
            for j in range(n_units):
                u = gi * n_units + j
                r = u >> int(math.log2(n_blocks))
                n = u & (n_blocks - 1)
                cur = r + d * NK * n
                if j % run == 0:
                    prv = r + d * NK * jnp.maximum(n - 1, 0)
                    k_prev, v_prev = keys(prv), values(prv)
                    pen = jnp.where(n == 0, -jnp.inf, 0.0).astype(F32)
                else:
                    k_prev, v_prev, pen = k_cur, v_cur, None
                k_cur, v_cur = keys(cur), values(cur)
                units.append(dict(cur=cur, pen=pen, q=q_ref[idx(cur), :] * (HEAD_DIM ** -0.5),
                                  k2=jnp.concatenate([k_prev, k_cur], axis=0),
                                  v2=jnp.concatenate([v_prev, v_cur], axis=0)))
            for unit in units:
                pv, rs, mn = [], [], []
                for h in range(2):
                    qh = jnp.where(head0 if h == 0 else ~head0, unit["q"], 0.0).astype(BF16)
                    s = _dot_nt(qh, unit["k2"]) + bias_ref[di, h]
                    if unit["pen"] is not None:
                        s = s + jnp.where(prev_cols, unit["pen"], 0.0)
                    m_h = jnp.max(s, axis=-1, keepdims=True)
                    pv_rs = _dot(jnp.exp(s - m_h).astype(BF16), unit["v2"])
                    pv.append(pv_rs[:, 0:128])
                    rs.append(pv_rs[:, 128:256])
                    mn.append(m_h)
                m_refs[di][idx(unit["cur"]), :] = jnp.where(head0, mn[0], mn[1])
                l_refs[di][idx(unit["cur"]), :] = jnp.where(head0, rs[0], rs[1])
                a_refs[di][idx(unit["cur"]), :] = jnp.where(head0, pv[0], pv[1])
            return carry

        lax.fori_loop(0, T_P // NK // n_units, group, 0)

    def merge(bi, carry):
        rows = pl.ds(pl.multiple_of(bi * NK, NK), NK)
        ms = [m_ref[rows, :] for m_ref in m_refs]
        m = functools.reduce(jnp.maximum, ms)
        num = jnp.zeros((NK, 128), F32)
        den = jnp.zeros((NK, 128), F32)
        for m_i, l_ref, a_ref in zip(ms, l_refs, a_refs):
            w = jnp.exp(m_i - m)
            num = num + w * a_ref[rows, :]
            den = den + w * l_ref[rows, :]
        o_ref[rows, :] = num / den
        return carry

    lax.fori_loop(0, T_P // NK, merge, 0, unroll=2)


def _attn_prompt(z, bias):
    blk = lambda off: pl.BlockSpec((T_P, 128), lambda b, h, off=off: (b, off + h))
    return pl.pallas_call(
        _attn_prompt_kernel,
        grid=(B_P, H_A // 2),
        in_specs=[blk(0), blk(W_A // 128), blk(2 * W_A // 128),
                  pl.BlockSpec((len(DILATIONS), 2, NK, 2 * NK), lambda b, h: (0, h, 0, 0))],
        out_specs=pl.BlockSpec((T_P, 128), lambda b, h: (b, h)),
        out_shape=jax.ShapeDtypeStruct((N_PROMPT, W_A), F32),
        scratch_shapes=[pltpu.VMEM((T_P, 128), F32)] * (3 * len(DILATIONS)),
        compiler_params=_params(("arbitrary", "arbitrary")),
        name="attn_prompt",
    )(z, z, z, bias)


def _attn_sample_kernel(q_ref, kn_ref, vn_ref, ck_ref, cv_ref, bias_ref, o_ref, kf_ref, vf_ref):
    for f_ref, c_ref, n_ref in ((kf_ref, ck_ref, kn_ref), (vf_ref, cv_ref, vn_ref)):
        f_ref[0:BUF, :] = c_ref[...]
        f_ref[BUF:BUF + T_S, :] = n_ref[...]
        f_ref[BUF + T_S:KF_ROWS, :] = jnp.zeros((KF_ROWS - BUF - T_S, 128), F32)
    head0 = lax.broadcasted_iota(I32, (T_S, 128), 1) < HEAD_DIM
    q = q_ref[...] * (HEAD_DIM ** -0.5)
    q2 = jnp.concatenate([jnp.where(head0, q, 0.0), jnp.where(head0, 0.0, q)], axis=0).astype(BF16)
    bias2 = jnp.concatenate([bias_ref[0], bias_ref[1]], axis=0)
    s = _dot_nt(q2, kf_ref[...].astype(BF16)) + bias2
    m = jnp.max(s, axis=-1, keepdims=True)
    p = jnp.exp(s - m)
    l = jnp.sum(p, axis=-1, keepdims=True)
    o2 = _dot(p.astype(BF16), vf_ref[...].astype(BF16)) / l
    o_ref[...] = jnp.where(head0, o2[0:T_S], o2[T_S:2 * T_S])


def _attn_sample(z, cache_k, cache_v, layer, bias):
    row0 = N_PROMPT // T_S
    blk = lambda off: pl.BlockSpec((T_S, 128), lambda b, h, off=off: (row0 + b, off + h))
    cache = pl.BlockSpec((None, BUF, 128), lambda b, h: (layer * B_S + b, 0, h))
    return pl.pallas_call(
        _attn_sample_kernel,
        grid=(B_S, H_A // 2),
        in_specs=[blk(0), blk(W_A // 128), blk(2 * W_A // 128), cache, cache,
                  pl.BlockSpec((2, T_S, KF_ROWS), lambda b, h: (h, 0, 0))],
        out_specs=pl.BlockSpec((T_S, 128), lambda b, h: (b, h)),
        out_shape=jax.ShapeDtypeStruct((N_SAMPLE, W_A), F32),
        scratch_shapes=[pltpu.VMEM((KF_ROWS, 128), F32)] * 2,
        compiler_params=_params(("arbitrary", "arbitrary")),
        name="attn_sample",
    )(z, z, z, cache_k, cache_v, bias)


def _gmlp_kernel(ub_ref, vb_ref, gbv_ref, bbv_ref, ws_ref, bs_ref, gmix_ref, bo_ref, vn_ref):
    v = jax.nn.gelu(vb_ref[...])
    mu = jnp.mean(v, axis=-1, keepdims=True)
    var = jnp.mean(jnp.square(v - mu), axis=-1, keepdims=True)
    vn = (v - mu) * lax.rsqrt(var + EPS) * gbv_ref[...] + bbv_ref[...]
    vn_ref[...] = vn
    vnb = vn.astype(BF16)
    causal = (lax.broadcasted_iota(I32, (CHUNK, CHUNK), 1) <= lax.broadcasted_iota(I32, (CHUNK, CHUNK), 0))
    first = lax.broadcasted_iota(I32, (CHUNK, 128), 1) < W_B // G_B
    parts = []
    for gp in range(G_B // 2):
        v2 = vnb[:, gp * 128:(gp + 1) * 128]
        m0 = _dot(jnp.where(causal, ws_ref[2 * gp], 0.0).astype(BF16), v2)
        m1 = _dot(jnp.where(causal, ws_ref[2 * gp + 1], 0.0).astype(BF16), v2)
        parts.append(jnp.where(first, m0, m1))
    mix = jnp.concatenate(parts, axis=1) + bs_ref[...]
    bo_ref[...] = _rms(jax.nn.gelu(ub_ref[...]) * mix, gmix_ref[...])


def _gmlp(z, n_chunks, col_u, col_v, g_bv, b_bv, w_s, b_s, g_mix_b):
    bs_wide = jnp.repeat(b_s.T, W_B // G_B, axis=1)
    vec = pl.BlockSpec((1, W_B), lambda i: (0, 0))
    out = pl.BlockSpec((CHUNK, W_B), lambda i: (i, 0))
    return pl.pallas_call(
        _gmlp_kernel,
        grid=(n_chunks,),
        in_specs=[pl.BlockSpec((CHUNK, W_B), lambda i: (i, col_u)),
                  pl.BlockSpec((CHUNK, W_B), lambda i: (i, col_v)),
                  vec, vec,
                  pl.BlockSpec((G_B, CHUNK, CHUNK), lambda i: (0, 0, 0)),
                  pl.BlockSpec((CHUNK, W_B), lambda i: (0, 0)),
                  vec],
        out_specs=[out, out],
        out_shape=[jax.ShapeDtypeStruct((n_chunks * CHUNK, W_B), F32)] * 2,
        compiler_params=_params(("arbitrary",)),
        name="gmlp",
    )(z, z, g_bv.reshape(1, W_B), b_bv.reshape(1, W_B), w_s, bs_wide, g_mix_b.reshape(1, W_B))


def _cmul(a, b):
    return a[0] * b[0] - a[1] * b[1], a[0] * b[1] + a[1] * b[0]


def _s5_kernel(u_ref, lr_ref, li_ref, ldt_ref, btr_ref, bti_ref, cr_ref, ci_ref, d_ref, h0r_ref, h0i_ref,
               y_ref, hr_ref, hi_ref, er_ref, ei_ref, gr_ref, gi_ref, *, L, nj, nsc, nb):
    hi = lax.Precision.HIGHEST
    lr, li = lr_ref[...], li_ref[...]
    dt = jnp.exp(ldt_ref[...])
    mag = jnp.exp(lr * dt)
    a = (mag * jnp.cos(li * dt), mag * jnp.sin(li * dt))
    den = lr * lr + li * li
    q = (((a[0] - 1.0) * lr + a[1] * li) / den, (a[1] * lr - (a[0] - 1.0) * li) / den)
    same = ((lax.broadcasted_iota(I32, (128, S5_SW), 0) >> int(math.log2(SSM_GROUP)))
            == (lax.broadcasted_iota(I32, (128, S5_SW), 1) >> int(math.log2(SSM_STATE))))
    bb = _cmul(q, (jnp.where(same, btr_ref[...], 0.0), jnp.where(same, bti_ref[...], 0.0)))
    c = (jnp.where(same, cr_ref[...], 0.0), jnp.where(same, ci_ref[...], 0.0))
    pw = [(jnp.ones_like(lr), jnp.zeros_like(lr))]
    for _ in range(L):
        pw.append(_cmul(pw[-1], a))
    ck = [_cmul(c, p) for p in pw]
    kbd = [(_dot_nt(bb[0], ck[k][0], hi) - _dot_nt(bb[1], ck[k][1], hi)).astype(BF16) for k in range(L)]
    mins = [tuple(x.astype(BF16) for x in _cmul(pw[L - 1 - s], bb)) for s in range(L)]
    cob = [tuple(x.astype(BF16) for x in ck[i + 1]) for i in range(L)]

    us = [u_ref[:, s * 128:(s + 1) * 128] for s in range(L)]
    ub = [x.astype(BF16) for x in us]
    sr = _dot(ub[0], mins[0][0])
    si = _dot(ub[0], mins[0][1])
    for s in range(1, L):
        sr = sr + _dot(ub[s], mins[s][0])
        si = si + _dot(ub[s], mins[s][1])
    al = pw[L]
    r = nsc * nb
    xl = [(sr[0:r], si[0:r])]
    for j in range(1, nj):
        t = _cmul(al, xl[-1])
        xl.append((t[0] + sr[j * r:(j + 1) * r], t[1] + si[j * r:(j + 1) * r]))
    alp = [al]
    for j in range(1, nj):
        alp.append(_cmul(alp[-1], al))
    er_ref[...] = xl[-1][0]
    ei_ref[...] = xl[-1][1]
    g = (h0r_ref[...], h0i_ref[...])
    for sc in range(nsc):
        gr_ref[sc * nb:(sc + 1) * nb, :] = g[0]
        gi_ref[sc * nb:(sc + 1) * nb, :] = g[1]
        t = _cmul(alp[nj - 1], g)
        g = (t[0] + er_ref[sc * nb:(sc + 1) * nb, :], t[1] + ei_ref[sc * nb:(sc + 1) * nb, :])
    hr_ref[...] = g[0]
    hi_ref[...] = g[1]
    gcat = (gr_ref[...], gi_ref[...])
    hprev = [gcat]
    for j in range(1, nj):
        t = _cmul(alp[j - 1], gcat)
        hprev.append((xl[j - 1][0] + t[0], xl[j - 1][1] + t[1]))
    hpr = jnp.concatenate([x[0] for x in hprev], axis=0).astype(BF16)
    hpi = jnp.concatenate([x[1] for x in hprev], axis=0).astype(BF16)
    dsk = d_ref[...]
    for i in range(L):
        y = dsk * us[i] + _dot_nt(hpr, cob[i][0]) - _dot_nt(hpi, cob[i][1])
        for s in range(i + 1):
            y = y + _dot(ub[s], kbd[i - s])
        y_ref[:, i * 128:(i + 1) * 128] = y


def _s5(u, lw, h0r, h0i, L, nj, nsc, nb):
    m, n = u.shape[1], u.shape[2]
    lanes = lambda x: x.reshape(S5_NLB, 1, S5_SW)
    rep = lambda x: jnp.tile(x.reshape(S5_NLB, 128, SSM_STATE), (1, 1, S5_GPB))
    vec = pl.BlockSpec((None, 1, S5_SW), lambda g: (g, 0, 0))
    mat = pl.BlockSpec((None, 128, S5_SW), lambda g: (g, 0, 0))
    st = pl.BlockSpec((None, nb, S5_SW), lambda g: (g, 0, 0))
    return pl.pallas_call(
        functools.partial(_s5_kernel, L=L, nj=nj, nsc=nsc, nb=nb),
        grid=(S5_NLB,),
        in_specs=[pl.BlockSpec((None, m, n), lambda g: (g, 0, 0)), vec, vec, vec, mat, mat, mat, mat,
                  pl.BlockSpec((None, 1, 128), lambda g: (g, 0, 0)), st, st],
        out_specs=[pl.BlockSpec((None, m, n), lambda g: (g, 0, 0)), st, st],
        out_shape=[jax.ShapeDtypeStruct((S5_NLB, m, n), F32)] + [jax.ShapeDtypeStruct((S5_NLB, nb, S5_SW), F32)] * 2,
        scratch_shapes=[pltpu.VMEM((nsc * nb, S5_SW), F32)] * 4,
        compiler_params=_params(("arbitrary",)),
        name="s5",
    )(u, lanes(lw["lam_re"]), lanes(lw["lam_im"]), lanes(jnp.repeat(lw["log_dt"], SSM_STATE)),
      rep(lw["ssm_b_re"].transpose(0, 2, 1)), rep(lw["ssm_b_im"].transpose(0, 2, 1)),
      rep(lw["ssm_c_re"]), rep(lw["ssm_c_im"]), lw["ssm_d"].reshape(S5_NLB, 1, 128), h0r, h0i)


def _s5_post_kernel(y_ref, w_ref, b_ref, g_ref, o_ref):
    zz = jax.nn.gelu(y_ref[...])
    out = zz * jax.nn.sigmoid(_dot(zz.astype(BF16), w_ref[...]) + b_ref[...])
    o_ref[...] = _rms(out, g_ref[...])


def _s5_post(y, w_glu, b_glu, g_mix_c):
    n = y.shape[0]
    tt = min(n, TM)
    vec = pl.BlockSpec((1, W_C), lambda i: (0, 0))
    return pl.pallas_call(
        _s5_post_kernel,
        grid=(n // tt,),
        in_specs=[pl.BlockSpec((tt, W_C), lambda i: (i, 0)), pl.BlockSpec((W_C, W_C), lambda i: (0, 0)), vec, vec],
        out_specs=pl.BlockSpec((tt, W_C), lambda i: (i, 0)),
        out_shape=jax.ShapeDtypeStruct((n, W_C), F32),
        compiler_params=_params(("arbitrary",)),
        name="s5_post",
    )(y, w_glu.astype(BF16), b_glu.reshape(1, W_C), g_mix_c.reshape(1, W_C))


def _mix_kernel(ap_ref, bp_ref, cp_ref, as_ref, bs_ref, cs_ref, g_ref, o_ref):
    def emit(a_ref, b_ref, c_ref):
        o_ref[:, 0:W_A] = _rms(a_ref[...], g_ref[...]).astype(BF16)
        o_ref[:, W_A:W_A + W_B] = b_ref[...].astype(BF16)
        o_ref[:, W_A + W_B:D] = c_ref[...].astype(BF16)

    last = pl.program_id(0) == NT - 1

    @pl.when(jnp.logical_not(last))
    def _():
        emit(ap_ref, bp_ref, cp_ref)

    @pl.when(last)
    def _():
        emit(as_ref, bs_ref, cs_ref)


def _mix(a_p, b_p, c_p, a_s, b_s, c_s, g_mix_a):
    pr = lambda w: pl.BlockSpec((TM, w), lambda i: (jnp.minimum(i, NT - 2), 0))
    sm = lambda w: pl.BlockSpec((TM, w), lambda i: (0, 0))
    return pl.pallas_call(
        _mix_kernel,
        grid=(NT,),
        in_specs=[pr(W_A), pr(W_B), pr(W_C), sm(W_A), sm(W_B), sm(W_C), pl.BlockSpec((1, W_A), lambda i: (0, 0))],
        out_specs=pl.BlockSpec((TM, D), lambda i: (i, 0)),
        out_shape=jax.ShapeDtypeStruct((NP, D), BF16),
        compiler_params=_params(("arbitrary",)),
        name="mix",
    )(a_p, b_p, c_p, a_s, b_s, c_s, g_mix_a.reshape(1, W_A))


def _router_kernel(h_ref, w_ref, b_ref, ind_ref, wd_ref, cnt_ref):
    i = pl.program_id(0)
    scores = jax.nn.sigmoid(_dot(h_ref[...], w_ref[...]))
    lane = lax.broadcasted_iota(I32, (TD, N_EXPERTS), 1).astype(F32)
    row = lax.broadcasted_iota(I32, (TD, N_EXPERTS), 0) + i * TD
    sel = scores + b_ref[...]
    ind = jnp.zeros((TD, N_EXPERTS), F32)
    for _ in range(TOP_K):
        m = jnp.max(sel, axis=-1, keepdims=True)
        idx = jnp.min(jnp.where(sel == m, lane, float(N_EXPERTS)), axis=-1, keepdims=True)
        hit = lane == idx
        ind = jnp.where(hit, 1.0, ind)
        sel = jnp.where(hit, -jnp.inf, sel)
    ind = jnp.where(row < N_REAL, ind, 0.0)
    wsel = ind * scores
    wd_ref[...] = wsel / jnp.sum(wsel + (1.0 - jnp.max(ind, axis=-1, keepdims=True)) / N_EXPERTS,
                                 axis=-1, keepdims=True) * ROUTE_SCALE
    ind_ref[...] = ind

    @pl.when(i == 0)
    def _():
        cnt_ref[...] = jnp.zeros((8, N_EXPERTS), F32)

    cnt_ref[...] += jnp.sum(ind, axis=0, keepdims=True)


def _router(h2b, w_router, b_router):
    til = pl.BlockSpec((TD, N_EXPERTS), lambda i: (i, 0))
    return pl.pallas_call(
        _router_kernel,
        grid=(NP // TD,),
        in_specs=[pl.BlockSpec((TD, D), lambda i: (i, 0)), pl.BlockSpec((D, N_EXPERTS), lambda i: (0, 0)),
                  pl.BlockSpec((1, N_EXPERTS), lambda i: (0, 0))],
        out_specs=[til, til, pl.BlockSpec((8, N_EXPERTS), lambda i: (0, 0))],
        out_shape=[jax.ShapeDtypeStruct((NP, N_EXPERTS), F32)] * 2 + [jax.ShapeDtypeStruct((8, N_EXPERTS), F32)],
        compiler_params=_params(("arbitrary",)),
        name="router",
    )(h2b, w_router.astype(BF16), b_router.reshape(1, N_EXPERTS))


def _group_starts(cnt):
    padded = jnp.floor((cnt + (BM - 1)) / BM) * BM
    tri = (lax.broadcasted_iota(I32, (N_EXPERTS, N_EXPERTS), 0)
           <= lax.broadcasted_iota(I32, (N_EXPERTS, N_EXPERTS), 1)).astype(F32)
    pend = jnp.dot(padded, tri, preferred_element_type=F32, precision=lax.Precision.HIGHEST)
    return padded, pend - padded, pend


def _plan_kernel(cnt_ref, be_ref, misc_ref):
    cnt = cnt_ref[...]
    _, pstart, pend = _group_starts(cnt)
    rowi = (lax.broadcasted_iota(I32, (NBP, N_EXPERTS), 0) * BM).astype(F32)
    be = jnp.minimum(jnp.sum(jnp.where(pend[0:1] <= rowi, 1.0, 0.0), axis=-1, keepdims=True), N_EXPERTS - 1.0)
    be_ref[...] = jnp.broadcast_to(be, (NBP, 128)).astype(I32)
    misc_ref[0:8, :] = pstart.astype(I32)
    misc_ref[8:16, :] = cnt.astype(I32)
    misc_ref[16:24, :] = (pend / BM).astype(I32)


def _plan(cnt):
    return pl.pallas_call(
        _plan_kernel,
        out_shape=[jax.ShapeDtypeStruct((NBP, 128), I32), jax.ShapeDtypeStruct((24, N_EXPERTS), I32)],
        name="moe_plan",
    )(cnt)


def _slots_kernel(ind_ref, wd_ref, cnt_ref, dest_ref, w_ref, run_ref):
    @pl.when(pl.program_id(0) == 0)
    def _():
        run_ref[...] = jnp.zeros((8, N_EXPERTS), F32)

    ind = ind_ref[...]
    _, pstart, _ = _group_starts(cnt_ref[...])
    ltri = (lax.broadcasted_iota(I32, (TD, TD), 1) < lax.broadcasted_iota(I32, (TD, TD), 0)).astype(BF16)
    pos = _dot(ltri, ind.astype(BF16)) + run_ref[0:1, :]
    run_ref[...] += jnp.sum(ind, axis=0, keepdims=True)
    cur = jnp.where(ind > 0.0, pstart[0:1] + pos, BIG)
    lane = lax.broadcasted_iota(I32, (TD, 128), 1)
    dest = jnp.full((TD, 128), -1.0, F32)
    wts = jnp.zeros((TD, 128), F32)
    wd = wd_ref[...]
    for k in range(TOP_K):
        m = jnp.min(cur, axis=-1, keepdims=True)
        hit = cur == m
        wk = jnp.sum(jnp.where(hit, wd, 0.0), axis=-1, keepdims=True)
        dest = jnp.where(lane == k, jnp.where(m < BIG, m, -1.0), dest)
        wts = jnp.where(lane == k, jnp.where(m < BIG, wk, 0.0), wts)
        cur = jnp.where(hit, BIG, cur)
    dest_ref[...] = dest.astype(I32)
    w_ref[...] = wts


def _slots(ind, wd, cnt):
    til = pl.BlockSpec((TD, N_EXPERTS), lambda i: (i, 0))
    out = pl.BlockSpec((TD, 128), lambda i: (i, 0))
    return pl.pallas_call(
        _slots_kernel,
        grid=(NP // TD,),
        in_specs=[til, til, pl.BlockSpec((8, N_EXPERTS), lambda i: (0, 0))],
        out_specs=[out, out],
        out_shape=[jax.ShapeDtypeStruct((NP, 128), I32), jax.ShapeDtypeStruct((NP, 128), F32)],
        scratch_shapes=[pltpu.VMEM((8, N_EXPERTS), F32)],
        compiler_params=_params(("arbitrary",)),
        name="moe_slots",
    )(ind, wd, cnt)


def _dispatch_kernel(ps_ref, cnt_ref, nbu_ref, dest_ref, x_ref, xs_ref, zero_ref, sem):
    i = pl.program_id(0)

    def token(ref, t):
        start = t * ROW_TILES if isinstance(t, int) else pl.multiple_of(t * ROW_TILES, ROW_TILES)
        return ref.at[pl.ds(start, ROW_TILES), :]

    def row_copy(r, k):
        return pltpu.make_async_copy(token(x_ref, r), token(xs_ref, dest_ref[r * 8 + k]), sem)

    def zero_row(d):
        return pltpu.make_async_copy(token(zero_ref, 0), token(xs_ref, d), sem)

    def zero_block(b):
        rows = BM * ROW_TILES
        return pltpu.make_async_copy(zero_ref, xs_ref.at[pl.ds(pl.multiple_of(b * rows, rows), rows), :], sem)

    @pl.when(i < N_DTILES)
    def _():
        n_rows = jnp.minimum(TD, N_REAL - i * TD)

        def issue(r, c):
            for k in range(TOP_K):
                row_copy(r, k).start(priority=k % 2)
            return c

        def drain(r, c):
            for k in range(TOP_K):
                row_copy(r, k).wait()
            return c

        lax.fori_loop(0, n_rows, issue, 0)
        lax.fori_loop(0, n_rows, drain, 0)

    @pl.when(i == N_DTILES)
    def _():
        zero_ref[...] = jnp.zeros(zero_ref.shape, F32)

        def per_expert(e, c):
            cnt = cnt_ref[e]
            base = ps_ref[e]
            end = ((cnt + (BM - 1)) >> BM_SHIFT) << BM_SHIFT
            lax.fori_loop(cnt, end, lambda j, c2: (zero_row(base + j).start(), c2)[1], 0)
            lax.fori_loop(cnt, end, lambda j, c2: (zero_row(base + j).wait(), c2)[1], 0)
            return c

        lax.fori_loop(0, N_EXPERTS, per_expert, 0)
        lax.fori_loop(nbu_ref[0], NB, lambda b, c: (zero_block(b).start(), c)[1], 0)
        lax.fori_loop(nbu_ref[0], NB, lambda b, c: (zero_block(b).wait(), c)[1], 0)


def _dispatch(pstart_i, cnt_i, nbu, dest_flat, x):
    return pl.pallas_call(
        _dispatch_kernel,
        grid_spec=pltpu.PrefetchScalarGridSpec(
            num_scalar_prefetch=3, grid=(N_DTILES + 1,),
            in_specs=[pl.BlockSpec((TD * 8,), lambda i, *_: (jnp.minimum(i, N_DTILES - 1),),
                                   memory_space=pltpu.SMEM),
                      pl.BlockSpec((TD * ROW_TILES, 128), lambda i, *_: (jnp.minimum(i, N_DTILES - 1), 0))],
            out_specs=pl.BlockSpec(memory_space=pl.ANY),
            scratch_shapes=[pltpu.VMEM((BM * ROW_TILES, 128), F32), pltpu.SemaphoreType.DMA(())]),
        out_shape=jax.ShapeDtypeStruct((CAP * ROW_TILES, 128), F32),
        compiler_params=_params(("arbitrary",)),
        name="moe_dispatch",
    )(pstart_i, cnt_i, nbu, dest_flat, x)


def _expert_kernel(be_ref, nbu_ref, gend_ref, x_ref, wg_hbm, wu_hbm, wd_hbm, o_ref,
                   wg_ref, wu_ref, wd_ref, slot_ref, sem, *, layer):
    i = pl.program_id(0)
    nbu = nbu_ref[0]

    def weight_copies(e, s):
        return [pltpu.make_async_copy(hbm.at[layer, e], buf.at[s], sem.at[s, j])
                for j, (hbm, buf) in enumerate(((wg_hbm, wg_ref), (wu_hbm, wu_ref), (wd_hbm, wd_ref)))]

    @pl.when(i < nbu)
    def _():
        e = be_ref[i]

        @pl.when(i == 0)
        def _():
            slot_ref[0] = 0
            for cp in weight_copies(e, 0):
                cp.start()

        new_group = jnp.logical_or(i == 0, be_ref[jnp.maximum(i - 1, 0)] != e)

        @pl.when(jnp.logical_and(new_group, i > 0))
        def _():
            slot_ref[0] = 1 - slot_ref[0]

        s = slot_ref[0]

        @pl.when(new_group)
        def _():
            for cp in weight_copies(e, s):
                cp.wait()
            nxt = gend_ref[e]

            @pl.when(nxt < nbu)
            def _():
                for cp in weight_copies(be_ref[jnp.minimum(nxt, NB - 1)], 1 - s):
                    cp.start()

        x = _load_token_rows(x_ref, BM).astype(BF16)
        g = _dot(x, wg_ref[s].astype(BF16))
        u = _dot(x, wu_ref[s].astype(BF16))
        a = (g * jax.nn.sigmoid(g) * u).astype(BF16)
        _store_token_rows(o_ref, _dot(a, wd_ref[s].astype(BF16)))

    @pl.when(i >= nbu)
    def _():
        o_ref[...] = jnp.zeros(o_ref.shape, F32)


def _experts(layer, be, nbu, group_end, xs, w_gate, w_up, w_down):
    any_spec = pl.BlockSpec(memory_space=pl.ANY)
    return pl.pallas_call(
        functools.partial(_expert_kernel, layer=layer),
        grid_spec=pltpu.PrefetchScalarGridSpec(
            num_scalar_prefetch=3, grid=(NB,),
            in_specs=[pl.BlockSpec((BM * ROW_TILES, 128), lambda i, be, nbu, ge: (jnp.minimum(i, nbu[0] - 1), 0)),
                      any_spec, any_spec, any_spec],
            out_specs=pl.BlockSpec((BM * ROW_TILES, 128), lambda i, be, nbu, ge: (i, 0)),
            scratch_shapes=[pltpu.VMEM((2, D, D_EXPERT), F32), pltpu.VMEM((2, D, D_EXPERT), F32),
                            pltpu.VMEM((2, D_EXPERT, D), F32), pltpu.SMEM((1,), I32),
                            pltpu.SemaphoreType.DMA((2, 3))]),
        out_shape=jax.ShapeDtypeStruct((CAP * ROW_TILES, 128), F32),
        compiler_params=_params(("arbitrary",)),
        name="moe_experts",
    )(be, nbu, group_end, xs, w_gate, w_up, w_down)


def _shared_kernel(h_ref, wg_ref, wu_ref, wd_ref, o_ref):
    h = h_ref[...]
    g = _dot(h, wg_ref[...])
    u = _dot(h, wu_ref[...])
    o_ref[...] = _dot((g * jax.nn.sigmoid(g) * u).astype(BF16), wd_ref[...])


def _shared(h2b, wg, wu, wd):
    return pl.pallas_call(
        _shared_kernel,
        grid=(NT,),
        in_specs=[pl.BlockSpec((TM, D), lambda i: (i, 0)),
                  pl.BlockSpec((D, D_SHARED), lambda i: (0, 0)),
                  pl.BlockSpec((D, D_SHARED), lambda i: (0, 0)),
                  pl.BlockSpec((D_SHARED, D), lambda i: (0, 0))],
        out_specs=pl.BlockSpec((TM, D), lambda i: (i, 0)),
        out_shape=jax.ShapeDtypeStruct((NP, D), F32),
        compiler_params=_params(("arbitrary",)),
        name="moe_shared",
    )(h2b, wg.astype(BF16), wu.astype(BF16), wd.astype(BF16))


def _combine_kernel(dest_ref, nxt_ref, w_ref, ys_ref, sh_ref, x_ref, gr_ref, gt_ref, o_ref, rows_ref, acc_ref, sem):
    i = pl.program_id(0)
    n_tiles = NP // TC_
    slot = i % 2

    def gather(d_ref, s, start):
        def body(r, c):
            for k in range(TOP_K):
                d = jnp.maximum(d_ref[r * 8 + k], 0)
                cp = pltpu.make_async_copy(
                    ys_ref.at[pl.ds(pl.multiple_of(d * ROW_TILES, ROW_TILES), ROW_TILES), :],
                    rows_ref.at[s, k, pl.ds(pl.multiple_of(r * ROW_TILES, ROW_TILES), ROW_TILES), :], sem.at[s])
                if start:
                    cp.start(priority=k % 2)
                else:
                    cp.wait()
            return c

        lax.fori_loop(0, TC_, body, 0)

    @pl.when(i == 0)
    def _():
        gather(dest_ref, 0, True)

    @pl.when(i + 1 < n_tiles)
    def _():
        gather(nxt_ref, 1 - slot, True)

    gather(dest_ref, slot, False)
    def weigh(t, c):
        rows = pl.ds(pl.multiple_of(t * ROW_TILES, ROW_TILES), ROW_TILES)
        acc = w_ref[t * 8] * rows_ref[slot, 0, rows, :]
        for k in range(1, TOP_K):
            acc = acc + w_ref[t * 8 + k] * rows_ref[slot, k, rows, :]
        acc_ref[rows, :] = acc
        return c

    lax.fori_loop(0, TC_, weigh, 0, unroll=4)
    routed = _load_token_rows(acc_ref, TC_)
    last = i >= N_PROMPT // TC_
    o_ref[...] = x_ref[...] + _pick(last, gt_ref, gr_ref) * (routed + sh_ref[...])


def _combine(dest_flat, w6, ys, shared, x, gate):
    n_tiles = NP // TC_
    n_p = N_PROMPT // TC_
    per = TM // TC_
    til = pl.BlockSpec((TC_, D), lambda i: (i, 0))
    return pl.pallas_call(
        _combine_kernel,
        grid=(n_tiles,),
        in_specs=[pl.BlockSpec((TC_ * 8,), lambda i: (i,), memory_space=pltpu.SMEM),
                  pl.BlockSpec((TC_ * 8,), lambda i: (jnp.minimum(i + 1, n_tiles - 1),), memory_space=pltpu.SMEM),
                  pl.BlockSpec((TC_ * 8,), lambda i: (i,), memory_space=pltpu.SMEM),
                  pl.BlockSpec(memory_space=pl.ANY), til, til,
                  pl.BlockSpec((None, 1, D), lambda i: (i // per, 0, 0)),
                  pl.BlockSpec((TC_, D), lambda i: (jnp.maximum(i - n_p, 0), 0))],
        out_specs=til,
        out_shape=jax.ShapeDtypeStruct((NP, D), F32),
        scratch_shapes=[pltpu.VMEM((2, TOP_K, TC_ * ROW_TILES, 128), F32), pltpu.VMEM((TC_ * ROW_TILES, 128), F32),
                        pltpu.SemaphoreType.DMA((2,))],
        compiler_params=_params(("arbitrary",)),
        name="moe_combine",
    )(dest_flat, dest_flat, w6, ys, shared, x, gate[0], gate[1])


def _pad_rows(a, rows):
    return jnp.concatenate([a, jnp.zeros((rows - a.shape[0],) + a.shape[1:], a.dtype)], axis=0)


def _s5_layer(z, lw, h0r_s, h0i_s):
    g_mix_c = lw["g_mix"][W_A + W_B:]
    nj = nsc = 16
    state = lambda h, b: h.reshape(S5_NLB, b, S5_GPB, SSM_STATE).transpose(1, 0, 2, 3).reshape(b, G_C, SSM_STATE)
    uc_p = z[:N_PROMPT, IN_COLS - W_C:]
    u_p = uc_p.reshape(B_P, nsc, nj, S5_L_P, S5_NLB, 128).transpose(4, 2, 1, 0, 3, 5)
    u_p = u_p.reshape(S5_NLB, nj * nsc * B_P, S5_L_P * 128)
    zeros_h = jnp.zeros((S5_NLB, B_P, S5_SW), F32)
    y_p, hr_p, hi_p = _s5(u_p, lw, zeros_h, zeros_h, S5_L_P, nj, nsc, B_P)
    y_p = y_p.reshape(S5_NLB, nj, nsc, B_P, S5_L_P, 128).transpose(3, 2, 1, 4, 0, 5).reshape(N_PROMPT, W_C)
    c_p = _s5_post(y_p, lw["w_glu"], lw["b_glu"], g_mix_c)

    uc_s = z[N_PROMPT:N_REAL, IN_COLS - W_C:]
    u_s = uc_s.reshape(B_S, T_S, S5_NLB, 128).transpose(2, 0, 1, 3).reshape(S5_NLB, B_S, S5_L_S * 128)
    h0 = lambda h: h.reshape(B_S, S5_NLB, S5_SW).transpose(1, 0, 2)
    y_s, hr_s, hi_s = _s5(u_s, lw, h0(h0r_s), h0(h0i_s), S5_L_S, 1, 1, B_S)
    y_s = y_s.reshape(S5_NLB, B_S, T_S, 128).transpose(1, 2, 0, 3).reshape(N_SAMPLE, W_C)
    c_s = _s5_post(y_s, lw["w_glu"], lw["b_glu"], g_mix_c)
    return c_p, c_s, state(hr_p, B_P), state(hi_p, B_P), state(hr_s, B_S), state(hi_s, B_S)


def _layer(x, l, lw, experts, mod, bias_p, bias_s, cache_k, cache_v, h0r_s, h0i_s):
    sh1, sc1, g1, sh2, sc2, g2 = [_tile_tabs(m) for m in jnp.split(mod, 6, axis=-1)]
    (h,) = _modnorm(x, lw["g_attn"], sc1, sh1, False)
    z = _matmul(h, lw["w_in"].astype(BF16), 1536)

    a_p = _attn_prompt(z, bias_p)
    a_s = _attn_sample(z, cache_k, cache_v, l, bias_s)

    g_mix_b = lw["g_mix"][W_A:W_A + W_B]
    gm = (lw["g_bv"], lw["b_bv"], lw["w_s"], lw["b_s"], g_mix_b)
    b_p, _ = _gmlp(z, N_PROMPT // CHUNK, 3 * W_A // W_B, 3 * W_A // W_B + 1, *gm)
    uv_s = z[N_PROMPT:N_REAL, 3 * W_A:3 * W_A + 2 * W_B].reshape(B_S, T_S, 2 * W_B)
    uv_s = jnp.pad(uv_s, ((0, 0), (0, CHUNK - T_S), (0, 0))).reshape(B_S * CHUNK, 2 * W_B)
    b_s, vn_s = _gmlp(uv_s, B_S, 0, 1, *gm)
    b_s = b_s.reshape(B_S, CHUNK, W_B)[:, :T_S].reshape(N_SAMPLE, W_B)
    new_vb = vn_s.reshape(B_S, CHUNK, W_B)[:, :T_S]

    c_p, c_s, hr_p, hi_p, hr_s, hi_s = _s5_layer(z, lw, h0r_s, h0i_s)

    mixed = _mix(a_p, b_p, c_p, _pad_rows(a_s, TM), _pad_rows(b_s, TM), _pad_rows(c_s, TM), lw["g_mix"][:W_A])
    x = _matmul_residual(mixed, lw["w_out"].astype(BF16), x, g1, 1024)

    h2b, h2 = _modnorm(x, lw["g_ffn"], sc2, sh2, True)
    ind, wd, cnt = _router(h2b, lw["w_router"], lw["b_router"])
    be, misc = _plan(cnt)
    dest, w6 = _slots(ind, wd, cnt)
    dest_flat = dest[:, :8].reshape(-1)
    xs = _dispatch(misc[0], misc[8], misc[16, N_EXPERTS - 1:], dest_flat, h2)
    ys = _experts(l, be[:NB, 0], misc[16, N_EXPERTS - 1:], misc[16], xs, *experts)
    shared = _shared(h2b, lw["w_sh_gate"], lw["w_sh_up"], lw["w_sh_down"])
    x = _combine(dest_flat, w6[:, :8].reshape(-1), ys, shared, x, g2)

    kv = lambda lo: (z[:N_PROMPT, lo:lo + W_A].reshape(B_P, T_P, H_A, HEAD_DIM)[:, T_P - BUF:],
                     z[N_PROMPT:N_REAL, lo:lo + W_A].reshape(B_S, T_S, H_A, HEAD_DIM))
    (k_p, k_s), (v_p, v_s) = kv(W_A), kv(2 * W_A)
    return x, (k_p, v_p, k_s, v_s, new_vb, hr_p, hi_p, hr_s, hi_s)


def kernel(x_prompt, x_sample, cache_a_k, cache_a_v, state_c_re, state_c_im, c_prompt, c_sample, rel_bias, w_ada, b_ada, g_attn, w_in, g_bv, b_bv, w_s, b_s, lam_re, lam_im, log_dt, ssm_b_re, ssm_b_im, ssm_c_re, ssm_c_im, ssm_d, w_glu, b_glu, g_mix, w_out, g_ffn, w_router, b_router, w_gate, w_up, w_down, w_sh_gate, w_sh_up, w_sh_down, g_final):
    assert x_prompt.shape == (B_P, T_P, D) and x_sample.shape == (B_S, T_S, D)
    assert cache_a_k.shape == (DEPTH, B_S, BUF, H_A, HEAD_DIM)
    per_layer = dict(g_attn=g_attn, w_in=w_in, g_bv=g_bv, b_bv=b_bv, w_s=w_s, b_s=b_s, lam_re=lam_re,
                     lam_im=lam_im, log_dt=log_dt, ssm_b_re=ssm_b_re, ssm_b_im=ssm_b_im, ssm_c_re=ssm_c_re,
                     ssm_c_im=ssm_c_im, ssm_d=ssm_d, w_glu=w_glu, b_glu=b_glu, g_mix=g_mix, w_out=w_out,
                     g_ffn=g_ffn, w_router=w_router, b_router=b_router, w_sh_gate=w_sh_gate, w_sh_up=w_sh_up,
                     w_sh_down=w_sh_down)

    x = _pad_rows(jnp.concatenate([x_prompt.reshape(N_PROMPT, D), x_sample.reshape(N_SAMPLE, D)], axis=0), NP)
    c_all = _pad_rows(jnp.concatenate([c_prompt, c_sample], axis=0), 16)
    mod = _mod_all(c_all, w_ada, b_ada)
    bias_p = _bias_bank(rel_bias, *_prompt_bias_tables())
    bias_s = _bias_bank(rel_bias, *_sample_bias_tables())[0]
    cache_k = cache_a_k.reshape(DEPTH * B_S, BUF, W_A)
    cache_v = cache_a_v.reshape(DEPTH * B_S, BUF, W_A)

    outs = []
    for l in range(DEPTH):
        lw = {name: w[l] for name, w in per_layer.items()}
        x, o = _layer(x, l, lw, (w_gate, w_up, w_down), mod[l], bias_p, bias_s, cache_k, cache_v,
                      state_c_re[l], state_c_im[l])
        outs.append(o)
    y = _finalnorm(x, g_final)
    stacked = [jnp.stack([o[j] for o in outs]) for j in range(9)]
    return (y[:N_PROMPT].reshape(B_P, T_P, D), y[N_PROMPT:N_REAL].reshape(B_S, T_S, D), *stacked)
```

```python
import functools
import math

import jax
import jax.numpy as jnp
import numpy as np
from jax import lax
from jax.experimental import pallas as pl
from jax.experimental.pallas import tpu as pltpu

F32 = jnp.float32
BF16 = jnp.bfloat16
I32 = jnp.int32

D = 2048
B_P, T_P = 2, 4096
B_S, T_S = 8, 8
DEPTH = 2
HEAD_DIM = 64
W_A = 1024
H_A = 16
W_B = 512
G_B = 8
W_C = 512
SSM_GROUP = 16
G_C = 32
SSM_STATE = 64
IN_COLS = 3 * W_A + 2 * W_B + W_C
DILATIONS = (1, 4, 16)
NK = 128
ATTN_UNITS = (8, 8, 8)
BUF = 2048
CHUNK = 128
N_BUCKETS = 32
REL_MAX_DIST = 2048
N_EXPERTS = 64
TOP_K = 6
D_EXPERT = 512
D_SHARED = 1024
ROUTE_SCALE = 2.5
EPS = 1e-6

N_PROMPT = B_P * T_P
N_SAMPLE = B_S * T_S
N_REAL = N_PROMPT + N_SAMPLE
TM = 512
NT = N_PROMPT // TM + 1
NP = NT * TM
TD = 256
N_DTILES = -(-N_REAL // TD)
ROW_TILES = D // 128
BM = 256
BM_SHIFT = 8
NB = -(-(N_REAL * TOP_K + N_EXPERTS * (BM - 1)) // BM)
NBP = -(-NB // 8) * 8
CAP = NB * BM
TC_ = 128
KF_ROWS = 2176
S5_L_P = 16
S5_L_S = T_S
S5_GPB = 128 // SSM_GROUP
S5_NLB = G_C // S5_GPB
S5_SW = S5_GPB * SSM_STATE
BIG = 1e9
assert BM == 1 << BM_SHIFT

VMEM_LIMIT_BYTES = 56 * 1024 * 1024


def _params(dims):
    return pltpu.CompilerParams(dimension_semantics=dims, vmem_limit_bytes=VMEM_LIMIT_BYTES)


def _dot(a, b):
    return jnp.dot(a, b, preferred_element_type=F32)


def _dot_nt(a, b, precision=None):
    return lax.dot_general(a, b, (((1,), (1,)), ((), ())), preferred_element_type=F32,
                           precision=precision)


def _rms(x, g):
    return x * lax.rsqrt(jnp.mean(x * x, axis=-1, keepdims=True) + EPS) * g


def _mod_kernel(c_ref, w_ref, b_ref, o_ref):
    c = c_ref[...]
    s = (c * jax.nn.sigmoid(c)).astype(BF16)
    o_ref[...] = _dot(s, w_ref[...].astype(BF16)) + b_ref[...]


def _mod_all(c_all, w_ada, b_ada):
    tn = 1024
    return pl.pallas_call(
        _mod_kernel,
        grid=(DEPTH, 6 * D // tn),
        in_specs=[pl.BlockSpec((16, D), lambda l, j: (0, 0)),
                  pl.BlockSpec((None, D, tn), lambda l, j: (l, 0, j)),
                  pl.BlockSpec((None, 1, tn), lambda l, j: (l, 0, j))],
        out_specs=pl.BlockSpec((None, 16, tn), lambda l, j: (l, 0, j)),
        out_shape=jax.ShapeDtypeStruct((DEPTH, 16, 6 * D), F32),
        compiler_params=_params(("arbitrary", "arbitrary")),
        name="mod",
    )(c_all, w_ada, b_ada.reshape(DEPTH, 1, 6 * D))


def _tile_tabs(m):
    row = jnp.concatenate([jnp.repeat(m[0:B_P], NT // B_P, axis=0), jnp.zeros((1, m.shape[1]), F32)], 0)
    tok = jnp.concatenate([jnp.repeat(m[B_P:B_P + B_S], T_S, axis=0),
                           jnp.zeros((TM - N_SAMPLE, m.shape[1]), F32)], 0)
    return row[:, None, :], tok


def _pick(last, tok_ref, row_ref):
    return jnp.where(last, tok_ref[...], row_ref[...])


def _modnorm_kernel(x_ref, g_ref, scr_ref, sct_ref, shr_ref, sht_ref, *o_refs):
    last = pl.program_id(0) == NT - 1
    y = _rms(x_ref[...], g_ref[...])
    y = y * (1.0 + _pick(last, sct_ref, scr_ref)) + _pick(last, sht_ref, shr_ref)
    o_refs[0][...] = y.astype(BF16)
    if len(o_refs) > 1:
        _store_token_rows(o_refs[1], y)


def _load_token_rows(ref, n, first=0):
    return jnp.concatenate([ref[pl.ds(first * ROW_TILES + j, n, stride=ROW_TILES), :] for j in range(ROW_TILES)],
                           axis=1)


def _store_token_rows(ref, val, first=0):
    for j in range(ROW_TILES):
        ref[pl.ds(first * ROW_TILES + j, val.shape[0], stride=ROW_TILES), :] = val[:, j * 128:(j + 1) * 128]


def _modnorm(x, g, sc, sh, with_token_rows):
    row = pl.BlockSpec((None, 1, D), lambda i: (i, 0, 0))
    tok = pl.BlockSpec((TM, D), lambda i: (0, 0))
    til = pl.BlockSpec((TM, D), lambda i: (i, 0))
    out_specs, out_shape = [til], [jax.ShapeDtypeStruct((NP, D), BF16)]
    if with_token_rows:
        out_specs.append(pl.BlockSpec((TM * ROW_TILES, 128), lambda i: (i, 0)))
        out_shape.append(jax.ShapeDtypeStruct((NP * ROW_TILES, 128), F32))
    return pl.pallas_call(
        _modnorm_kernel,
        grid=(NT,),
        in_specs=[til, pl.BlockSpec((1, D), lambda i: (0, 0)), row, tok, row, tok],
        out_specs=out_specs,
        out_shape=out_shape,
        compiler_params=_params(("arbitrary",)),
        name="modnorm",
    )(x, g.reshape(1, D), sc[0], sc[1], sh[0], sh[1])


def _finalnorm_kernel(x_ref, g_ref, o_ref):
    o_ref[...] = _rms(x_ref[...], g_ref[...])


def _finalnorm(x, g):
    til = pl.BlockSpec((TM, D), lambda i: (i, 0))
    return pl.pallas_call(
        _finalnorm_kernel,
        grid=(NT,),
        in_specs=[til, pl.BlockSpec((1, D), lambda i: (0, 0))],
        out_specs=til,
        out_shape=jax.ShapeDtypeStruct((NP, D), F32),
        compiler_params=_params(("arbitrary",)),
        name="finalnorm",
    )(x, g.reshape(1, D))


def _mm_kernel(a_ref, b_ref, o_ref):
    o_ref[...] = _dot(a_ref[...], b_ref[...])


def _matmul(a, b, tn):
    k, n = b.shape
    return pl.pallas_call(
        _mm_kernel,
        grid=(n // tn, NT),
        in_specs=[pl.BlockSpec((TM, k), lambda c, i: (i, 0)),
                  pl.BlockSpec((k, tn), lambda c, i: (0, c))],
        out_specs=pl.BlockSpec((TM, tn), lambda c, i: (i, c)),
        out_shape=jax.ShapeDtypeStruct((NP, n), F32),
        compiler_params=_params(("arbitrary", "arbitrary")),
        name="matmul",
    )(a, b)


def _mm_res_kernel(a_ref, b_ref, x_ref, gr_ref, gt_ref, o_ref):
    last = pl.program_id(1) == NT - 1
    o_ref[...] = x_ref[...] + _pick(last, gt_ref, gr_ref) * _dot(a_ref[...], b_ref[...])


def _matmul_residual(a, b, x, gate, tn):
    k, n = b.shape
    return pl.pallas_call(
        _mm_res_kernel,
        grid=(n // tn, NT),
        in_specs=[pl.BlockSpec((TM, k), lambda c, i: (i, 0)),
                  pl.BlockSpec((k, tn), lambda c, i: (0, c)),
                  pl.BlockSpec((TM, tn), lambda c, i: (i, c)),
                  pl.BlockSpec((None, 1, tn), lambda c, i: (i, 0, c)),
                  pl.BlockSpec((TM, tn), lambda c, i: (0, c))],
        out_specs=pl.BlockSpec((TM, tn), lambda c, i: (i, c)),
        out_shape=jax.ShapeDtypeStruct((NP, n), F32),
        compiler_params=_params(("arbitrary", "arbitrary")),
        name="matmul_residual",
    )(a, b, x, gate[0], gate[1])


def _t5_bucket(dist):
    max_exact = N_BUCKETS // 2
    dist = np.maximum(dist, 0)
    ratio = (np.log(np.maximum(dist, 1).astype(np.float32) / np.float32(max_exact))
             / np.float32(math.log(REL_MAX_DIST / max_exact)))
    large = np.minimum(max_exact + (ratio * np.float32(N_BUCKETS - max_exact)).astype(np.int32), N_BUCKETS - 1)
    return np.where(dist < max_exact, dist, large)


def _bias_kernel(rb_ref, bucket_ref, add_ref, o_ref):
    h = pl.program_id(1)
    bucket = bucket_ref[...]
    acc = jnp.full(bucket.shape, -jnp.inf, F32)
    for b in range(N_BUCKETS):
        acc = jnp.where(bucket == b, rb_ref[b * H_A + h], acc)
    o_ref[...] = acc + add_ref[...]


def _bias_bank(rel_bias, bucket, add):
    n, r, c = bucket.shape
    blk = pl.BlockSpec((None, r, c), lambda i, h, rb: (i, 0, 0))
    return pl.pallas_call(
        _bias_kernel,
        grid_spec=pltpu.PrefetchScalarGridSpec(
            num_scalar_prefetch=1, grid=(n, H_A), in_specs=[blk, blk],
            out_specs=pl.BlockSpec((None, None, r, c), lambda i, h, rb: (i, h, 0, 0))),
        out_shape=jax.ShapeDtypeStruct((n, H_A, r, c), F32),
        compiler_params=_params(("arbitrary", "arbitrary")),
        name="bias_bank",
    )(rel_bias.reshape(-1), bucket, add)


def _prompt_bias_tables():
    qi = np.arange(NK)[:, None]
    kj = np.arange(2 * NK)[None, :]
    rel = qi + NK - kj
    band = (rel >= 0) & (rel <= NK)
    bucket = np.stack([np.where(band, _t5_bucket(rel * d), -1) for d in DILATIONS]).astype(np.int32)
    return jnp.asarray(bucket), jnp.zeros(bucket.shape, F32)


def _sample_bias_tables():
    dist_np = BUF + np.arange(T_S)[:, None] - np.arange(KF_ROWS)[None, :]
    mult = np.zeros(dist_np.shape, np.float64)
    for d in DILATIONS:
        mult += (dist_np >= 0) & (dist_np % d == 0) & (dist_np // d <= NK)
    mult = np.where(np.arange(KF_ROWS)[None, :] < BUF + T_S, mult, 0.0)
    valid = mult > 0
    add = np.where(valid, np.log(np.maximum(mult, 1.0)), 0.0).astype(np.float32)
    bucket = np.where(valid, _t5_bucket(dist_np), -1).astype(np.int32)
    return jnp.asarray(bucket)[None], jnp.asarray(add)[None]


def _attn_prompt_kernel(q_ref, k_ref, v_ref, bias_ref, o_ref, *stats):
    lane = lax.broadcasted_iota(I32, (NK, 128), 1)
    head0 = lane < HEAD_DIM
    prev_cols = lax.broadcasted_iota(I32, (NK, 2 * NK), 1) < NK
    ones = jnp.ones((NK, 128), F32)
    n_br = len(DILATIONS)
    m_refs, l_refs, a_refs = stats[0:n_br], stats[n_br:2 * n_br], stats[2 * n_br:3 * n_br]
    for di in range(n_br):
        d = DILATIONS[di]
        n_blocks = T_P // (d * NK)
        n_units = ATTN_UNITS[di]
        run = min(n_units, n_blocks)
        assert n_units % run == 0 and n_blocks % run == 0

        def group(gi, carry, di=di, d=d, n_blocks=n_blocks, n_units=n_units, run=run):
            def idx(start):
                return pl.ds(start, NK) if d == 1 else pl.ds(start, NK, stride=d)

            def keys(start):
                return k_ref[idx(start), :].astype(BF16)

            def values(start):
                return jnp.concatenate([v_ref[idx(start), :], ones], axis=1).astype(BF16)

            units = []
            for j in range(n_units):
                u = gi * n_units + j
                r = u >> int(math.log2(n_blocks))
                n = u & (n_blocks - 1)
                cur = r + d * NK * n
                if j % run == 0:
                    prv = r + d * NK * jnp.maximum(n - 1, 0)
                    k_prev, v_prev = keys(prv), values(prv)
                    pen = jnp.where(n == 0, -jnp.inf, 0.0).astype(F32)
                else:
                    k_prev, v_prev, pen = k_cur, v_cur, None
                k_cur, v_cur = keys(cur), values(cur)
                units.append(dict(cur=cur, pen=pen, q=q_ref[idx(cur), :] * (HEAD_DIM ** -0.5),
                                  k2=jnp.concatenate([k_prev, k_cur], axis=0),
                                  v2=jnp.concatenate([v_prev, v_cur], axis=0)))
            for unit in units:
                pv, rs, mn = [], [], []
                for h in range(2):
                    qh = jnp.where(head0 if h == 0 else ~head0, unit["q"], 0.0).astype(BF16)
                    s = _dot_nt(qh, unit["k2"]) + bias_ref[di, h]
                    if unit["pen"] is not None:
                        s = s + jnp.where(prev_cols, unit["pen"], 0.0)
                    m_h = jnp.max(s, axis=-1, keepdims=True)
                    pv_rs = _dot(jnp.exp(s - m_h).astype(BF16), unit["v2"])
                    pv.append(pv_rs[:, 0:128])
                    rs.append(pv_rs[:, 128:256])
                    mn.append(m_h)
                m_refs[di][idx(unit["cur"]), :] = jnp.where(head0, mn[0], mn[1])
                l_refs[di][idx(unit["cur"]), :] = jnp.where(head0, rs[0], rs[1])
                a_refs[di][idx(unit["cur"]), :] = jnp.where(head0, pv[0], pv[1])
            return carry

        lax.fori_loop(0, T_P // NK // n_units, group, 0)

    def merge(bi, carry):
        rows = pl.ds(pl.multiple_of(bi * NK, NK), NK)
        ms = [m_ref[rows, :] for m_ref in m_refs]
        m = functools.reduce(jnp.maximum, ms)
        num = jnp.zeros((NK, 128), F32)
        den = jnp.zeros((NK, 128), F32)
        for m_i, l_ref, a_ref in zip(ms, l_refs, a_refs):
            w = jnp.exp(m_i - m)
            num = num + w * a_ref[rows, :]
            den = den + w * l_ref[rows, :]
        o_ref[rows, :] = num / den
        return carry

    lax.fori_loop(0, T_P // NK, merge, 0, unroll=2)


def _attn_prompt(z, bias):
    blk = lambda off: pl.BlockSpec((T_P, 128), lambda b, h, off=off: (b, off + h))
    return pl.pallas_call(
        _attn_prompt_kernel,
        grid=(B_P, H_A // 2),
        in_specs=[blk(0), blk(W_A // 128), blk(2 * W_A // 128),
                  pl.BlockSpec((len(DILATIONS), 2, NK, 2 * NK), lambda b, h: (0, h, 0, 0))],
        out_specs=pl.BlockSpec((T_P, 128), lambda b, h: (b, h)),
        out_shape=jax.ShapeDtypeStruct((N_PROMPT, W_A), F32),
        scratch_shapes=[pltpu.VMEM((T_P, 128), F32)] * (3 * len(DILATIONS)),
        compiler_params=_params(("arbitrary", "arbitrary")),
        name="attn_prompt",
    )(z, z, z, bias)


def _attn_sample_kernel(q_ref, kn_ref, vn_ref, ck_ref, cv_ref, bias_ref, o_ref, kf_ref, vf_ref):
    for f_ref, c_ref, n_ref in ((kf_ref, ck_ref, kn_ref), (vf_ref, cv_ref, vn_ref)):
        f_ref[0:BUF, :] = c_ref[...]
        f_ref[BUF:BUF + T_S, :] = n_ref[...]
        f_ref[BUF + T_S:KF_ROWS, :] = jnp.zeros((KF_ROWS - BUF - T_S, 128), F32)
    head0 = lax.broadcasted_iota(I32, (T_S, 128), 1) < HEAD_DIM
    q = q_ref[...] * (HEAD_DIM ** -0.5)
    q2 = jnp.concatenate([jnp.where(head0, q, 0.0), jnp.where(head0, 0.0, q)], axis=0).astype(BF16)
    bias2 = jnp.concatenate([bias_ref[0], bias_ref[1]], axis=0)
    s = _dot_nt(q2, kf_ref[...].astype(BF16)) + bias2
    m = jnp.max(s, axis=-1, keepdims=True)
    p = jnp.exp(s - m)
    l = jnp.sum(p, axis=-1, keepdims=True)
    o2 = _dot(p.astype(BF16), vf_ref[...].astype(BF16)) / l
    o_ref[...] = jnp.where(head0, o2[0:T_S], o2[T_S:2 * T_S])


def _attn_sample(z, cache_k, cache_v, layer, bias):
    row0 = N_PROMPT // T_S
    blk = lambda off: pl.BlockSpec((T_S, 128), lambda b, h, off=off: (row0 + b, off + h))
    cache = pl.BlockSpec((None, BUF, 128), lambda b, h: (layer * B_S + b, 0, h))
    return pl.pallas_call(
        _attn_sample_kernel,
        grid=(B_S, H_A // 2),
        in_specs=[blk(0), blk(W_A // 128), blk(2 * W_A // 128), cache, cache,
                  pl.BlockSpec((2, T_S, KF_ROWS), lambda b, h: (h, 0, 0))],
        out_specs=pl.BlockSpec((T_S, 128), lambda b, h: (b, h)),
        out_shape=jax.ShapeDtypeStruct((N_SAMPLE, W_A), F32),
        scratch_shapes=[pltpu.VMEM((KF_ROWS, 128), F32)] * 2,
        compiler_params=_params(("arbitrary", "arbitrary")),
        name="attn_sample",
    )(z, z, z, cache_k, cache_v, bias)


def _gmlp_kernel(ub_ref, vb_ref, gbv_ref, bbv_ref, ws_ref, bs_ref, gmix_ref, bo_ref, vn_ref):
    v = jax.nn.gelu(vb_ref[...])
    mu = jnp.mean(v, axis=-1, keepdims=True)
    var = jnp.mean(jnp.square(v - mu), axis=-1, keepdims=True)
    vn = (v - mu) * lax.rsqrt(var + EPS) * gbv_ref[...] + bbv_ref[...]
    vn_ref[...] = vn
    vnb = vn.astype(BF16)
    causal = (lax.broadcasted_iota(I32, (CHUNK, CHUNK), 1) <= lax.broadcasted_iota(I32, (CHUNK, CHUNK), 0))
    first = lax.broadcasted_iota(I32, (CHUNK, 128), 1) < W_B // G_B
    parts = []
    for gp in range(G_B // 2):
        v2 = vnb[:, gp * 128:(gp + 1) * 128]
        m0 = _dot(jnp.where(causal, ws_ref[2 * gp], 0.0).astype(BF16), v2)
        m1 = _dot(jnp.where(causal, ws_ref[2 * gp + 1], 0.0).astype(BF16), v2)
        parts.append(jnp.where(first, m0, m1))
    mix = jnp.concatenate(parts, axis=1) + bs_ref[...]
    bo_ref[...] = _rms(jax.nn.gelu(ub_ref[...]) * mix, gmix_ref[...])


def _gmlp(z, n_chunks, col_u, col_v, g_bv, b_bv, w_s, b_s, g_mix_b):
    bs_wide = jnp.repeat(b_s.T, W_B // G_B, axis=1)
    vec = pl.BlockSpec((1, W_B), lambda i: (0, 0))
    out = pl.BlockSpec((CHUNK, W_B), lambda i: (i, 0))
    return pl.pallas_call(
        _gmlp_kernel,
        grid=(n_chunks,),
        in_specs=[pl.BlockSpec((CHUNK, W_B), lambda i: (i, col_u)),
                  pl.BlockSpec((CHUNK, W_B), lambda i: (i, col_v)),
                  vec, vec,
                  pl.BlockSpec((G_B, CHUNK, CHUNK), lambda i: (0, 0, 0)),
                  pl.BlockSpec((CHUNK, W_B), lambda i: (0, 0)),
                  vec],
        out_specs=[out, out],
        out_shape=[jax.ShapeDtypeStruct((n_chunks * CHUNK, W_B), F32)] * 2,
        compiler_params=_params(("arbitrary",)),
        name="gmlp",
    )(z, z, g_bv.reshape(1, W_B), b_bv.reshape(1, W_B), w_s, bs_wide, g_mix_b.reshape(1, W_B))


def _cmul(a, b):
    return a[0] * b[0] - a[1] * b[1], a[0] * b[1] + a[1] * b[0]


def _s5_kernel(u_ref, lr_ref, li_ref, ldt_ref, btr_ref, bti_ref, cr_ref, ci_ref, d_ref, h0r_ref, h0i_ref,
               y_ref, hr_ref, hi_ref, er_ref, ei_ref, gr_ref, gi_ref, *, L, nj, nsc, nb):
    hi = lax.Precision.HIGHEST
    lr, li = lr_ref[...], li_ref[...]
    dt = jnp.exp(ldt_ref[...])
    mag = jnp.exp(lr * dt)
    a = (mag * jnp.cos(li * dt), mag * jnp.sin(li * dt))
    den = lr * lr + li * li
    q = (((a[0] - 1.0) * lr + a[1] * li) / den, (a[1] * lr - (a[0] - 1.0) * li) / den)
    same = ((lax.broadcasted_iota(I32, (128, S5_SW), 0) >> int(math.log2(SSM_GROUP)))
            == (lax.broadcasted_iota(I32, (128, S5_SW), 1) >> int(math.log2(SSM_STATE))))
    bb = _cmul(q, (jnp.where(same, btr_ref[...], 0.0), jnp.where(same, bti_ref[...], 0.0)))
    c = (jnp.where(same, cr_ref[...], 0.0), jnp.where(same, ci_ref[...], 0.0))
    pw = [(jnp.ones_like(lr), jnp.zeros_like(lr))]
    for _ in range(L):
        pw.append(_cmul(pw[-1], a))
    ck = [_cmul(c, p) for p in pw]
    kbd = [(_dot_nt(bb[0], ck[k][0], hi) - _dot_nt(bb[1], ck[k][1], hi)).astype(BF16) for k in range(L)]
    mins = [tuple(x.astype(BF16) for x in _cmul(pw[L - 1 - s], bb)) for s in range(L)]
    cob = [tuple(x.astype(BF16) for x in ck[i + 1]) for i in range(L)]

    us = [u_ref[:, s * 128:(s + 1) * 128] for s in range(L)]
    ub = [x.astype(BF16) for x in us]
    sr = _dot(ub[0], mins[0][0])
    si = _dot(ub[0], mins[0][1])
    for s in range(1, L):
        sr = sr + _dot(ub[s], mins[s][0])
        si = si + _dot(ub[s], mins[s][1])
    al = pw[L]
    r = nsc * nb
    xl = [(sr[0:r], si[0:r])]
    for j in range(1, nj):
        t = _cmul(al, xl[-1])
        xl.append((t[0] + sr[j * r:(j + 1) * r], t[1] + si[j * r:(j + 1) * r]))
    alp = [al]
    for j in range(1, nj):
        alp.append(_cmul(alp[-1], al))
    er_ref[...] = xl[-1][0]
    ei_ref[...] = xl[-1][1]
    g = (h0r_ref[...], h0i_ref[...])
    for sc in range(nsc):
        gr_ref[sc * nb:(sc + 1) * nb, :] = g[0]
        gi_ref[sc * nb:(sc + 1) * nb, :] = g[1]
        t = _cmul(alp[nj - 1], g)
        g = (t[0] + er_ref[sc * nb:(sc + 1) * nb, :], t[1] + ei_ref[sc * nb:(sc + 1) * nb, :])
    hr_ref[...] = g[0]
    hi_ref[...] = g[1]
    gcat = (gr_ref[...], gi_ref[...])
    hprev = [gcat]
    for j in range(1, nj):
        t = _cmul(alp[j - 1], gcat)
        hprev.append((xl[j - 1][0] + t[0], xl[j - 1][1] + t[1]))
    hpr = jnp.concatenate([x[0] for x in hprev], axis=0).astype(BF16)
    hpi = jnp.concatenate([x[1] for x in hprev], axis=0).astype(BF16)
    y = (jnp.concatenate([d_ref[...]] * L, axis=1) * u_ref[...]
         + _dot_nt(hpr, jnp.concatenate([x[0] for x in cob], axis=0))
         - _dot_nt(hpi, jnp.concatenate([x[1] for x in cob], axis=0)))
    for s in range(L):
        conv = _dot(ub[s], jnp.concatenate(kbd[0:L - s], axis=1))
        y = y + conv if s == 0 else jnp.concatenate([y[:, :s * 128], y[:, s * 128:] + conv], axis=1)
    y_ref[...] = y


def _s5(u, lw, h0r, h0i, L, nj, nsc, nb):
    m, n = u.shape[1], u.shape[2]
    lanes = lambda x: x.reshape(S5_NLB, 1, S5_SW)
    rep = lambda x: jnp.tile(x.reshape(S5_NLB, 128, SSM_STATE), (1, 1, S5_GPB))
    vec = pl.BlockSpec((None, 1, S5_SW), lambda g: (g, 0, 0))
    mat = pl.BlockSpec((None, 128, S5_SW), lambda g: (g, 0, 0))
    st = pl.BlockSpec((None, nb, S5_SW), lambda g: (g, 0, 0))
    return pl.pallas_call(
        functools.partial(_s5_kernel, L=L, nj=nj, nsc=nsc, nb=nb),
        grid=(S5_NLB,),
        in_specs=[pl.BlockSpec((None, m, n), lambda g: (g, 0, 0)), vec, vec, vec, mat, mat, mat, mat,
                  pl.BlockSpec((None, 1, 128), lambda g: (g, 0, 0)), st, st],
        out_specs=[pl.BlockSpec((None, m, n), lambda g: (g, 0, 0)), st, st],
        out_shape=[jax.ShapeDtypeStruct((S5_NLB, m, n), F32)] + [jax.ShapeDtypeStruct((S5_NLB, nb, S5_SW), F32)] * 2,
        scratch_shapes=[pltpu.VMEM((nsc * nb, S5_SW), F32)] * 4,
        compiler_params=_params(("arbitrary",)),
        name="s5",
    )(u, lanes(lw["lam_re"]), lanes(lw["lam_im"]), lanes(jnp.repeat(lw["log_dt"], SSM_STATE)),
      rep(lw["ssm_b_re"].transpose(0, 2, 1)), rep(lw["ssm_b_im"].transpose(0, 2, 1)),
      rep(lw["ssm_c_re"]), rep(lw["ssm_c_im"]), lw["ssm_d"].reshape(S5_NLB, 1, 128), h0r, h0i)


def _s5_post_kernel(y_ref, w_ref, b_ref, g_ref, o_ref):
    zz = jax.nn.gelu(y_ref[...])
    out = zz * jax.nn.sigmoid(_dot(zz.astype(BF16), w_ref[...]) + b_ref[...])
    o_ref[...] = _rms(out, g_ref[...])


def _s5_post(y, w_glu, b_glu, g_mix_c):
    n = y.shape[0]
    tt = min(n, TM)
    vec = pl.BlockSpec((1, W_C), lambda i: (0, 0))
    return pl.pallas_call(
        _s5_post_kernel,
        grid=(n // tt,),
        in_specs=[pl.BlockSpec((tt, W_C), lambda i: (i, 0)), pl.BlockSpec((W_C, W_C), lambda i: (0, 0)), vec, vec],
        out_specs=pl.BlockSpec((tt, W_C), lambda i: (i, 0)),
        out_shape=jax.ShapeDtypeStruct((n, W_C), F32),
        compiler_params=_params(("arbitrary",)),
        name="s5_post",
    )(y, w_glu.astype(BF16), b_glu.reshape(1, W_C), g_mix_c.reshape(1, W_C))


def _mix_kernel(ap_ref, bp_ref, cp_ref, as_ref, bs_ref, cs_ref, g_ref, o_ref):
    def emit(a_ref, b_ref, c_ref):
        o_ref[:, 0:W_A] = _rms(a_ref[...], g_ref[...]).astype(BF16)
        o_ref[:, W_A:W_A + W_B] = b_ref[...].astype(BF16)
        o_ref[:, W_A + W_B:D] = c_ref[...].astype(BF16)

    last = pl.program_id(0) == NT - 1

    @pl.when(jnp.logical_not(last))
    def _():
        emit(ap_ref, bp_ref, cp_ref)

    @pl.when(last)
    def _():
        emit(as_ref, bs_ref, cs_ref)


def _mix(a_p, b_p, c_p, a_s, b_s, c_s, g_mix_a):
    pr = lambda w: pl.BlockSpec((TM, w), lambda i: (jnp.minimum(i, NT - 2), 0))
    sm = lambda w: pl.BlockSpec((TM, w), lambda i: (0, 0))
    return pl.pallas_call(
        _mix_kernel,
        grid=(NT,),
        in_specs=[pr(W_A), pr(W_B), pr(W_C), sm(W_A), sm(W_B), sm(W_C), pl.BlockSpec((1, W_A), lambda i: (0, 0))],
        out_specs=pl.BlockSpec((TM, D), lambda i: (i, 0)),
        out_shape=jax.ShapeDtypeStruct((NP, D), BF16),
        compiler_params=_params(("arbitrary",)),
        name="mix",
    )(a_p, b_p, c_p, a_s, b_s, c_s, g_mix_a.reshape(1, W_A))


def _router_kernel(h_ref, w_ref, b_ref, ind_ref, wd_ref, cnt_ref):
    i = pl.program_id(0)
    scores = jax.nn.sigmoid(_dot(h_ref[...], w_ref[...]))
    lane = lax.broadcasted_iota(I32, (TD, N_EXPERTS), 1).astype(F32)
    row = lax.broadcasted_iota(I32, (TD, N_EXPERTS), 0) + i * TD
    sel = scores + b_ref[...]
    ind = jnp.zeros((TD, N_EXPERTS), F32)
    for _ in range(TOP_K):
        m = jnp.max(sel, axis=-1, keepdims=True)
        idx = jnp.min(jnp.where(sel == m, lane, float(N_EXPERTS)), axis=-1, keepdims=True)
        hit = lane == idx
        ind = jnp.where(hit, 1.0, ind)
        sel = jnp.where(hit, -jnp.inf, sel)
    ind = jnp.where(row < N_REAL, ind, 0.0)
    wsel = ind * scores
    wd_ref[...] = wsel / jnp.sum(wsel + (1.0 - jnp.max(ind, axis=-1, keepdims=True)) / N_EXPERTS,
                                 axis=-1, keepdims=True) * ROUTE_SCALE
    ind_ref[...] = ind

    @pl.when(i == 0)
    def _():
        cnt_ref[...] = jnp.zeros((8, N_EXPERTS), F32)

    cnt_ref[...] += jnp.sum(ind, axis=0, keepdims=True)


def _router(h2b, w_router, b_router):
    til = pl.BlockSpec((TD, N_EXPERTS), lambda i: (i, 0))
    return pl.pallas_call(
        _router_kernel,
        grid=(NP // TD,),
        in_specs=[pl.BlockSpec((TD, D), lambda i: (i, 0)), pl.BlockSpec((D, N_EXPERTS), lambda i: (0, 0)),
                  pl.BlockSpec((1, N_EXPERTS), lambda i: (0, 0))],
        out_specs=[til, til, pl.BlockSpec((8, N_EXPERTS), lambda i: (0, 0))],
        out_shape=[jax.ShapeDtypeStruct((NP, N_EXPERTS), F32)] * 2 + [jax.ShapeDtypeStruct((8, N_EXPERTS), F32)],
        compiler_params=_params(("arbitrary",)),
        name="router",
    )(h2b, w_router.astype(BF16), b_router.reshape(1, N_EXPERTS))


def _group_starts(cnt):
    padded = jnp.floor((cnt + (BM - 1)) / BM) * BM
    tri = (lax.broadcasted_iota(I32, (N_EXPERTS, N_EXPERTS), 0)
           <= lax.broadcasted_iota(I32, (N_EXPERTS, N_EXPERTS), 1)).astype(F32)
    pend = jnp.dot(padded, tri, preferred_element_type=F32, precision=lax.Precision.HIGHEST)
    return padded, pend - padded, pend


def _plan_kernel(cnt_ref, be_ref, misc_ref):
    cnt = cnt_ref[...]
    _, pstart, pend = _group_starts(cnt)
    rowi = (lax.broadcasted_iota(I32, (NBP, N_EXPERTS), 0) * BM).astype(F32)
    be = jnp.minimum(jnp.sum(jnp.where(pend[0:1] <= rowi, 1.0, 0.0), axis=-1, keepdims=True), N_EXPERTS - 1.0)
    be_ref[...] = jnp.broadcast_to(be, (NBP, 128)).astype(I32)
    misc_ref[0:8, :] = pstart.astype(I32)
    misc_ref[8:16, :] = cnt.astype(I32)
    misc_ref[16:24, :] = (pend / BM).astype(I32)


def _plan(cnt):
    return pl.pallas_call(
        _plan_kernel,
        out_shape=[jax.ShapeDtypeStruct((NBP, 128), I32), jax.ShapeDtypeStruct((24, N_EXPERTS), I32)],
        name="moe_plan",
    )(cnt)


def _slots_kernel(ind_ref, wd_ref, cnt_ref, dest_ref, w_ref, run_ref):
    @pl.when(pl.program_id(0) == 0)
    def _():
        run_ref[...] = jnp.zeros((8, N_EXPERTS), F32)

    ind = ind_ref[...]
    _, pstart, _ = _group_starts(cnt_ref[...])
    ltri = (lax.broadcasted_iota(I32, (TD, TD), 1) < lax.broadcasted_iota(I32, (TD, TD), 0)).astype(BF16)
    pos = _dot(ltri, ind.astype(BF16)) + run_ref[0:1, :]
    run_ref[...] += jnp.sum(ind, axis=0, keepdims=True)
    cur = jnp.where(ind > 0.0, pstart[0:1] + pos, BIG)
    lane = lax.broadcasted_iota(I32, (TD, 128), 1)
    dest = jnp.full((TD, 128), -1.0, F32)
    wts = jnp.zeros((TD, 128), F32)
    wd = wd_ref[...]
    for k in range(TOP_K):
        m = jnp.min(cur, axis=-1, keepdims=True)
        hit = cur == m
        wk = jnp.sum(jnp.where(hit, wd, 0.0), axis=-1, keepdims=True)
        dest = jnp.where(lane == k, jnp.where(m < BIG, m, -1.0), dest)
        wts = jnp.where(lane == k, jnp.where(m < BIG, wk, 0.0), wts)
        cur = jnp.where(hit, BIG, cur)
    dest_ref[...] = dest.astype(I32)
    w_ref[...] = wts


def _slots(ind, wd, cnt):
    til = pl.BlockSpec((TD, N_EXPERTS), lambda i: (i, 0))
    out = pl.BlockSpec((TD, 128), lambda i: (i, 0))
    return pl.pallas_call(
        _slots_kernel,
        grid=(NP // TD,),
        in_specs=[til, til, pl.BlockSpec((8, N_EXPERTS), lambda i: (0, 0))],
        out_specs=[out, out],
        out_shape=[jax.ShapeDtypeStruct((NP, 128), I32), jax.ShapeDtypeStruct((NP, 128), F32)],
        scratch_shapes=[pltpu.VMEM((8, N_EXPERTS), F32)],
        compiler_params=_params(("arbitrary",)),
        name="moe_slots",
    )(ind, wd, cnt)


def _dispatch_kernel(ps_ref, cnt_ref, nbu_ref, dest_ref, x_ref, xs_ref, zero_ref, sem):
    i = pl.program_id(0)

    def token(ref, t):
        start = t * ROW_TILES if isinstance(t, int) else pl.multiple_of(t * ROW_TILES, ROW_TILES)
        return ref.at[pl.ds(start, ROW_TILES), :]

    def row_copy(r, k):
        return pltpu.make_async_copy(token(x_ref, r), token(xs_ref, dest_ref[r * 8 + k]), sem)

    def zero_row(d):
        return pltpu.make_async_copy(token(zero_ref, 0), token(xs_ref, d), sem)

    def zero_block(b):
        rows = BM * ROW_TILES
        return pltpu.make_async_copy(zero_ref, xs_ref.at[pl.ds(pl.multiple_of(b * rows, rows), rows), :], sem)

    @pl.when(i < N_DTILES)
    def _():
        n_rows = jnp.minimum(TD, N_REAL - i * TD)

        def issue(r, c):
            for k in range(TOP_K):
                row_copy(r, k).start(priority=k % 2)
            return c

        def drain(r, c):
            for k in range(TOP_K):
                row_copy(r, k).wait()
            return c

        lax.fori_loop(0, n_rows, issue, 0)
        lax.fori_loop(0, n_rows, drain, 0)

    @pl.when(i == N_DTILES)
    def _():
        zero_ref[...] = jnp.zeros(zero_ref.shape, F32)

        def per_expert(e, c):
            cnt = cnt_ref[e]
            base = ps_ref[e]
            end = ((cnt + (BM - 1)) >> BM_SHIFT) << BM_SHIFT
            lax.fori_loop(cnt, end, lambda j, c2: (zero_row(base + j).start(), c2)[1], 0)
            lax.fori_loop(cnt, end, lambda j, c2: (zero_row(base + j).wait(), c2)[1], 0)
            return c

        lax.fori_loop(0, N_EXPERTS, per_expert, 0)
        lax.fori_loop(nbu_ref[0], NB, lambda b, c: (zero_block(b).start(), c)[1], 0)
        lax.fori_loop(nbu_ref[0], NB, lambda b, c: (zero_block(b).wait(), c)[1], 0)


def _dispatch(pstart_i, cnt_i, nbu, dest_flat, x):
    return pl.pallas_call(
        _dispatch_kernel,
        grid_spec=pltpu.PrefetchScalarGridSpec(
            num_scalar_prefetch=3, grid=(N_DTILES + 1,),
            in_specs=[pl.BlockSpec((TD * 8,), lambda i, *_: (jnp.minimum(i, N_DTILES - 1),),
                                   memory_space=pltpu.SMEM),
                      pl.BlockSpec((TD * ROW_TILES, 128), lambda i, *_: (jnp.minimum(i, N_DTILES - 1), 0))],
            out_specs=pl.BlockSpec(memory_space=pl.ANY),
            scratch_shapes=[pltpu.VMEM((BM * ROW_TILES, 128), F32), pltpu.SemaphoreType.DMA(())]),
        out_shape=jax.ShapeDtypeStruct((CAP * ROW_TILES, 128), F32),
        compiler_params=_params(("arbitrary",)),
        name="moe_dispatch",
    )(pstart_i, cnt_i, nbu, dest_flat, x)


def _expert_kernel(be_ref, nbu_ref, gend_ref, x_ref, wg_hbm, wu_hbm, wd_hbm, o_ref,
                   wg_ref, wu_ref, wd_ref, slot_ref, sem, *, layer):
    i = pl.program_id(0)
    nbu = nbu_ref[0]

    def weight_copies(e, s):
        return [pltpu.make_async_copy(hbm.at[layer, e], buf.at[s], sem.at[s, j])
                for j, (hbm, buf) in enumerate(((wg_hbm, wg_ref), (wu_hbm, wu_ref), (wd_hbm, wd_ref)))]

    @pl.when(i < nbu)
    def _():
        e = be_ref[i]

        @pl.when(i == 0)
        def _():
            slot_ref[0] = 0
            for cp in weight_copies(e, 0):
                cp.start()

        new_group = jnp.logical_or(i == 0, be_ref[jnp.maximum(i - 1, 0)] != e)

        @pl.when(jnp.logical_and(new_group, i > 0))
        def _():
            slot_ref[0] = 1 - slot_ref[0]

        s = slot_ref[0]

        @pl.when(new_group)
        def _():
            for cp in weight_copies(e, s):
                cp.wait()
            nxt = gend_ref[e]

            @pl.when(nxt < nbu)
            def _():
                for cp in weight_copies(be_ref[jnp.minimum(nxt, NB - 1)], 1 - s):
                    cp.start()

        x = _load_token_rows(x_ref, BM).astype(BF16)
        g = _dot(x, wg_ref[s].astype(BF16))
        u = _dot(x, wu_ref[s].astype(BF16))
        a = (g * jax.nn.sigmoid(g) * u).astype(BF16)
        _store_token_rows(o_ref, _dot(a, wd_ref[s].astype(BF16)))

    @pl.when(i >= nbu)
    def _():
        o_ref[...] = jnp.zeros(o_ref.shape, F32)


def _experts(layer, be, nbu, group_end, xs, w_gate, w_up, w_down):
    any_spec = pl.BlockSpec(memory_space=pl.ANY)
    return pl.pallas_call(
        functools.partial(_expert_kernel, layer=layer),
        grid_spec=pltpu.PrefetchScalarGridSpec(
            num_scalar_prefetch=3, grid=(NB,),
            in_specs=[pl.BlockSpec((BM * ROW_TILES, 128), lambda i, be, nbu, ge: (jnp.minimum(i, nbu[0] - 1), 0)),
                      any_spec, any_spec, any_spec],
            out_specs=pl.BlockSpec((BM * ROW_TILES, 128), lambda i, be, nbu, ge: (i, 0)),
            scratch_shapes=[pltpu.VMEM((2, D, D_EXPERT), F32), pltpu.VMEM((2, D, D_EXPERT), F32),
                            pltpu.VMEM((2, D_EXPERT, D), F32), pltpu.SMEM((1,), I32),
                            pltpu.SemaphoreType.DMA((2, 3))]),
        out_shape=jax.ShapeDtypeStruct((CAP * ROW_TILES, 128), F32),
        compiler_params=_params(("arbitrary",)),
        name="moe_experts",
    )(be, nbu, group_end, xs, w_gate, w_up, w_down)


def _shared_kernel(h_ref, wg_ref, wu_ref, wd_ref, o_ref):
    h = h_ref[...]
    g = _dot(h, wg_ref[...])
    u = _dot(h, wu_ref[...])
    o_ref[...] = _dot((g * jax.nn.sigmoid(g) * u).astype(BF16), wd_ref[...])


def _shared(h2b, wg, wu, wd):
    return pl.pallas_call(
        _shared_kernel,
        grid=(NT,),
        in_specs=[pl.BlockSpec((TM, D), lambda i: (i, 0)),
                  pl.BlockSpec((D, D_SHARED), lambda i: (0, 0)),
                  pl.BlockSpec((D, D_SHARED), lambda i: (0, 0)),
                  pl.BlockSpec((D_SHARED, D), lambda i: (0, 0))],
        out_specs=pl.BlockSpec((TM, D), lambda i: (i, 0)),
        out_shape=jax.ShapeDtypeStruct((NP, D), F32),
        compiler_params=_params(("arbitrary",)),
        name="moe_shared",
    )(h2b, wg.astype(BF16), wu.astype(BF16), wd.astype(BF16))


def _combine_kernel(dest_ref, nxt_ref, w_ref, ys_ref, sh_ref, x_ref, gr_ref, gt_ref, o_ref, rows_ref, acc_ref, sem):
    i = pl.program_id(0)
    n_tiles = NP // TC_
    slot = i % 2

    def gather(d_ref, s, start):
        def body(r, c):
            for k in range(TOP_K):
                d = jnp.maximum(d_ref[r * 8 + k], 0)
                cp = pltpu.make_async_copy(
                    ys_ref.at[pl.ds(pl.multiple_of(d * ROW_TILES, ROW_TILES), ROW_TILES), :],
                    rows_ref.at[s, k, pl.ds(pl.multiple_of(r * ROW_TILES, ROW_TILES), ROW_TILES), :], sem.at[s])
                if start:
                    cp.start(priority=k % 2)
                else:
                    cp.wait()
            return c

        lax.fori_loop(0, TC_, body, 0)

    @pl.when(i == 0)
    def _():
        gather(dest_ref, 0, True)

    @pl.when(i + 1 < n_tiles)
    def _():
        gather(nxt_ref, 1 - slot, True)

    gather(dest_ref, slot, False)
    def weigh(t, c):
        rows = pl.ds(pl.multiple_of(t * ROW_TILES, ROW_TILES), ROW_TILES)
        acc = w_ref[t * 8] * rows_ref[slot, 0, rows, :]
        for k in range(1, TOP_K):
            acc = acc + w_ref[t * 8 + k] * rows_ref[slot, k, rows, :]
        acc_ref[rows, :] = acc
        return c

    lax.fori_loop(0, TC_, weigh, 0, unroll=4)
    routed = _load_token_rows(acc_ref, TC_)
    last = i >= N_PROMPT // TC_
    o_ref[...] = x_ref[...] + _pick(last, gt_ref, gr_ref) * (routed + sh_ref[...])


def _combine(dest_flat, w6, ys, shared, x, gate):
    n_tiles = NP // TC_
    n_p = N_PROMPT // TC_
    per = TM // TC_
    til = pl.BlockSpec((TC_, D), lambda i: (i, 0))
    return pl.pallas_call(
        _combine_kernel,
        grid=(n_tiles,),
        in_specs=[pl.BlockSpec((TC_ * 8,), lambda i: (i,), memory_space=pltpu.SMEM),
                  pl.BlockSpec((TC_ * 8,), lambda i: (jnp.minimum(i + 1, n_tiles - 1),), memory_space=pltpu.SMEM),
                  pl.BlockSpec((TC_ * 8,), lambda i: (i,), memory_space=pltpu.SMEM),
                  pl.BlockSpec(memory_space=pl.ANY), til, til,
                  pl.BlockSpec((None, 1, D), lambda i: (i // per, 0, 0)),
                  pl.BlockSpec((TC_, D), lambda i: (jnp.maximum(i - n_p, 0), 0))],
        out_specs=til,
        out_shape=jax.ShapeDtypeStruct((NP, D), F32),
        scratch_shapes=[pltpu.VMEM((2, TOP_K, TC_ * ROW_TILES, 128), F32), pltpu.VMEM((TC_ * ROW_TILES, 128), F32),
                        pltpu.SemaphoreType.DMA((2,))],
        compiler_params=_params(("arbitrary",)),
        name="moe_combine",
    )(dest_flat, dest_flat, w6, ys, shared, x, gate[0], gate[1])


def _pad_rows(a, rows):
    return jnp.concatenate([a, jnp.zeros((rows - a.shape[0],) + a.shape[1:], a.dtype)], axis=0)


def _s5_layer(z, lw, h0r_s, h0i_s):
    g_mix_c = lw["g_mix"][W_A + W_B:]
    nj = nsc = 16
    state = lambda h, b: h.reshape(S5_NLB, b, S5_GPB, SSM_STATE).transpose(1, 0, 2, 3).reshape(b, G_C, SSM_STATE)
    uc_p = z[:N_PROMPT, IN_COLS - W_C:]
    u_p = uc_p.reshape(B_P, nsc, nj, S5_L_P, S5_NLB, 128).transpose(4, 2, 1, 0, 3, 5)
    u_p = u_p.reshape(S5_NLB, nj * nsc * B_P, S5_L_P * 128)
    zeros_h = jnp.zeros((S5_NLB, B_P, S5_SW), F32)
    y_p, hr_p, hi_p = _s5(u_p, lw, zeros_h, zeros_h, S5_L_P, nj, nsc, B_P)
    y_p = y_p.reshape(S5_NLB, nj, nsc, B_P, S5_L_P, 128).transpose(3, 2, 1, 4, 0, 5).reshape(N_PROMPT, W_C)
    c_p = _s5_post(y_p, lw["w_glu"], lw["b_glu"], g_mix_c)

    uc_s = z[N_PROMPT:N_REAL, IN_COLS - W_C:]
    u_s = uc_s.reshape(B_S, T_S, S5_NLB, 128).transpose(2, 0, 1, 3).reshape(S5_NLB, B_S, S5_L_S * 128)
    h0 = lambda h: h.reshape(B_S, S5_NLB, S5_SW).transpose(1, 0, 2)
    y_s, hr_s, hi_s = _s5(u_s, lw, h0(h0r_s), h0(h0i_s), S5_L_S, 1, 1, B_S)
    y_s = y_s.reshape(S5_NLB, B_S, T_S, 128).transpose(1, 2, 0, 3).reshape(N_SAMPLE, W_C)
    c_s = _s5_post(y_s, lw["w_glu"], lw["b_glu"], g_mix_c)
    return c_p, c_s, state(hr_p, B_P), state(hi_p, B_P), state(hr_s, B_S), state(hi_s, B_S)


def _layer(x, l, lw, experts, mod, bias_p, bias_s, cache_k, cache_v, h0r_s, h0i_s):
    sh1, sc1, g1, sh2, sc2, g2 = [_tile_tabs(m) for m in jnp.split(mod, 6, axis=-1)]
    (h,) = _modnorm(x, lw["g_attn"], sc1, sh1, False)
    z = _matmul(h, lw["w_in"].astype(BF16), 1536)

    a_p = _attn_prompt(z, bias_p)
    a_s = _attn_sample(z, cache_k, cache_v, l, bias_s)

    g_mix_b = lw["g_mix"][W_A:W_A + W_B]
    gm = (lw["g_bv"], lw["b_bv"], lw["w_s"], lw["b_s"], g_mix_b)
    b_p, _ = _gmlp(z, N_PROMPT // CHUNK, 3 * W_A // W_B, 3 * W_A // W_B + 1, *gm)
    uv_s = z[N_PROMPT:N_REAL, 3 * W_A:3 * W_A + 2 * W_B].reshape(B_S, T_S, 2 * W_B)
    uv_s = jnp.pad(uv_s, ((0, 0), (0, CHUNK - T_S), (0, 0))).reshape(B_S * CHUNK, 2 * W_B)
    b_s, vn_s = _gmlp(uv_s, B_S, 0, 1, *gm)
    b_s = b_s.reshape(B_S, CHUNK, W_B)[:, :T_S].reshape(N_SAMPLE, W_B)
    new_vb = vn_s.reshape(B_S, CHUNK, W_B)[:, :T_S]

    c_p, c_s, hr_p, hi_p, hr_s, hi_s = _s5_layer(z, lw, h0r_s, h0i_s)

    mixed = _mix(a_p, b_p, c_p, _pad_rows(a_s, TM), _pad_rows(b_s, TM), _pad_rows(c_s, TM), lw["g_mix"][:W_A])
    x = _matmul_residual(mixed, lw["w_out"].astype(BF16), x, g1, 1024)

    h2b, h2 = _modnorm(x, lw["g_ffn"], sc2, sh2, True)
    ind, wd, cnt = _router(h2b, lw["w_router"], lw["b_router"])
    be, misc = _plan(cnt)
    dest, w6 = _slots(ind, wd, cnt)
    dest_flat = dest[:, :8].reshape(-1)
    xs = _dispatch(misc[0], misc[8], misc[16, N_EXPERTS - 1:], dest_flat, h2)
    ys = _experts(l, be[:NB, 0], misc[16, N_EXPERTS - 1:], misc[16], xs, *experts)
    shared = _shared(h2b, lw["w_sh_gate"], lw["w_sh_up"], lw["w_sh_down"])
    x = _combine(dest_flat, w6[:, :8].reshape(-1), ys, shared, x, g2)

    kv = lambda lo: (z[:N_PROMPT, lo:lo + W_A].reshape(B_P, T_P, H_A, HEAD_DIM)[:, T_P - BUF:],
                     z[N_PROMPT:N_REAL, lo:lo + W_A].reshape(B_S, T_S, H_A, HEAD_DIM))
    (k_p, k_s), (v_p, v_s) = kv(W_A), kv(2 * W_A)
    return x, (k_p, v_p, k_s, v_s, new_vb, hr_p, hi_p, hr_s, hi_s)


def kernel(x_prompt, x_sample, cache_a_k, cache_a_v, state_c_re, state_c_im, c_prompt, c_sample, rel_bias, w_ada, b_ada, g_attn, w_in, g_bv, b_bv, w_s, b_s, lam_re, lam_im, log_dt, ssm_b_re, ssm_b_im, ssm_c_re, ssm_c_im, ssm_d, w_glu, b_glu, g_mix, w_out, g_ffn, w_router, b_router, w_gate, w_up, w_down, w_sh_gate, w_sh_up, w_sh_down, g_final):
    assert x_prompt.shape == (B_P, T_P, D) and x_sample.shape == (B_S, T_S, D)
    assert cache_a_k.shape == (DEPTH, B_S, BUF, H_A, HEAD_DIM)
    per_layer = dict(g_attn=g_attn, w_in=w_in, g_bv=g_bv, b_bv=b_bv, w_s=w_s, b_s=b_s, lam_re=lam_re,
                     lam_im=lam_im, log_dt=log_dt, ssm_b_re=ssm_b_re, ssm_b_im=ssm_b_im, ssm_c_re=ssm_c_re,
                     ssm_c_im=ssm_c_im, ssm_d=ssm_d, w_glu=w_glu, b_glu=b_glu, g_mix=g_mix, w_out=w_out,
                     g_ffn=g_ffn, w_router=w_router, b_router=b_router, w_sh_gate=w_sh_gate, w_sh_up=w_sh_up,
                     w_sh_down=w_sh_down)

    x = _pad_rows(jnp.concatenate([x_prompt.reshape(N_PROMPT, D), x_sample.reshape(N_SAMPLE, D)], axis=0), NP)
    c_all = _pad_rows(jnp.concatenate([c_prompt, c_sample], axis=0), 16)
    mod = _mod_all(c_all, w_ada, b_ada)
    bias_p = _bias_bank(rel_bias, *_prompt_bias_tables())
    bias_s = _bias_bank(rel_bias, *_sample_bias_tables())[0]
    cache_k = cache_a_k.reshape(DEPTH * B_S, BUF, W_A)
    cache_v = cache_a_v.reshape(DEPTH * B_S, BUF, W_A)

    outs = []
    for l in range(DEPTH):
        lw = {name: w[l] for name, w in per_layer.items()}
        x, o = _layer(x, l, lw, (w_gate, w_up, w_down), mod[l], bias_p, bias_s, cache_k, cache_v,
                      state_c_re[l], state_c_im[l])
        outs.append(o)
    y = _finalnorm(x, g_final)
    stacked = [jnp.stack([o[j] for o in outs]) for j in range(9)]
    return (y[:N_PROMPT].reshape(B_P, T_P, D), y[N_PROMPT:N_REAL].reshape(B_S, T_S, D), *stacked)
```

```python
import functools
import math

import jax
import jax.numpy as jnp
import numpy as np
from jax import lax
from jax.experimental import pallas as pl
from jax.experimental.pallas import tpu as pltpu

F32 = jnp.float32
BF16 = jnp.bfloat16
I32 = jnp.int32

D = 2048
B_P, T_P = 2, 4096
B_S, T_S = 8, 8
DEPTH = 2
HEAD_DIM = 64
W_A = 1024
H_A = 16
W_B = 512
G_B = 8
W_C = 512
SSM_GROUP = 16
G_C = 32
SSM_STATE = 64
IN_COLS = 3 * W_A + 2 * W_B + W_C
DILATIONS = (1, 4, 16)
NK = 128
ATTN_UNITS = (8, 8, 8)
BUF = 2048
CHUNK = 128
N_BUCKETS = 32
REL_MAX_DIST = 2048
N_EXPERTS = 64
TOP_K = 6
D_EXPERT = 512
D_SHARED = 1024
ROUTE_SCALE = 2.5
EPS = 1e-6

N_PROMPT = B_P * T_P
N_SAMPLE = B_S * T_S
N_REAL = N_PROMPT + N_SAMPLE
TM = 512
NT = N_PROMPT // TM + 1
NP = NT * TM
TD = 256
N_DTILES = -(-N_REAL // TD)
ROW_TILES = D // 128
BM = 256
BM_SHIFT = 8
NB = -(-(N_REAL * TOP_K + N_EXPERTS * (BM - 1)) // BM)
NBP = -(-NB // 8) * 8
CAP = NB * BM
TC_ = 128
KF_ROWS = 2176
SAMPLE_CHUNK = 512
SAMPLE_CHUNKS = BUF // SAMPLE_CHUNK
S5_L_P = 16
S5_L_S = T_S
S5_GPB = 128 // SSM_GROUP
S5_NLB = G_C // S5_GPB
S5_SW = S5_GPB * SSM_STATE
BIG = 1e9
assert BM == 1 << BM_SHIFT

VMEM_LIMIT_BYTES = 56 * 1024 * 1024


def _params(dims):
    return pltpu.CompilerParams(dimension_semantics=dims, vmem_limit_bytes=VMEM_LIMIT_BYTES)


def _dot(a, b):
    return jnp.dot(a, b, preferred_element_type=F32)


def _dot_nt(a, b, precision=None):
    return lax.dot_general(a, b, (((1,), (1,)), ((), ())), preferred_element_type=F32,
                           precision=precision)


def _rms(x, g):
    return x * lax.rsqrt(jnp.mean(x * x, axis=-1, keepdims=True) + EPS) * g


def _mod_kernel(c_ref, w_ref, b_ref, o_ref):
    c = c_ref[...]
    s = (c * jax.nn.sigmoid(c)).astype(BF16)
    o_ref[...] = _dot(s, w_ref[...].astype(BF16)) + b_ref[...]


def _mod_all(c_all, w_ada, b_ada):
    tn = 1024
    return pl.pallas_call(
        _mod_kernel,
        grid=(DEPTH, 6 * D // tn),
        in_specs=[pl.BlockSpec((16, D), lambda l, j: (0, 0)),
                  pl.BlockSpec((None, D, tn), lambda l, j: (l, 0, j)),
                  pl.BlockSpec((None, 1, tn), lambda l, j: (l, 0, j))],
        out_specs=pl.BlockSpec((None, 16, tn), lambda l, j: (l, 0, j)),
        out_shape=jax.ShapeDtypeStruct((DEPTH, 16, 6 * D), F32),
        compiler_params=_params(("arbitrary", "arbitrary")),
        name="mod",
    )(c_all, w_ada, b_ada.reshape(DEPTH, 1, 6 * D))


def _tile_tabs(m):
    row = jnp.concatenate([jnp.repeat(m[0:B_P], NT // B_P, axis=0), jnp.zeros((1, m.shape[1]), F32)], 0)
    tok = jnp.concatenate([jnp.repeat(m[B_P:B_P + B_S], T_S, axis=0),
                           jnp.zeros((TM - N_SAMPLE, m.shape[1]), F32)], 0)
    return row[:, None, :], tok


def _pick(last, tok_ref, row_ref):
    return jnp.where(last, tok_ref[...], row_ref[...])


def _modnorm_kernel(x_ref, g_ref, scr_ref, sct_ref, shr_ref, sht_ref, *o_refs):
    last = pl.program_id(0) == NT - 1
    y = _rms(x_ref[...], g_ref[...])
    y = y * (1.0 + _pick(last, sct_ref, scr_ref)) + _pick(last, sht_ref, shr_ref)
    o_refs[0][...] = y.astype(BF16)
    if len(o_refs) > 1:
        _store_token_rows(o_refs[1], y)


def _load_token_rows(ref, n, first=0):
    return jnp.concatenate([ref[pl.ds(first * ROW_TILES + j, n, stride=ROW_TILES), :] for j in range(ROW_TILES)],
                           axis=1)


def _store_token_rows(ref, val, first=0):
    for j in range(ROW_TILES):
        ref[pl.ds(first * ROW_TILES + j, val.shape[0], stride=ROW_TILES), :] = val[:, j * 128:(j + 1) * 128]


def _modnorm(x, g, sc, sh, with_token_rows):
    row = pl.BlockSpec((None, 1, D), lambda i: (i, 0, 0))
    tok = pl.BlockSpec((TM, D), lambda i: (0, 0))
    til = pl.BlockSpec((TM, D), lambda i: (i, 0))
    out_specs, out_shape = [til], [jax.ShapeDtypeStruct((NP, D), BF16)]
    if with_token_rows:
        out_specs.append(pl.BlockSpec((TM * ROW_TILES, 128), lambda i: (i, 0)))
        out_shape.append(jax.ShapeDtypeStruct((NP * ROW_TILES, 128), F32))
    return pl.pallas_call(
        _modnorm_kernel,
        grid=(NT,),
        in_specs=[til, pl.BlockSpec((1, D), lambda i: (0, 0)), row, tok, row, tok],
        out_specs=out_specs,
        out_shape=out_shape,
        compiler_params=_params(("arbitrary",)),
        name="modnorm",
    )(x, g.reshape(1, D), sc[0], sc[1], sh[0], sh[1])


def _finalnorm_kernel(x_ref, g_ref, o_ref):
    o_ref[...] = _rms(x_ref[...], g_ref[...])


def _finalnorm(x, g):
    til = pl.BlockSpec((TM, D), lambda i: (i, 0))
    return pl.pallas_call(
        _finalnorm_kernel,
        grid=(NT,),
        in_specs=[til, pl.BlockSpec((1, D), lambda i: (0, 0))],
        out_specs=til,
        out_shape=jax.ShapeDtypeStruct((NP, D), F32),
        compiler_params=_params(("arbitrary",)),
        name="finalnorm",
    )(x, g.reshape(1, D))


def _mm_kernel(a_ref, b_ref, o_ref):
    o_ref[...] = _dot(a_ref[...], b_ref[...])


def _matmul(a, b, tn):
    k, n = b.shape
    return pl.pallas_call(
        _mm_kernel,
        grid=(n // tn, NT),
        in_specs=[pl.BlockSpec((TM, k), lambda c, i: (i, 0)),
                  pl.BlockSpec((k, tn), lambda c, i: (0, c))],
        out_specs=pl.BlockSpec((TM, tn), lambda c, i: (i, c)),
        out_shape=jax.ShapeDtypeStruct((NP, n), F32),
        compiler_params=_params(("arbitrary", "arbitrary")),
        name="matmul",
    )(a, b)


def _mm_res_kernel(a_ref, b_ref, x_ref, gr_ref, gt_ref, o_ref):
    last = pl.program_id(1) == NT - 1
    o_ref[...] = x_ref[...] + _pick(last, gt_ref, gr_ref) * _dot(a_ref[...], b_ref[...])


def _matmul_residual(a, b, x, gate, tn):
    k, n = b.shape
    return pl.pallas_call(
        _mm_res_kernel,
        grid=(n // tn, NT),
        in_specs=[pl.BlockSpec((TM, k), lambda c, i: (i, 0)),
                  pl.BlockSpec((k, tn), lambda c, i: (0, c)),
                  pl.BlockSpec((TM, tn), lambda c, i: (i, c)),
                  pl.BlockSpec((None, 1, tn), lambda c, i: (i, 0, c)),
                  pl.BlockSpec((TM, tn), lambda c, i: (0, c))],
        out_specs=pl.BlockSpec((TM, tn), lambda c, i: (i, c)),
        out_shape=jax.ShapeDtypeStruct((NP, n), F32),
        compiler_params=_params(("arbitrary", "arbitrary")),
        name="matmul_residual",
    )(a, b, x, gate[0], gate[1])


def _t5_bucket(dist):
    max_exact = N_BUCKETS // 2
    dist = np.maximum(dist, 0)
    ratio = (np.log(np.maximum(dist, 1).astype(np.float32) / np.float32(max_exact))
             / np.float32(math.log(REL_MAX_DIST / max_exact)))
    large = np.minimum(max_exact + (ratio * np.float32(N_BUCKETS - max_exact)).astype(np.int32), N_BUCKETS - 1)
    return np.where(dist < max_exact, dist, large)


def _bias_kernel(rb_ref, bucket_ref, add_ref, o_ref):
    h = pl.program_id(1)
    bucket = bucket_ref[...]
    acc = jnp.full(bucket.shape, -jnp.inf, F32)
    for b in range(N_BUCKETS):
        acc = jnp.where(bucket == b, rb_ref[b * H_A + h], acc)
    o_ref[...] = acc + add_ref[...]


def _bias_bank(rel_bias, bucket, add):
    n, r, c = bucket.shape
    blk = pl.BlockSpec((None, r, c), lambda i, h, rb: (i, 0, 0))
    return pl.pallas_call(
        _bias_kernel,
        grid_spec=pltpu.PrefetchScalarGridSpec(
            num_scalar_prefetch=1, grid=(n, H_A), in_specs=[blk, blk],
            out_specs=pl.BlockSpec((None, None, r, c), lambda i, h, rb: (i, h, 0, 0))),
        out_shape=jax.ShapeDtypeStruct((n, H_A, r, c), F32),
        compiler_params=_params(("arbitrary", "arbitrary")),
        name="bias_bank",
    )(rel_bias.reshape(-1), bucket, add)


def _prompt_bias_tables():
    qi = np.arange(NK)[:, None]
    kj = np.arange(2 * NK)[None, :]
    rel = qi + NK - kj
    band = (rel >= 0) & (rel <= NK)
    bucket = np.stack([np.where(band, _t5_bucket(rel * d), -1) for d in DILATIONS]).astype(np.int32)
    return jnp.asarray(bucket), jnp.zeros(bucket.shape, F32)


def _sample_bias_tables():
    dist_np = BUF + np.arange(T_S)[:, None] - np.arange(KF_ROWS)[None, :]
    mult = np.zeros(dist_np.shape, np.float64)
    for d in DILATIONS:
        mult += (dist_np >= 0) & (dist_np % d == 0) & (dist_np // d <= NK)
    mult = np.where(np.arange(KF_ROWS)[None, :] < BUF + T_S, mult, 0.0)
    valid = mult > 0
    add = np.where(valid, np.log(np.maximum(mult, 1.0)), 0.0).astype(np.float32)
    bucket = np.where(valid, _t5_bucket(dist_np), -1).astype(np.int32)
    return jnp.asarray(bucket)[None], jnp.asarray(add)[None]


def _attn_prompt_kernel(q_ref, k_ref, v_ref, bias_ref, o_ref, *stats):
    lane = lax.broadcasted_iota(I32, (NK, 128), 1)
    head0 = lane < HEAD_DIM
    prev_cols = lax.broadcasted_iota(I32, (NK, 2 * NK), 1) < NK
    ones = jnp.ones((NK, 128), F32)
    n_br = len(DILATIONS)
    m_refs, l_refs, a_refs = stats[0:n_br], stats[n_br:2 * n_br], stats[2 * n_br:3 * n_br]
    for di in range(n_br):
        d = DILATIONS[di]
        n_blocks = T_P // (d * NK)
        n_units = ATTN_UNITS[di]
        run = min(n_units, n_blocks)
        assert n_units % run == 0 and n_blocks % run == 0

        def group(gi, carry, di=di, d=d, n_blocks=n_blocks, n_units=n_units, run=run):
            def idx(start):
                return pl.ds(start, NK) if d == 1 else pl.ds(start, NK, stride=d)

            def keys(start):
                return k_ref[idx(start), :].astype(BF16)

            def values(start):
                return jnp.concatenate([v_ref[idx(start), :], ones], axis=1).astype(BF16)

            units = []
            for j in range(n_units):
                u = gi * n_units + j
                r = u >> int(math.log2(n_blocks))
                n = u & (n_blocks - 1)
                cur = r + d * NK * n
                if j % run == 0:
                    prv = r + d * NK * jnp.maximum(n - 1, 0)
                    k_prev, v_prev = keys(prv), values(prv)
                    pen = jnp.where(n == 0, -jnp.inf, 0.0).astype(F32)
                else:
                    k_prev, v_prev, pen = k_cur, v_cur, None
                k_cur, v_cur = keys(cur), values(cur)
                units.append(dict(cur=cur, pen=pen, q=q_ref[idx(cur), :] * (HEAD_DIM ** -0.5),
                                  k2=jnp.concatenate([k_prev, k_cur], axis=0),
                                  v2=jnp.concatenate([v_prev, v_cur], axis=0)))
            for unit in units:
                pv, rs, mn = [], [], []
                for h in range(2):
                    qh = jnp.where(head0 if h == 0 else ~head0, unit["q"], 0.0).astype(BF16)
                    s = _dot_nt(qh, unit["k2"]) + bias_ref[di, h]
                    if unit["pen"] is not None:
                        s = s + jnp.where(prev_cols, unit["pen"], 0.0)
                    m_h = jnp.max(s, axis=-1, keepdims=True)
                    pv_rs = _dot(jnp.exp(s - m_h).astype(BF16), unit["v2"])
                    pv.append(pv_rs[:, 0:128])
                    rs.append(pv_rs[:, 128:256])
                    mn.append(m_h)
                m_refs[di][idx(unit["cur"]), :] = jnp.where(head0, mn[0], mn[1])
                l_refs[di][idx(unit["cur"]), :] = jnp.where(head0, rs[0], rs[1])
                a_refs[di][idx(unit["cur"]), :] = jnp.where(head0, pv[0], pv[1])
            return carry

        lax.fori_loop(0, T_P // NK // n_units, group, 0)

    def merge(bi, carry):
        rows = pl.ds(pl.multiple_of(bi * NK, NK), NK)
        ms = [m_ref[rows, :] for m_ref in m_refs]
        m = functools.reduce(jnp.maximum, ms)
        num = jnp.zeros((NK, 128), F32)
        den = jnp.zeros((NK, 128), F32)
        for m_i, l_ref, a_ref in zip(ms, l_refs, a_refs):
            w = jnp.exp(m_i - m)
            num = num + w * a_ref[rows, :]
            den = den + w * l_ref[rows, :]
        o_ref[rows, :] = num / den
        return carry

    lax.fori_loop(0, T_P // NK, merge, 0, unroll=2)


def _attn_prompt(z, bias):
    blk = lambda off: pl.BlockSpec((T_P, 128), lambda b, h, off=off: (b, off + h))
    return pl.pallas_call(
        _attn_prompt_kernel,
        grid=(B_P, H_A // 2),
        in_specs=[blk(0), blk(W_A // 128), blk(2 * W_A // 128),
                  pl.BlockSpec((len(DILATIONS), 2, NK, 2 * NK), lambda b, h: (0, h, 0, 0))],
        out_specs=pl.BlockSpec((T_P, 128), lambda b, h: (b, h)),
        out_shape=jax.ShapeDtypeStruct((N_PROMPT, W_A), F32),
        scratch_shapes=[pltpu.VMEM((T_P, 128), F32)] * (3 * len(DILATIONS)),
        compiler_params=_params(("arbitrary", "arbitrary")),
        name="attn_prompt",
    )(z, z, z, bias)


def _attn_sample_kernel(q_ref, kn_ref, vn_ref, ck_ref, cv_ref, bias_ref, biasn_ref, o_ref, m_ref, l_ref, a_ref):
    c = pl.program_id(1)
    q = (q_ref[...] * (HEAD_DIM ** -0.5)).astype(BF16)

    def accumulate(k2, v2, bias, first):
        s = _dot_nt(q, k2.astype(BF16)) + bias
        m_new = jnp.max(s, axis=-1, keepdims=True)
        if not first:
            m_new = jnp.maximum(m_ref[...], m_new)
        p = jnp.exp(s - m_new)
        l_new = jnp.sum(p, axis=-1, keepdims=True)
        a_new = _dot(p.astype(BF16), v2.astype(BF16))
        if not first:
            alpha = jnp.exp(m_ref[...] - m_new)
            l_new = alpha * l_ref[...] + l_new
            a_new = alpha * a_ref[...] + a_new
        m_ref[...] = m_new
        l_ref[...] = l_new
        a_ref[...] = a_new

    @pl.when(c == 0)
    def _():
        accumulate(ck_ref[...], cv_ref[...], bias_ref[...], True)

    @pl.when(c > 0)
    def _():
        accumulate(ck_ref[...], cv_ref[...], bias_ref[...], False)

    @pl.when(c == SAMPLE_CHUNKS - 1)
    def _():
        accumulate(kn_ref[...], vn_ref[...], biasn_ref[...], False)
        o_ref[...] = a_ref[...] / l_ref[...]


def _sample_bias_by_head(bias_s):
    same = jnp.eye(H_A, dtype=bool)[:, None, None, :]
    full = jnp.where(same, bias_s[:, :, :BUF + T_S, None], -jnp.inf).reshape(H_A * T_S, (BUF + T_S) * H_A)
    return full[:, :BUF * H_A], full[:, BUF * H_A:]


def _attn_sample(z, cache_k, cache_v, layer, bias_c, bias_n):
    zs = z[N_PROMPT:N_REAL]
    q2 = zs[:, :W_A].reshape(B_S, T_S, H_A, HEAD_DIM).transpose(0, 2, 1, 3).reshape(B_S, H_A * T_S, HEAD_DIM)
    kn = zs[:, W_A:2 * W_A].reshape(B_S, T_S * H_A, HEAD_DIM)
    vn = zs[:, 2 * W_A:3 * W_A].reshape(B_S, T_S * H_A, HEAD_DIM)
    rows = SAMPLE_CHUNK * H_A
    small = pl.BlockSpec((None, H_A * T_S, HEAD_DIM), lambda b, c: (b, 0, 0))
    cache = pl.BlockSpec((None, rows, HEAD_DIM), lambda b, c: (layer * B_S + b, c, 0))
    out = pl.pallas_call(
        _attn_sample_kernel,
        grid=(B_S, SAMPLE_CHUNKS),
        in_specs=[small, small, small, cache, cache,
                  pl.BlockSpec((H_A * T_S, rows), lambda b, c: (0, c)),
                  pl.BlockSpec((H_A * T_S, H_A * T_S), lambda b, c: (0, 0))],
        out_specs=small,
        out_shape=jax.ShapeDtypeStruct((B_S, H_A * T_S, HEAD_DIM), F32),
        scratch_shapes=[pltpu.VMEM((H_A * T_S, 1), F32), pltpu.VMEM((H_A * T_S, 1), F32),
                        pltpu.VMEM((H_A * T_S, HEAD_DIM), F32)],
        compiler_params=_params(("arbitrary", "arbitrary")),
        name="attn_sample",
    )(q2, kn, vn, cache_k, cache_v, bias_c, bias_n)
    return out.reshape(B_S, H_A, T_S, HEAD_DIM).transpose(0, 2, 1, 3).reshape(N_SAMPLE, W_A)


def _gmlp_kernel(ub_ref, vb_ref, gbv_ref, bbv_ref, ws_ref, bs_ref, gmix_ref, bo_ref, vn_ref):
    v = jax.nn.gelu(vb_ref[...])
    mu = jnp.mean(v, axis=-1, keepdims=True)
    var = jnp.mean(jnp.square(v - mu), axis=-1, keepdims=True)
    vn = (v - mu) * lax.rsqrt(var + EPS) * gbv_ref[...] + bbv_ref[...]
    vn_ref[...] = vn
    vnb = vn.astype(BF16)
    causal = (lax.broadcasted_iota(I32, (CHUNK, CHUNK), 1) <= lax.broadcasted_iota(I32, (CHUNK, CHUNK), 0))
    first = lax.broadcasted_iota(I32, (CHUNK, 128), 1) < W_B // G_B
    parts = []
    for gp in range(G_B // 2):
        v2 = vnb[:, gp * 128:(gp + 1) * 128]
        m0 = _dot(jnp.where(causal, ws_ref[2 * gp], 0.0).astype(BF16), v2)
        m1 = _dot(jnp.where(causal, ws_ref[2 * gp + 1], 0.0).astype(BF16), v2)
        parts.append(jnp.where(first, m0, m1))
    mix = jnp.concatenate(parts, axis=1) + bs_ref[...]
    bo_ref[...] = _rms(jax.nn.gelu(ub_ref[...]) * mix, gmix_ref[...])


def _gmlp(z, n_chunks, col_u, col_v, g_bv, b_bv, w_s, b_s, g_mix_b):
    bs_wide = jnp.repeat(b_s.T, W_B // G_B, axis=1)
    vec = pl.BlockSpec((1, W_B), lambda i: (0, 0))
    out = pl.BlockSpec((CHUNK, W_B), lambda i: (i, 0))
    return pl.pallas_call(
        _gmlp_kernel,
        grid=(n_chunks,),
        in_specs=[pl.BlockSpec((CHUNK, W_B), lambda i: (i, col_u)),
                  pl.BlockSpec((CHUNK, W_B), lambda i: (i, col_v)),
                  vec, vec,
                  pl.BlockSpec((G_B, CHUNK, CHUNK), lambda i: (0, 0, 0)),
                  pl.BlockSpec((CHUNK, W_B), lambda i: (0, 0)),
                  vec],
        out_specs=[out, out],
        out_shape=[jax.ShapeDtypeStruct((n_chunks * CHUNK, W_B), F32)] * 2,
        compiler_params=_params(("arbitrary",)),
        name="gmlp",
    )(z, z, g_bv.reshape(1, W_B), b_bv.reshape(1, W_B), w_s, bs_wide, g_mix_b.reshape(1, W_B))


def _cmul(a, b):
    return a[0] * b[0] - a[1] * b[1], a[0] * b[1] + a[1] * b[0]


def _s5_kernel(u_ref, lr_ref, li_ref, ldt_ref, btr_ref, bti_ref, cr_ref, ci_ref, d_ref, h0r_ref, h0i_ref,
               y_ref, hr_ref, hi_ref, er_ref, ei_ref, gr_ref, gi_ref, *, L, nj, nsc, nb):
    hi = lax.Precision.HIGHEST
    lr, li = lr_ref[...], li_ref[...]
    dt = jnp.exp(ldt_ref[...])
    mag = jnp.exp(lr * dt)
    a = (mag * jnp.cos(li * dt), mag * jnp.sin(li * dt))
    den = lr * lr + li * li
    q = (((a[0] - 1.0) * lr + a[1] * li) / den, (a[1] * lr - (a[0] - 1.0) * li) / den)
    same = ((lax.broadcasted_iota(I32, (128, S5_SW), 0) >> int(math.log2(SSM_GROUP)))
            == (lax.broadcasted_iota(I32, (128, S5_SW), 1) >> int(math.log2(SSM_STATE))))
    bb = _cmul(q, (jnp.where(same, btr_ref[...], 0.0), jnp.where(same, bti_ref[...], 0.0)))
    c = (jnp.where(same, cr_ref[...], 0.0), jnp.where(same, ci_ref[...], 0.0))
    pw = [(jnp.ones_like(lr), jnp.zeros_like(lr))]
    for _ in range(L):
        pw.append(_cmul(pw[-1], a))
    ck = [_cmul(c, p) for p in pw]
    kbd = [(_dot_nt(bb[0], ck[k][0], hi) - _dot_nt(bb[1], ck[k][1], hi)).astype(BF16) for k in range(L)]
    mins = [tuple(x.astype(BF16) for x in _cmul(pw[L - 1 - s], bb)) for s in range(L)]
    cob = [tuple(x.astype(BF16) for x in ck[i + 1]) for i in range(L)]

    us = [u_ref[:, s * 128:(s + 1) * 128] for s in range(L)]
    ub = [x.astype(BF16) for x in us]
    sr = _dot(ub[0], mins[0][0])
    si = _dot(ub[0], mins[0][1])
    for s in range(1, L):
        sr = sr + _dot(ub[s], mins[s][0])
        si = si + _dot(ub[s], mins[s][1])
    al = pw[L]
    r = nsc * nb
    xl = [(sr[0:r], si[0:r])]
    for j in range(1, nj):
        t = _cmul(al, xl[-1])
        xl.append((t[0] + sr[j * r:(j + 1) * r], t[1] + si[j * r:(j + 1) * r]))
    alp = [al]
    for j in range(1, nj):
        alp.append(_cmul(alp[-1], al))
    er_ref[...] = xl[-1][0]
    ei_ref[...] = xl[-1][1]
    g = (h0r_ref[...], h0i_ref[...])
    for sc in range(nsc):
        gr_ref[sc * nb:(sc + 1) * nb, :] = g[0]
        gi_ref[sc * nb:(sc + 1) * nb, :] = g[1]
        t = _cmul(alp[nj - 1], g)
        g = (t[0] + er_ref[sc * nb:(sc + 1) * nb, :], t[1] + ei_ref[sc * nb:(sc + 1) * nb, :])
    hr_ref[...] = g[0]
    hi_ref[...] = g[1]
    gcat = (gr_ref[...], gi_ref[...])
    hprev = [gcat]
    for j in range(1, nj):
        t = _cmul(alp[j - 1], gcat)
        hprev.append((xl[j - 1][0] + t[0], xl[j - 1][1] + t[1]))
    hpr = jnp.concatenate([x[0] for x in hprev], axis=0).astype(BF16)
    hpi = jnp.concatenate([x[1] for x in hprev], axis=0).astype(BF16)
    y = (jnp.concatenate([d_ref[...]] * L, axis=1) * u_ref[...]
         + _dot_nt(hpr, jnp.concatenate([x[0] for x in cob], axis=0))
         - _dot_nt(hpi, jnp.concatenate([x[1] for x in cob], axis=0)))
    for s in range(L):
        conv = _dot(ub[s], jnp.concatenate(kbd[0:L - s], axis=1))
        y = y + conv if s == 0 else jnp.concatenate([y[:, :s * 128], y[:, s * 128:] + conv], axis=1)
    y_ref[...] = y


def _s5(u, lw, h0r, h0i, L, nj, nsc, nb):
    m, n = u.shape[1], u.shape[2]
    lanes = lambda x: x.reshape(S5_NLB, 1, S5_SW)
    rep = lambda x: jnp.tile(x.reshape(S5_NLB, 128, SSM_STATE), (1, 1, S5_GPB))
    vec = pl.BlockSpec((None, 1, S5_SW), lambda g: (g, 0, 0))
    mat = pl.BlockSpec((None, 128, S5_SW), lambda g: (g, 0, 0))
    st = pl.BlockSpec((None, nb, S5_SW), lambda g: (g, 0, 0))
    return pl.pallas_call(
        functools.partial(_s5_kernel, L=L, nj=nj, nsc=nsc, nb=nb),
        grid=(S5_NLB,),
        in_specs=[pl.BlockSpec((None, m, n), lambda g: (g, 0, 0)), vec, vec, vec, mat, mat, mat, mat,
                  pl.BlockSpec((None, 1, 128), lambda g: (g, 0, 0)), st, st],
        out_specs=[pl.BlockSpec((None, m, n), lambda g: (g, 0, 0)), st, st],
        out_shape=[jax.ShapeDtypeStruct((S5_NLB, m, n), F32)] + [jax.ShapeDtypeStruct((S5_NLB, nb, S5_SW), F32)] * 2,
        scratch_shapes=[pltpu.VMEM((nsc * nb, S5_SW), F32)] * 4,
        compiler_params=_params(("arbitrary",)),
        name="s5",
    )(u, lanes(lw["lam_re"]), lanes(lw["lam_im"]), lanes(jnp.repeat(lw["log_dt"], SSM_STATE)),
      rep(lw["ssm_b_re"].transpose(0, 2, 1)), rep(lw["ssm_b_im"].transpose(0, 2, 1)),
      rep(lw["ssm_c_re"]), rep(lw["ssm_c_im"]), lw["ssm_d"].reshape(S5_NLB, 1, 128), h0r, h0i)


def _s5_post_kernel(y_ref, w_ref, b_ref, g_ref, o_ref):
    zz = jax.nn.gelu(y_ref[...])
    out = zz * jax.nn.sigmoid(_dot(zz.astype(BF16), w_ref[...]) + b_ref[...])
    o_ref[...] = _rms(out, g_ref[...])


def _s5_post(y, w_glu, b_glu, g_mix_c):
    n = y.shape[0]
    tt = min(n, TM)
    vec = pl.BlockSpec((1, W_C), lambda i: (0, 0))
    return pl.pallas_call(
        _s5_post_kernel,
        grid=(n // tt,),
        in_specs=[pl.BlockSpec((tt, W_C), lambda i: (i, 0)), pl.BlockSpec((W_C, W_C), lambda i: (0, 0)), vec, vec],
        out_specs=pl.BlockSpec((tt, W_C), lambda i: (i, 0)),
        out_shape=jax.ShapeDtypeStruct((n, W_C), F32),
        compiler_params=_params(("arbitrary",)),
        name="s5_post",
    )(y, w_glu.astype(BF16), b_glu.reshape(1, W_C), g_mix_c.reshape(1, W_C))


def _mix_kernel(ap_ref, bp_ref, cp_ref, as_ref, bs_ref, cs_ref, g_ref, o_ref):
    def emit(a_ref, b_ref, c_ref):
        o_ref[:, 0:W_A] = _rms(a_ref[...], g_ref[...]).astype(BF16)
        o_ref[:, W_A:W_A + W_B] = b_ref[...].astype(BF16)
        o_ref[:, W_A + W_B:D] = c_ref[...].astype(BF16)

    last = pl.program_id(0) == NT - 1

    @pl.when(jnp.logical_not(last))
    def _():
        emit(ap_ref, bp_ref, cp_ref)

    @pl.when(last)
    def _():
        emit(as_ref, bs_ref, cs_ref)


def _mix(a_p, b_p, c_p, a_s, b_s, c_s, g_mix_a):
    pr = lambda w: pl.BlockSpec((TM, w), lambda i: (jnp.minimum(i, NT - 2), 0))
    sm = lambda w: pl.BlockSpec((TM, w), lambda i: (0, 0))
    return pl.pallas_call(
        _mix_kernel,
        grid=(NT,),
        in_specs=[pr(W_A), pr(W_B), pr(W_C), sm(W_A), sm(W_B), sm(W_C), pl.BlockSpec((1, W_A), lambda i: (0, 0))],
        out_specs=pl.BlockSpec((TM, D), lambda i: (i, 0)),
        out_shape=jax.ShapeDtypeStruct((NP, D), BF16),
        compiler_params=_params(("arbitrary",)),
        name="mix",
    )(a_p, b_p, c_p, a_s, b_s, c_s, g_mix_a.reshape(1, W_A))


def _router_kernel(h_ref, w_ref, b_ref, ind_ref, wd_ref, cnt_ref):
    i = pl.program_id(0)
    scores = jax.nn.sigmoid(_dot(h_ref[...], w_ref[...]))
    lane = lax.broadcasted_iota(I32, (TD, N_EXPERTS), 1).astype(F32)
    row = lax.broadcasted_iota(I32, (TD, N_EXPERTS), 0) + i * TD
    sel = scores + b_ref[...]
    ind = jnp.zeros((TD, N_EXPERTS), F32)
    for _ in range(TOP_K):
        m = jnp.max(sel, axis=-1, keepdims=True)
        idx = jnp.min(jnp.where(sel == m, lane, float(N_EXPERTS)), axis=-1, keepdims=True)
        hit = lane == idx
        ind = jnp.where(hit, 1.0, ind)
        sel = jnp.where(hit, -jnp.inf, sel)
    ind = jnp.where(row < N_REAL, ind, 0.0)
    wsel = ind * scores
    wd_ref[...] = wsel / jnp.sum(wsel + (1.0 - jnp.max(ind, axis=-1, keepdims=True)) / N_EXPERTS,
                                 axis=-1, keepdims=True) * ROUTE_SCALE
    ind_ref[...] = ind

    @pl.when(i == 0)
    def _():
        cnt_ref[...] = jnp.zeros((8, N_EXPERTS), F32)

    cnt_ref[...] += jnp.sum(ind, axis=0, keepdims=True)


def _router(h2b, w_router, b_router):
    til = pl.BlockSpec((TD, N_EXPERTS), lambda i: (i, 0))
    return pl.pallas_call(
        _router_kernel,
        grid=(NP // TD,),
        in_specs=[pl.BlockSpec((TD, D), lambda i: (i, 0)), pl.BlockSpec((D, N_EXPERTS), lambda i: (0, 0)),
                  pl.BlockSpec((1, N_EXPERTS), lambda i: (0, 0))],
        out_specs=[til, til, pl.BlockSpec((8, N_EXPERTS), lambda i: (0, 0))],
        out_shape=[jax.ShapeDtypeStruct((NP, N_EXPERTS), F32)] * 2 + [jax.ShapeDtypeStruct((8, N_EXPERTS), F32)],
        compiler_params=_params(("arbitrary",)),
        name="router",
    )(h2b, w_router.astype(BF16), b_router.reshape(1, N_EXPERTS))


def _group_starts(cnt):
    padded = jnp.floor((cnt + (BM - 1)) / BM) * BM
    tri = (lax.broadcasted_iota(I32, (N_EXPERTS, N_EXPERTS), 0)
           <= lax.broadcasted_iota(I32, (N_EXPERTS, N_EXPERTS), 1)).astype(F32)
    pend = jnp.dot(padded, tri, preferred_element_type=F32, precision=lax.Precision.HIGHEST)
    return padded, pend - padded, pend


def _plan_kernel(cnt_ref, be_ref, misc_ref):
    cnt = cnt_ref[...]
    _, pstart, pend = _group_starts(cnt)
    rowi = (lax.broadcasted_iota(I32, (NBP, N_EXPERTS), 0) * BM).astype(F32)
    be = jnp.minimum(jnp.sum(jnp.where(pend[0:1] <= rowi, 1.0, 0.0), axis=-1, keepdims=True), N_EXPERTS - 1.0)
    be_ref[...] = jnp.broadcast_to(be, (NBP, 128)).astype(I32)
    misc_ref[0:8, :] = pstart.astype(I32)
    misc_ref[8:16, :] = cnt.astype(I32)
    misc_ref[16:24, :] = (pend / BM).astype(I32)


def _plan(cnt):
    return pl.pallas_call(
        _plan_kernel,
        out_shape=[jax.ShapeDtypeStruct((NBP, 128), I32), jax.ShapeDtypeStruct((24, N_EXPERTS), I32)],
        name="moe_plan",
    )(cnt)


def _slots_kernel(ind_ref, wd_ref, cnt_ref, dest_ref, w_ref, run_ref):
    @pl.when(pl.program_id(0) == 0)
    def _():
        run_ref[...] = jnp.zeros((8, N_EXPERTS), F32)

    ind = ind_ref[...]
    _, pstart, _ = _group_starts(cnt_ref[...])
    ltri = (lax.broadcasted_iota(I32, (TD, TD), 1) < lax.broadcasted_iota(I32, (TD, TD), 0)).astype(BF16)
    pos = _dot(ltri, ind.astype(BF16)) + run_ref[0:1, :]
    run_ref[...] += jnp.sum(ind, axis=0, keepdims=True)
    cur = jnp.where(ind > 0.0, pstart[0:1] + pos, BIG)
    lane = lax.broadcasted_iota(I32, (TD, 128), 1)
    dest = jnp.full((TD, 128), -1.0, F32)
    wts = jnp.zeros((TD, 128), F32)
    wd = wd_ref[...]
    for k in range(TOP_K):
        m = jnp.min(cur, axis=-1, keepdims=True)
        hit = cur == m
        wk = jnp.sum(jnp.where(hit, wd, 0.0), axis=-1, keepdims=True)
        dest = jnp.where(lane == k, jnp.where(m < BIG, m, -1.0), dest)
        wts = jnp.where(lane == k, jnp.where(m < BIG, wk, 0.0), wts)
        cur = jnp.where(hit, BIG, cur)
    dest_ref[...] = dest.astype(I32)
    w_ref[...] = wts


def _slots(ind, wd, cnt):
    til = pl.BlockSpec((TD, N_EXPERTS), lambda i: (i, 0))
    out = pl.BlockSpec((TD, 128), lambda i: (i, 0))
    return pl.pallas_call(
        _slots_kernel,
        grid=(NP // TD,),
        in_specs=[til, til, pl.BlockSpec((8, N_EXPERTS), lambda i: (0, 0))],
        out_specs=[out, out],
        out_shape=[jax.ShapeDtypeStruct((NP, 128), I32), jax.ShapeDtypeStruct((NP, 128), F32)],
        scratch_shapes=[pltpu.VMEM((8, N_EXPERTS), F32)],
        compiler_params=_params(("arbitrary",)),
        name="moe_slots",
    )(ind, wd, cnt)


def _dispatch_kernel(ps_ref, cnt_ref, nbu_ref, dest_ref, x_ref, xs_ref, zero_ref, sem):
    i = pl.program_id(0)

    def token(ref, t):
        start = t * ROW_TILES if isinstance(t, int) else pl.multiple_of(t * ROW_TILES, ROW_TILES)
        return ref.at[pl.ds(start, ROW_TILES), :]

    def row_copy(r, k):
        return pltpu.make_async_copy(token(x_ref, r), token(xs_ref, dest_ref[r * 8 + k]), sem)

    def zero_rows(d, n):
        rows = n * ROW_TILES
        return pltpu.make_async_copy(zero_ref.at[pl.ds(0, rows), :],
                                     xs_ref.at[pl.ds(pl.multiple_of(d * ROW_TILES, ROW_TILES), rows), :], sem)

    def zero_block(b):
        rows = BM * ROW_TILES
        return pltpu.make_async_copy(zero_ref, xs_ref.at[pl.ds(pl.multiple_of(b * rows, rows), rows), :], sem)

    @pl.when(i < N_DTILES)
    def _():
        n_rows = jnp.minimum(TD, N_REAL - i * TD)

        def issue(r, c):
            for k in range(TOP_K):
                row_copy(r, k).start(priority=k % 2)
            return c

        def drain(r, c):
            for k in range(TOP_K):
                row_copy(r, k).wait()
            return c

        lax.fori_loop(0, n_rows, issue, 0)
        lax.fori_loop(0, n_rows, drain, 0)

    @pl.when(i == N_DTILES)
    def _():
        zero_ref[...] = jnp.zeros(zero_ref.shape, F32)

        def per_expert(e, c):
            cnt = cnt_ref[e]
            base = ps_ref[e] + cnt
            rem = (((cnt + (BM - 1)) >> BM_SHIFT) << BM_SHIFT) - cnt
            for start in (True, False):
                for k in range(BM_SHIFT):
                    cp = zero_rows(base + ((rem >> (k + 1)) << (k + 1)), 1 << k)

                    @pl.when((rem & (1 << k)) != 0)
                    def _():
                        cp.start() if start else cp.wait()
            return c

        lax.fori_loop(0, N_EXPERTS, per_expert, 0)
        lax.fori_loop(nbu_ref[0], NB, lambda b, c: (zero_block(b).start(), c)[1], 0)
        lax.fori_loop(nbu_ref[0], NB, lambda b, c: (zero_block(b).wait(), c)[1], 0)


def _dispatch(pstart_i, cnt_i, nbu, dest_flat, x):
    return pl.pallas_call(
        _dispatch_kernel,
        grid_spec=pltpu.PrefetchScalarGridSpec(
            num_scalar_prefetch=3, grid=(N_DTILES + 1,),
            in_specs=[pl.BlockSpec((TD * 8,), lambda i, *_: (jnp.minimum(i, N_DTILES - 1),),
                                   memory_space=pltpu.SMEM),
                      pl.BlockSpec((TD * ROW_TILES, 128), lambda i, *_: (jnp.minimum(i, N_DTILES - 1), 0))],
            out_specs=pl.BlockSpec(memory_space=pl.ANY),
            scratch_shapes=[pltpu.VMEM((BM * ROW_TILES, 128), F32), pltpu.SemaphoreType.DMA(())]),
        out_shape=jax.ShapeDtypeStruct((CAP * ROW_TILES, 128), F32),
        compiler_params=_params(("arbitrary",)),
        name="moe_dispatch",
    )(pstart_i, cnt_i, nbu, dest_flat, x)


def _expert_kernel(be_ref, nbu_ref, gend_ref, x_ref, wg_hbm, wu_hbm, wd_hbm, o_ref,
                   wg_ref, wu_ref, wd_ref, slot_ref, sem, *, layer):
    i = pl.program_id(0)
    nbu = nbu_ref[0]

    def weight_copies(e, s):
        return [pltpu.make_async_copy(hbm.at[layer, e], buf.at[s], sem.at[s, j])
                for j, (hbm, buf) in enumerate(((wg_hbm, wg_ref), (wu_hbm, wu_ref), (wd_hbm, wd_ref)))]

    @pl.when(i < nbu)
    def _():
        e = be_ref[i]

        @pl.when(i == 0)
        def _():
            slot_ref[0] = 0
            for cp in weight_copies(e, 0):
                cp.start()

        new_group = jnp.logical_or(i == 0, be_ref[jnp.maximum(i - 1, 0)] != e)

        @pl.when(jnp.logical_and(new_group, i > 0))
        def _():
            slot_ref[0] = 1 - slot_ref[0]

        s = slot_ref[0]

        @pl.when(new_group)
        def _():
            for cp in weight_copies(e, s):
                cp.wait()
            nxt = gend_ref[e]

            @pl.when(nxt < nbu)
            def _():
                for cp in weight_copies(be_ref[jnp.minimum(nxt, NB - 1)], 1 - s):
                    cp.start()

        x = _load_token_rows(x_ref, BM).astype(BF16)
        g = _dot(x, wg_ref[s].astype(BF16))
        u = _dot(x, wu_ref[s].astype(BF16))
        a = (g * jax.nn.sigmoid(g) * u).astype(BF16)
        _store_token_rows(o_ref, _dot(a, wd_ref[s].astype(BF16)))

    @pl.when(i >= nbu)
    def _():
        o_ref[...] = jnp.zeros(o_ref.shape, F32)


def _experts(layer, be, nbu, group_end, xs, w_gate, w_up, w_down):
    any_spec = pl.BlockSpec(memory_space=pl.ANY)
    return pl.pallas_call(
        functools.partial(_expert_kernel, layer=layer),
        grid_spec=pltpu.PrefetchScalarGridSpec(
            num_scalar_prefetch=3, grid=(NB,),
            in_specs=[pl.BlockSpec((BM * ROW_TILES, 128), lambda i, be, nbu, ge: (jnp.minimum(i, nbu[0] - 1), 0)),
                      any_spec, any_spec, any_spec],
            out_specs=pl.BlockSpec((BM * ROW_TILES, 128), lambda i, be, nbu, ge: (i, 0)),
            scratch_shapes=[pltpu.VMEM((2, D, D_EXPERT), F32), pltpu.VMEM((2, D, D_EXPERT), F32),
                            pltpu.VMEM((2, D_EXPERT, D), F32), pltpu.SMEM((1,), I32),
                            pltpu.SemaphoreType.DMA((2, 3))]),
        out_shape=jax.ShapeDtypeStruct((CAP * ROW_TILES, 128), F32),
        compiler_params=_params(("arbitrary",)),
        name="moe_experts",
    )(be, nbu, group_end, xs, w_gate, w_up, w_down)


def _shared_kernel(h_ref, wg_ref, wu_ref, wd_ref, o_ref):
    h = h_ref[...]
    g = _dot(h, wg_ref[...])
    u = _dot(h, wu_ref[...])
    o_ref[...] = _dot((g * jax.nn.sigmoid(g) * u).astype(BF16), wd_ref[...])


def _shared(h2b, wg, wu, wd):
    return pl.pallas_call(
        _shared_kernel,
        grid=(NT,),
        in_specs=[pl.BlockSpec((TM, D), lambda i: (i, 0)),
                  pl.BlockSpec((D, D_SHARED), lambda i: (0, 0)),
                  pl.BlockSpec((D, D_SHARED), lambda i: (0, 0)),
                  pl.BlockSpec((D_SHARED, D), lambda i: (0, 0))],
        out_specs=pl.BlockSpec((TM, D), lambda i: (i, 0)),
        out_shape=jax.ShapeDtypeStruct((NP, D), F32),
        compiler_params=_params(("arbitrary",)),
        name="moe_shared",
    )(h2b, wg.astype(BF16), wu.astype(BF16), wd.astype(BF16))


def _combine_kernel(dest_ref, nxt_ref, w_ref, ys_ref, sh_ref, x_ref, gr_ref, gt_ref, o_ref, rows_ref, acc_ref, sem):
    i = pl.program_id(0)
    n_tiles = NP // TC_
    slot = i % 2

    def gather(d_ref, s, start):
        def body(r, c):
            for k in range(TOP_K):
                d = jnp.maximum(d_ref[r * 8 + k], 0)
                cp = pltpu.make_async_copy(
                    ys_ref.at[pl.ds(pl.multiple_of(d * ROW_TILES, ROW_TILES), ROW_TILES), :],
                    rows_ref.at[s, k, pl.ds(pl.multiple_of(r * ROW_TILES, ROW_TILES), ROW_TILES), :], sem.at[s])
                if start:
                    cp.start(priority=k % 2)
                else:
                    cp.wait()
            return c

        lax.fori_loop(0, TC_, body, 0)

    @pl.when(i == 0)
    def _():
        gather(dest_ref, 0, True)

    @pl.when(i + 1 < n_tiles)
    def _():
        gather(nxt_ref, 1 - slot, True)

    gather(dest_ref, slot, False)
    def weigh(t, c):
        rows = pl.ds(pl.multiple_of(t * ROW_TILES, ROW_TILES), ROW_TILES)
        acc = w_ref[t * 8] * rows_ref[slot, 0, rows, :]
        for k in range(1, TOP_K):
            acc = acc + w_ref[t * 8 + k] * rows_ref[slot, k, rows, :]
        acc_ref[rows, :] = acc
        return c

    lax.fori_loop(0, TC_, weigh, 0, unroll=4)
    routed = _load_token_rows(acc_ref, TC_)
    last = i >= N_PROMPT // TC_
    o_ref[...] = x_ref[...] + _pick(last, gt_ref, gr_ref) * (routed + sh_ref[...])


def _combine(dest_flat, w6, ys, shared, x, gate):
    n_tiles = NP // TC_
    n_p = N_PROMPT // TC_
    per = TM // TC_
    til = pl.BlockSpec((TC_, D), lambda i: (i, 0))
    return pl.pallas_call(
        _combine_kernel,
        grid=(n_tiles,),
        in_specs=[pl.BlockSpec((TC_ * 8,), lambda i: (i,), memory_space=pltpu.SMEM),
                  pl.BlockSpec((TC_ * 8,), lambda i: (jnp.minimum(i + 1, n_tiles - 1),), memory_space=pltpu.SMEM),
                  pl.BlockSpec((TC_ * 8,), lambda i: (i,), memory_space=pltpu.SMEM),
                  pl.BlockSpec(memory_space=pl.ANY), til, til,
                  pl.BlockSpec((None, 1, D), lambda i: (i // per, 0, 0)),
                  pl.BlockSpec((TC_, D), lambda i: (jnp.maximum(i - n_p, 0), 0))],
        out_specs=til,
        out_shape=jax.ShapeDtypeStruct((NP, D), F32),
        scratch_shapes=[pltpu.VMEM((2, TOP_K, TC_ * ROW_TILES, 128), F32), pltpu.VMEM((TC_ * ROW_TILES, 128), F32),
                        pltpu.SemaphoreType.DMA((2,))],
        compiler_params=_params(("arbitrary",)),
        name="moe_combine",
    )(dest_flat, dest_flat, w6, ys, shared, x, gate[0], gate[1])


def _pad_rows(a, rows):
    return jnp.concatenate([a, jnp.zeros((rows - a.shape[0],) + a.shape[1:], a.dtype)], axis=0)


def _s5_layer(z, lw, h0r_s, h0i_s):
    g_mix_c = lw["g_mix"][W_A + W_B:]
    nj = nsc = 16
    state = lambda h, b: h.reshape(S5_NLB, b, S5_GPB, SSM_STATE).transpose(1, 0, 2, 3).reshape(b, G_C, SSM_STATE)
    uc_p = z[:N_PROMPT, IN_COLS - W_C:]
    u_p = uc_p.reshape(B_P, nsc, nj, S5_L_P, S5_NLB, 128).transpose(4, 2, 1, 0, 3, 5)
    u_p = u_p.reshape(S5_NLB, nj * nsc * B_P, S5_L_P * 128)
    zeros_h = jnp.zeros((S5_NLB, B_P, S5_SW), F32)
    y_p, hr_p, hi_p = _s5(u_p, lw, zeros_h, zeros_h, S5_L_P, nj, nsc, B_P)
    y_p = y_p.reshape(S5_NLB, nj, nsc, B_P, S5_L_P, 128).transpose(3, 2, 1, 4, 0, 5).reshape(N_PROMPT, W_C)
    c_p = _s5_post(y_p, lw["w_glu"], lw["b_glu"], g_mix_c)

    uc_s = z[N_PROMPT:N_REAL, IN_COLS - W_C:]
    u_s = uc_s.reshape(B_S, T_S, S5_NLB, 128).transpose(2, 0, 1, 3).reshape(S5_NLB, B_S, S5_L_S * 128)
    h0 = lambda h: h.reshape(B_S, S5_NLB, S5_SW).transpose(1, 0, 2)
    y_s, hr_s, hi_s = _s5(u_s, lw, h0(h0r_s), h0(h0i_s), S5_L_S, 1, 1, B_S)
    y_s = y_s.reshape(S5_NLB, B_S, T_S, 128).transpose(1, 2, 0, 3).reshape(N_SAMPLE, W_C)
    c_s = _s5_post(y_s, lw["w_glu"], lw["b_glu"], g_mix_c)
    return c_p, c_s, state(hr_p, B_P), state(hi_p, B_P), state(hr_s, B_S), state(hi_s, B_S)


def _layer(x, l, lw, experts, mod, bias_p, bias_s, cache_k, cache_v, h0r_s, h0i_s):
    sh1, sc1, g1, sh2, sc2, g2 = [_tile_tabs(m) for m in jnp.split(mod, 6, axis=-1)]
    (h,) = _modnorm(x, lw["g_attn"], sc1, sh1, False)
    z = _matmul(h, lw["w_in"].astype(BF16), 1536)

    a_p = _attn_prompt(z, bias_p)
    a_s = _attn_sample(z, cache_k, cache_v, l, *bias_s)

    g_mix_b = lw["g_mix"][W_A:W_A + W_B]
    gm = (lw["g_bv"], lw["b_bv"], lw["w_s"], lw["b_s"], g_mix_b)
    b_p, _ = _gmlp(z, N_PROMPT // CHUNK, 3 * W_A // W_B, 3 * W_A // W_B + 1, *gm)
    uv_s = z[N_PROMPT:N_REAL, 3 * W_A:3 * W_A + 2 * W_B].reshape(B_S, T_S, 2 * W_B)
    uv_s = jnp.pad(uv_s, ((0, 0), (0, CHUNK - T_S), (0, 0))).reshape(B_S * CHUNK, 2 * W_B)
    b_s, vn_s = _gmlp(uv_s, B_S, 0, 1, *gm)
    b_s = b_s.reshape(B_S, CHUNK, W_B)[:, :T_S].reshape(N_SAMPLE, W_B)
    new_vb = vn_s.reshape(B_S, CHUNK, W_B)[:, :T_S]

    c_p, c_s, hr_p, hi_p, hr_s, hi_s = _s5_layer(z, lw, h0r_s, h0i_s)

    mixed = _mix(a_p, b_p, c_p, _pad_rows(a_s, TM), _pad_rows(b_s, TM), _pad_rows(c_s, TM), lw["g_mix"][:W_A])
    x = _matmul_residual(mixed, lw["w_out"].astype(BF16), x, g1, 1024)

    h2b, h2 = _modnorm(x, lw["g_ffn"], sc2, sh2, True)
    ind, wd, cnt = _router(h2b, lw["w_router"], lw["b_router"])
    be, misc = _plan(cnt)
    dest, w6 = _slots(ind, wd, cnt)
    dest_flat = dest[:, :8].reshape(-1)
    xs = _dispatch(misc[0], misc[8], misc[16, N_EXPERTS - 1:], dest_flat, h2)
    ys = _experts(l, be[:NB, 0], misc[16, N_EXPERTS - 1:], misc[16], xs, *experts)
    shared = _shared(h2b, lw["w_sh_gate"], lw["w_sh_up"], lw["w_sh_down"])
    x = _combine(dest_flat, w6[:, :8].reshape(-1), ys, shared, x, g2)

    kv = lambda lo: (z[:N_PROMPT, lo:lo + W_A].reshape(B_P, T_P, H_A, HEAD_DIM)[:, T_P - BUF:],
                     z[N_PROMPT:N_REAL, lo:lo + W_A].reshape(B_S, T_S, H_A, HEAD_DIM))
    (k_p, k_s), (v_p, v_s) = kv(W_A), kv(2 * W_A)
    return x, (k_p, v_p, k_s, v_s, new_vb, hr_p, hi_p, hr_s, hi_s)


def kernel(x_prompt, x_sample, cache_a_k, cache_a_v, state_c_re, state_c_im, c_prompt, c_sample, rel_bias, w_ada, b_ada, g_attn, w_in, g_bv, b_bv, w_s, b_s, lam_re, lam_im, log_dt, ssm_b_re, ssm_b_im, ssm_c_re, ssm_c_im, ssm_d, w_glu, b_glu, g_mix, w_out, g_ffn, w_router, b_router, w_gate, w_up, w_down, w_sh_gate, w_sh_up, w_sh_down, g_final):
    assert x_prompt.shape == (B_P, T_P, D) and x_sample.shape == (B_S, T_S, D)
    assert cache_a_k.shape == (DEPTH, B_S, BUF, H_A, HEAD_DIM)
    per_layer = dict(g_attn=g_attn, w_in=w_in, g_bv=g_bv, b_bv=b_bv, w_s=w_s, b_s=b_s, lam_re=lam_re,
                     lam_im=lam_im, log_dt=log_dt, ssm_b_re=ssm_b_re, ssm_b_im=ssm_b_im, ssm_c_re=ssm_c_re,
                     ssm_c_im=ssm_c_im, ssm_d=ssm_d, w_glu=w_glu, b_glu=b_glu, g_mix=g_mix, w_out=w_out,
                     g_ffn=g_ffn, w_router=w_router, b_router=b_router, w_sh_gate=w_sh_gate, w_sh_up=w_sh_up,
                     w_sh_down=w_sh_down)

    x = _pad_rows(jnp.concatenate([x_prompt.reshape(N_PROMPT, D), x_sample.reshape(N_SAMPLE, D)], axis=0), NP)
    c_all = _pad_rows(jnp.concatenate([c_prompt, c_sample], axis=0), 16)
    mod = _mod_all(c_all, w_ada, b_ada)
    bias_p = _bias_bank(rel_bias, *_prompt_bias_tables())
    bias_s = _sample_bias_by_head(_bias_bank(rel_bias, *_sample_bias_tables())[0])
    cache_k = cache_a_k.reshape(DEPTH * B_S, BUF * H_A, HEAD_DIM)
    cache_v = cache_a_v.reshape(DEPTH * B_S, BUF * H_A, HEAD_DIM)

    outs = []
    for l in range(DEPTH):
        lw = {name: w[l] for name, w in per_layer.items()}
        x, o = _layer(x, l, lw, (w_gate, w_up, w_down), mod[l], bias_p, bias_s, cache_k, cache_v,
                      state_c_re[l], state_c_im[l])
        outs.append(o)
    y = _finalnorm(x, g_final)
    stacked = [jnp.stack([o[j] for o in outs]) for j in range(9)]
    return (y[:N_PROMPT].reshape(B_P, T_P, D), y[N_PROMPT:N_REAL].reshape(B_S, T_S, D), *stacked)
```

```python
import functools
import math

import jax
import jax.numpy as jnp
import numpy as np
from jax import lax
from jax.experimental import pallas as pl
from jax.experimental.pallas import tpu as pltpu

F32 = jnp.float32
BF16 = jnp.bfloat16
I32 = jnp.int32

D = 2048
B_P, T_P = 2, 4096
B_S, T_S = 8, 8
DEPTH = 2
HEAD_DIM = 64
W_A = 1024
H_A = 16
W_B = 512
G_B = 8
W_C = 512
SSM_GROUP = 16
G_C = 32
SSM_STATE = 64
IN_COLS = 3 * W_A + 2 * W_B + W_C
DILATIONS = (1, 4, 16)
NK = 128
ATTN_UNITS = (8, 8, 8)
BUF = 2048
CHUNK = 128
N_BUCKETS = 32
REL_MAX_DIST = 2048
N_EXPERTS = 64
TOP_K = 6
D_EXPERT = 512
D_SHARED = 1024
ROUTE_SCALE = 2.5
EPS = 1e-6

N_PROMPT = B_P * T_P
N_SAMPLE = B_S * T_S
N_REAL = N_PROMPT + N_SAMPLE
TM = 512
NT = N_PROMPT // TM + 1
NP = NT * TM
TD = 256
N_DTILES = -(-N_REAL // TD)
ROW_TILES = D // 128
BM = 256
BM_SHIFT = 8
NB = -(-(N_REAL * TOP_K + N_EXPERTS * (BM - 1)) // BM)
NBP = -(-NB // 8) * 8
CAP = NB * BM
TC_ = 128
KF_ROWS = 2176
S5_L_P = 16
S5_L_S = T_S
S5_GPB = 128 // SSM_GROUP
S5_NLB = G_C // S5_GPB
S5_SW = S5_GPB * SSM_STATE
BIG = 1e9
assert BM == 1 << BM_SHIFT

VMEM_LIMIT_BYTES = 56 * 1024 * 1024


def _params(dims):
    return pltpu.CompilerParams(dimension_semantics=dims, vmem_limit_bytes=VMEM_LIMIT_BYTES)


def _dot(a, b):
    return jnp.dot(a, b, preferred_element_type=F32)


def _dot_nt(a, b, precision=None):
    return lax.dot_general(a, b, (((1,), (1,)), ((), ())), preferred_element_type=F32,
                           precision=precision)


def _rms(x, g):
    return x * lax.rsqrt(jnp.mean(x * x, axis=-1, keepdims=True) + EPS) * g


def _mod_kernel(c_ref, w_ref, b_ref, o_ref):
    c = c_ref[...]
    s = (c * jax.nn.sigmoid(c)).astype(BF16)
    o_ref[...] = _dot(s, w_ref[...].astype(BF16)) + b_ref[...]


def _mod_all(c_all, w_ada, b_ada):
    tn = 1024
    return pl.pallas_call(
        _mod_kernel,
        grid=(DEPTH, 6 * D // tn),
        in_specs=[pl.BlockSpec((16, D), lambda l, j: (0, 0)),
                  pl.BlockSpec((None, D, tn), lambda l, j: (l, 0, j)),
                  pl.BlockSpec((None, 1, tn), lambda l, j: (l, 0, j))],
        out_specs=pl.BlockSpec((None, 16, tn), lambda l, j: (l, 0, j)),
        out_shape=jax.ShapeDtypeStruct((DEPTH, 16, 6 * D), F32),
        compiler_params=_params(("arbitrary", "arbitrary")),
        name="mod",
    )(c_all, w_ada, b_ada.reshape(DEPTH, 1, 6 * D))


def _tile_tabs(m):
    row = jnp.concatenate([jnp.repeat(m[0:B_P], NT // B_P, axis=0), jnp.zeros((1, m.shape[1]), F32)], 0)
    tok = jnp.concatenate([jnp.repeat(m[B_P:B_P + B_S], T_S, axis=0),
                           jnp.zeros((TM - N_SAMPLE, m.shape[1]), F32)], 0)
    return row[:, None, :], tok


def _pick(last, tok_ref, row_ref):
    return jnp.where(last, tok_ref[...], row_ref[...])


def _modnorm_kernel(x_ref, g_ref, scr_ref, sct_ref, shr_ref, sht_ref, *o_refs):
    last = pl.program_id(0) == NT - 1
    y = _rms(x_ref[...], g_ref[...])
    y = y * (1.0 + _pick(last, sct_ref, scr_ref)) + _pick(last, sht_ref, shr_ref)
    o_refs[0][...] = y.astype(BF16)
    if len(o_refs) > 1:
        _store_token_rows(o_refs[1], y)


def _load_token_rows(ref, n, first=0):
    return jnp.concatenate([ref[pl.ds(first * ROW_TILES + j, n, stride=ROW_TILES), :] for j in range(ROW_TILES)],
                           axis=1)


def _store_token_rows(ref, val, first=0):
    for j in range(ROW_TILES):
        ref[pl.ds(first * ROW_TILES + j, val.shape[0], stride=ROW_TILES), :] = val[:, j * 128:(j + 1) * 128]


def _modnorm(x, g, sc, sh, with_token_rows):
    row = pl.BlockSpec((None, 1, D), lambda i: (i, 0, 0))
    tok = pl.BlockSpec((TM, D), lambda i: (0, 0))
    til = pl.BlockSpec((TM, D), lambda i: (i, 0))
    out_specs, out_shape = [til], [jax.ShapeDtypeStruct((NP, D), BF16)]
    if with_token_rows:
        out_specs.append(pl.BlockSpec((TM * ROW_TILES, 128), lambda i: (i, 0)))
        out_shape.append(jax.ShapeDtypeStruct((NP * ROW_TILES, 128), F32))
    return pl.pallas_call(
        _modnorm_kernel,
        grid=(NT,),
        in_specs=[til, pl.BlockSpec((1, D), lambda i: (0, 0)), row, tok, row, tok],
        out_specs=out_specs,
        out_shape=out_shape,
        compiler_params=_params(("arbitrary",)),
        name="modnorm",
    )(x, g.reshape(1, D), sc[0], sc[1], sh[0], sh[1])


def _finalnorm_kernel(x_ref, g_ref, o_ref):
    o_ref[...] = _rms(x_ref[...], g_ref[...])


def _finalnorm(x, g):
    til = pl.BlockSpec((TM, D), lambda i: (i, 0))
    return pl.pallas_call(
        _finalnorm_kernel,
        grid=(NT,),
        in_specs=[til, pl.BlockSpec((1, D), lambda i: (0, 0))],
        out_specs=til,
        out_shape=jax.ShapeDtypeStruct((NP, D), F32),
        compiler_params=_params(("arbitrary",)),
        name="finalnorm",
    )(x, g.reshape(1, D))


def _mm_norm_kernel(x_ref, g_ref, scr_ref, sct_ref, shr_ref, sht_ref, b_ref, o_ref):
    last = pl.program_id(1) == NT - 1
    y = _rms(x_ref[...], g_ref[...])
    y = y * (1.0 + _pick(last, sct_ref, scr_ref)) + _pick(last, sht_ref, shr_ref)
    o_ref[...] = _dot(y.astype(BF16), b_ref[...])


def _matmul_norm(x, g, sc, sh, b, tn):
    k, n = b.shape
    row = pl.BlockSpec((None, 1, D), lambda c, i: (i, 0, 0))
    tok = pl.BlockSpec((TM, D), lambda c, i: (0, 0))
    return pl.pallas_call(
        _mm_norm_kernel,
        grid=(n // tn, NT),
        in_specs=[pl.BlockSpec((TM, D), lambda c, i: (i, 0)), pl.BlockSpec((1, D), lambda c, i: (0, 0)),
                  row, tok, row, tok, pl.BlockSpec((k, tn), lambda c, i: (0, c))],
        out_specs=pl.BlockSpec((TM, tn), lambda c, i: (i, c)),
        out_shape=jax.ShapeDtypeStruct((NP, n), F32),
        compiler_params=_params(("arbitrary", "arbitrary")),
        name="matmul_norm",
    )(x, g.reshape(1, D), sc[0], sc[1], sh[0], sh[1], b)


def _mm_mix_kernel(ap_ref, bp_ref, cp_ref, as_ref, bs_ref, cs_ref, gm_ref, w_ref, x_ref, gr_ref, gt_ref, o_ref):
    last = pl.program_id(1) == NT - 1
    a = jnp.where(last, as_ref[...], ap_ref[...])
    b = jnp.where(last, bs_ref[...], bp_ref[...])
    c = jnp.where(last, cs_ref[...], cp_ref[...])
    mixed = jnp.concatenate([_rms(a, gm_ref[...]).astype(BF16), b.astype(BF16), c.astype(BF16)], axis=1)
    o_ref[...] = x_ref[...] + _pick(last, gt_ref, gr_ref) * _dot(mixed, w_ref[...])


def _matmul_mix_residual(a_p, b_p, c_p, a_s, b_s, c_s, g_mix_a, w, x, gate, tn):
    pr = lambda wd: pl.BlockSpec((TM, wd), lambda c, i: (jnp.minimum(i, NT - 2), 0))
    sm = lambda wd: pl.BlockSpec((TM, wd), lambda c, i: (0, 0))
    return pl.pallas_call(
        _mm_mix_kernel,
        grid=(D // tn, NT),
        in_specs=[pr(W_A), pr(W_B), pr(W_C), sm(W_A), sm(W_B), sm(W_C), pl.BlockSpec((1, W_A), lambda c, i: (0, 0)),
                  pl.BlockSpec((D, tn), lambda c, i: (0, c)),
                  pl.BlockSpec((TM, tn), lambda c, i: (i, c)),
                  pl.BlockSpec((None, 1, tn), lambda c, i: (i, 0, c)),
                  pl.BlockSpec((TM, tn), lambda c, i: (0, c))],
        out_specs=pl.BlockSpec((TM, tn), lambda c, i: (i, c)),
        out_shape=jax.ShapeDtypeStruct((NP, D), F32),
        compiler_params=_params(("arbitrary", "arbitrary")),
        name="matmul_mix_residual",
    )(a_p, b_p, c_p, a_s, b_s, c_s, g_mix_a.reshape(1, W_A), w, x, gate[0], gate[1])


def _t5_bucket(dist):
    max_exact = N_BUCKETS // 2
    dist = np.maximum(dist, 0)
    ratio = (np.log(np.maximum(dist, 1).astype(np.float32) / np.float32(max_exact))
             / np.float32(math.log(REL_MAX_DIST / max_exact)))
    large = np.minimum(max_exact + (ratio * np.float32(N_BUCKETS - max_exact)).astype(np.int32), N_BUCKETS - 1)
    return np.where(dist < max_exact, dist, large)


def _bias_kernel(rb_ref, bucket_ref, add_ref, o_ref):
    h = pl.program_id(1)
    bucket = bucket_ref[...]
    acc = jnp.full(bucket.shape, -jnp.inf, F32)
    for b in range(N_BUCKETS):
        acc = jnp.where(bucket == b, rb_ref[b * H_A + h], acc)
    o_ref[...] = acc + add_ref[...]


def _bias_bank(rel_bias, bucket, add):
    n, r, c = bucket.shape
    blk = pl.BlockSpec((None, r, c), lambda i, h, rb: (i, 0, 0))
    return pl.pallas_call(
        _bias_kernel,
        grid_spec=pltpu.PrefetchScalarGridSpec(
            num_scalar_prefetch=1, grid=(n, H_A), in_specs=[blk, blk],
            out_specs=pl.BlockSpec((None, None, r, c), lambda i, h, rb: (i, h, 0, 0))),
        out_shape=jax.ShapeDtypeStruct((n, H_A, r, c), F32),
        compiler_params=_params(("arbitrary", "arbitrary")),
        name="bias_bank",
    )(rel_bias.reshape(-1), bucket, add)


def _prompt_bias_tables():
    qi = np.arange(NK)[:, None]
    kj = np.arange(2 * NK)[None, :]
    rel = qi + NK - kj
    band = (rel >= 0) & (rel <= NK)
    bucket = np.stack([np.where(band, _t5_bucket(rel * d), -1) for d in DILATIONS]).astype(np.int32)
    return jnp.asarray(bucket), jnp.zeros(bucket.shape, F32)


def _sample_bias_tables():
    dist_np = BUF + np.arange(T_S)[:, None] - np.arange(KF_ROWS)[None, :]
    mult = np.zeros(dist_np.shape, np.float64)
    for d in DILATIONS:
        mult += (dist_np >= 0) & (dist_np % d == 0) & (dist_np // d <= NK)
    mult = np.where(np.arange(KF_ROWS)[None, :] < BUF + T_S, mult, 0.0)
    valid = mult > 0
    add = np.where(valid, np.log(np.maximum(mult, 1.0)), 0.0).astype(np.float32)
    bucket = np.where(valid, _t5_bucket(dist_np), -1).astype(np.int32)
    return jnp.asarray(bucket)[None], jnp.asarray(add)[None]


def _attn_prompt_kernel(q_ref, k_ref, v_ref, bias_ref, o_ref, *stats):
    lane = lax.broadcasted_iota(I32, (NK, 128), 1)
    head0 = lane < HEAD_DIM
    prev_cols = lax.broadcasted_iota(I32, (NK, 2 * NK), 1) < NK
    ones = jnp.ones((NK, 128), F32)
    n_br = len(DILATIONS)
    m_refs, l_refs, a_refs = stats[0:n_br], stats[n_br:2 * n_br], stats[2 * n_br:3 * n_br]
    for di in range(n_br):
        d = DILATIONS[di]
        n_blocks = T_P // (d * NK)
        n_units = ATTN_UNITS[di]
        run = min(n_units, n_blocks)
        assert n_units % run == 0 and n_blocks % run == 0

        def group(gi, carry, di=di, d=d, n_blocks=n_blocks, n_units=n_units, run=run):
            def idx(start):
                return pl.ds(start, NK) if d == 1 else pl.ds(start, NK, stride=d)

            def keys(start):
                return k_ref[idx(start), :].astype(BF16)

            def values(start):
                return jnp.concatenate([v_ref[idx(start), :], ones], axis=1).astype(BF16)

            units = []
            for j in range(n_units):
                u = gi * n_units + j
                r = u >> int(math.log2(n_blocks))
                n = u & (n_blocks - 1)
                cur = r + d * NK * n
                if j % run == 0:
                    prv = r + d * NK * jnp.maximum(n - 1, 0)
                    k_prev, v_prev = keys(prv), values(prv)
                    pen = jnp.where(n == 0, -jnp.inf, 0.0).astype(F32)
                else:
                    k_prev, v_prev, pen = k_cur, v_cur, None
                k_cur, v_cur = keys(cur), values(cur)
                units.append(dict(cur=cur, pen=pen, q=q_ref[idx(cur), :] * (HEAD_DIM ** -0.5),
                                  k2=jnp.concatenate([k_prev, k_cur], axis=0),
                                  v2=jnp.concatenate([v_prev, v_cur], axis=0)))
            for unit in units:
                pv, rs, mn = [], [], []
                for h in range(2):
                    qh = jnp.where(head0 if h == 0 else ~head0, unit["q"], 0.0).astype(BF16)
                    s = _dot_nt(qh, unit["k2"]) + bias_ref[di, h]
                    if unit["pen"] is not None:
                        s = s + jnp.where(prev_cols, unit["pen"], 0.0)
                    m_h = jnp.max(s, axis=-1, keepdims=True)
                    pv_rs = _dot(jnp.exp(s - m_h).astype(BF16), unit["v2"])
                    pv.append(pv_rs[:, 0:128])
                    rs.append(pv_rs[:, 128:256])
                    mn.append(m_h)
                m_refs[di][idx(unit["cur"]), :] = jnp.where(head0, mn[0], mn[1])
                l_refs[di][idx(unit["cur"]), :] = jnp.where(head0, rs[0], rs[1])
                a_refs[di][idx(unit["cur"]), :] = jnp.where(head0, pv[0], pv[1])
            return carry

        lax.fori_loop(0, T_P // NK // n_units, group, 0)

    def merge(bi, carry):
        rows = pl.ds(pl.multiple_of(bi * NK, NK), NK)
        ms = [m_ref[rows, :] for m_ref in m_refs]
        m = functools.reduce(jnp.maximum, ms)
        num = jnp.zeros((NK, 128), F32)
        den = jnp.zeros((NK, 128), F32)
        for m_i, l_ref, a_ref in zip(ms, l_refs, a_refs):
            w = jnp.exp(m_i - m)
            num = num + w * a_ref[rows, :]
            den = den + w * l_ref[rows, :]
        o_ref[rows, :] = num / den
        return carry

    lax.fori_loop(0, T_P // NK, merge, 0, unroll=2)


def _attn_prompt(z, bias):
    blk = lambda off: pl.BlockSpec((T_P, 128), lambda b, h, off=off: (b, off + h))
    return pl.pallas_call(
        _attn_prompt_kernel,
        grid=(B_P, H_A // 2),
        in_specs=[blk(0), blk(W_A // 128), blk(2 * W_A // 128),
                  pl.BlockSpec((len(DILATIONS), 2, NK, 2 * NK), lambda b, h: (0, h, 0, 0))],
        out_specs=pl.BlockSpec((T_P, 128), lambda b, h: (b, h)),
        out_shape=jax.ShapeDtypeStruct((N_PROMPT, W_A), F32),
        scratch_shapes=[pltpu.VMEM((T_P, 128), F32)] * (3 * len(DILATIONS)),
        compiler_params=_params(("arbitrary", "arbitrary")),
        name="attn_prompt",
    )(z, z, z, bias)


def _attn_sample_kernel(q_ref, kn_ref, vn_ref, ck_ref, cv_ref, bias_ref, o_ref, kf_ref, vf_ref):
    for f_ref, c_ref, n_ref in ((kf_ref, ck_ref, kn_ref), (vf_ref, cv_ref, vn_ref)):
        f_ref[0:BUF, :] = c_ref[...]
        f_ref[BUF:BUF + T_S, :] = n_ref[...]
        f_ref[BUF + T_S:KF_ROWS, :] = jnp.zeros((KF_ROWS - BUF - T_S, 128), F32)
    head0 = lax.broadcasted_iota(I32, (T_S, 128), 1) < HEAD_DIM
    q = q_ref[...] * (HEAD_DIM ** -0.5)
    q2 = jnp.concatenate([jnp.where(head0, q, 0.0), jnp.where(head0, 0.0, q)], axis=0).astype(BF16)
    bias2 = jnp.concatenate([bias_ref[0], bias_ref[1]], axis=0)
    s = _dot_nt(q2, kf_ref[...].astype(BF16)) + bias2
    m = jnp.max(s, axis=-1, keepdims=True)
    p = jnp.exp(s - m)
    l = jnp.sum(p, axis=-1, keepdims=True)
    o2 = _dot(p.astype(BF16), vf_ref[...].astype(BF16)) / l
    o_ref[...] = jnp.where(head0, o2[0:T_S], o2[T_S:2 * T_S])


def _attn_sample(z, cache_k, cache_v, layer, bias):
    row0 = N_PROMPT // T_S
    blk = lambda off: pl.BlockSpec((T_S, 128), lambda b, h, off=off: (row0 + b, off + h))
    cache = pl.BlockSpec((None, BUF, 128), lambda b, h: (layer * B_S + b, 0, h))
    return pl.pallas_call(
        _attn_sample_kernel,
        grid=(B_S, H_A // 2),
        in_specs=[blk(0), blk(W_A // 128), blk(2 * W_A // 128), cache, cache,
                  pl.BlockSpec((2, T_S, KF_ROWS), lambda b, h: (h, 0, 0))],
        out_specs=pl.BlockSpec((T_S, 128), lambda b, h: (b, h)),
        out_shape=jax.ShapeDtypeStruct((N_SAMPLE, W_A), F32),
        scratch_shapes=[pltpu.VMEM((KF_ROWS, 128), F32)] * 2,
        compiler_params=_params(("arbitrary", "arbitrary")),
        name="attn_sample",
    )(z, z, z, cache_k, cache_v, bias)


def _gmlp_kernel(ub_ref, vb_ref, gbv_ref, bbv_ref, ws_ref, bs_ref, gmix_ref, bo_ref, vn_ref):
    v = jax.nn.gelu(vb_ref[...])
    mu = jnp.mean(v, axis=-1, keepdims=True)
    var = jnp.mean(jnp.square(v - mu), axis=-1, keepdims=True)
    vn = (v - mu) * lax.rsqrt(var + EPS) * gbv_ref[...] + bbv_ref[...]
    vn_ref[...] = vn
    vnb = vn.astype(BF16)
    causal = (lax.broadcasted_iota(I32, (CHUNK, CHUNK), 1) <= lax.broadcasted_iota(I32, (CHUNK, CHUNK), 0))
    first = lax.broadcasted_iota(I32, (CHUNK, 128), 1) < W_B // G_B
    parts = []
    for gp in range(G_B // 2):
        v2 = vnb[:, gp * 128:(gp + 1) * 128]
        m0 = _dot(jnp.where(causal, ws_ref[2 * gp], 0.0).astype(BF16), v2)
        m1 = _dot(jnp.where(causal, ws_ref[2 * gp + 1], 0.0).astype(BF16), v2)
        parts.append(jnp.where(first, m0, m1))
    mix = jnp.concatenate(parts, axis=1) + bs_ref[...]
    bo_ref[...] = _rms(jax.nn.gelu(ub_ref[...]) * mix, gmix_ref[...])


def _gmlp(z, n_chunks, col_u, col_v, g_bv, b_bv, w_s, b_s, g_mix_b):
    bs_wide = jnp.repeat(b_s.T, W_B // G_B, axis=1)
    vec = pl.BlockSpec((1, W_B), lambda i: (0, 0))
    out = pl.BlockSpec((CHUNK, W_B), lambda i: (i, 0))
    return pl.pallas_call(
        _gmlp_kernel,
        grid=(n_chunks,),
        in_specs=[pl.BlockSpec((CHUNK, W_B), lambda i: (i, col_u)),
                  pl.BlockSpec((CHUNK, W_B), lambda i: (i, col_v)),
                  vec, vec,
                  pl.BlockSpec((G_B, CHUNK, CHUNK), lambda i: (0, 0, 0)),
                  pl.BlockSpec((CHUNK, W_B), lambda i: (0, 0)),
                  vec],
        out_specs=[out, out],
        out_shape=[jax.ShapeDtypeStruct((n_chunks * CHUNK, W_B), F32)] * 2,
        compiler_params=_params(("arbitrary",)),
        name="gmlp",
    )(z, z, g_bv.reshape(1, W_B), b_bv.reshape(1, W_B), w_s, bs_wide, g_mix_b.reshape(1, W_B))


def _cmul(a, b):
    return a[0] * b[0] - a[1] * b[1], a[0] * b[1] + a[1] * b[0]


def _s5_kernel(u_ref, lr_ref, li_ref, ldt_ref, btr_ref, bti_ref, cr_ref, ci_ref, d_ref, h0r_ref, h0i_ref,
               y_ref, hr_ref, hi_ref, er_ref, ei_ref, gr_ref, gi_ref, *, L, nj, nsc, nb):
    hi = lax.Precision.HIGHEST
    lr, li = lr_ref[...], li_ref[...]
    dt = jnp.exp(ldt_ref[...])
    mag = jnp.exp(lr * dt)
    a = (mag * jnp.cos(li * dt), mag * jnp.sin(li * dt))
    den = lr * lr + li * li
    q = (((a[0] - 1.0) * lr + a[1] * li) / den, (a[1] * lr - (a[0] - 1.0) * li) / den)
    same = ((lax.broadcasted_iota(I32, (128, S5_SW), 0) >> int(math.log2(SSM_GROUP)))
            == (lax.broadcasted_iota(I32, (128, S5_SW), 1) >> int(math.log2(SSM_STATE))))
    bb = _cmul(q, (jnp.where(same, btr_ref[...], 0.0), jnp.where(same, bti_ref[...], 0.0)))
    c = (jnp.where(same, cr_ref[...], 0.0), jnp.where(same, ci_ref[...], 0.0))
    pw = [(jnp.ones_like(lr), jnp.zeros_like(lr))]
    for _ in range(L):
        pw.append(_cmul(pw[-1], a))
    ck = [_cmul(c, p) for p in pw]
    kbd = [(_dot_nt(bb[0], ck[k][0], hi) - _dot_nt(bb[1], ck[k][1], hi)).astype(BF16) for k in range(L)]
    mins = [tuple(x.astype(BF16) for x in _cmul(pw[L - 1 - s], bb)) for s in range(L)]
    cob = [tuple(x.astype(BF16) for x in ck[i + 1]) for i in range(L)]

    us = [u_ref[:, s * 128:(s + 1) * 128] for s in range(L)]
    ub = [x.astype(BF16) for x in us]
    sr = _dot(ub[0], mins[0][0])
    si = _dot(ub[0], mins[0][1])
    for s in range(1, L):
        sr = sr + _dot(ub[s], mins[s][0])
        si = si + _dot(ub[s], mins[s][1])
    al = pw[L]
    r = nsc * nb
    xl = [(sr[0:r], si[0:r])]
    for j in range(1, nj):
        t = _cmul(al, xl[-1])
        xl.append((t[0] + sr[j * r:(j + 1) * r], t[1] + si[j * r:(j + 1) * r]))
    alp = [al]
    for j in range(1, nj):
        alp.append(_cmul(alp[-1], al))
    er_ref[...] = xl[-1][0]
    ei_ref[...] = xl[-1][1]
    g = (h0r_ref[...], h0i_ref[...])
    for sc in range(nsc):
        gr_ref[sc * nb:(sc + 1) * nb, :] = g[0]
        gi_ref[sc * nb:(sc + 1) * nb, :] = g[1]
        t = _cmul(alp[nj - 1], g)
        g = (t[0] + er_ref[sc * nb:(sc + 1) * nb, :], t[1] + ei_ref[sc * nb:(sc + 1) * nb, :])
    hr_ref[...] = g[0]
    hi_ref[...] = g[1]
    gcat = (gr_ref[...], gi_ref[...])
    hprev = [gcat]
    for j in range(1, nj):
        t = _cmul(alp[j - 1], gcat)
        hprev.append((xl[j - 1][0] + t[0], xl[j - 1][1] + t[1]))
    hpr = jnp.concatenate([x[0] for x in hprev], axis=0).astype(BF16)
    hpi = jnp.concatenate([x[1] for x in hprev], axis=0).astype(BF16)
    y = (jnp.concatenate([d_ref[...]] * L, axis=1) * u_ref[...]
         + _dot_nt(hpr, jnp.concatenate([x[0] for x in cob], axis=0))
         - _dot_nt(hpi, jnp.concatenate([x[1] for x in cob], axis=0)))
    for s in range(L):
        conv = _dot(ub[s], jnp.concatenate(kbd[0:L - s], axis=1))
        y = y + conv if s == 0 else jnp.concatenate([y[:, :s * 128], y[:, s * 128:] + conv], axis=1)
    y_ref[...] = y


def _s5(u, lw, h0r, h0i, L, nj, nsc, nb):
    m, n = u.shape[1], u.shape[2]
    lanes = lambda x: x.reshape(S5_NLB, 1, S5_SW)
    rep = lambda x: jnp.tile(x.reshape(S5_NLB, 128, SSM_STATE), (1, 1, S5_GPB))
    vec = pl.BlockSpec((None, 1, S5_SW), lambda g: (g, 0, 0))
    mat = pl.BlockSpec((None, 128, S5_SW), lambda g: (g, 0, 0))
    st = pl.BlockSpec((None, nb, S5_SW), lambda g: (g, 0, 0))
    return pl.pallas_call(
        functools.partial(_s5_kernel, L=L, nj=nj, nsc=nsc, nb=nb),
        grid=(S5_NLB,),
        in_specs=[pl.BlockSpec((None, m, n), lambda g: (g, 0, 0)), vec, vec, vec, mat, mat, mat, mat,
                  pl.BlockSpec((None, 1, 128), lambda g: (g, 0, 0)), st, st],
        out_specs=[pl.BlockSpec((None, m, n), lambda g: (g, 0, 0)), st, st],
        out_shape=[jax.ShapeDtypeStruct((S5_NLB, m, n), F32)] + [jax.ShapeDtypeStruct((S5_NLB, nb, S5_SW), F32)] * 2,
        scratch_shapes=[pltpu.VMEM((nsc * nb, S5_SW), F32)] * 4,
        compiler_params=_params(("arbitrary",)),
        name="s5",
    )(u, lanes(lw["lam_re"]), lanes(lw["lam_im"]), lanes(jnp.repeat(lw["log_dt"], SSM_STATE)),
      rep(lw["ssm_b_re"].transpose(0, 2, 1)), rep(lw["ssm_b_im"].transpose(0, 2, 1)),
      rep(lw["ssm_c_re"]), rep(lw["ssm_c_im"]), lw["ssm_d"].reshape(S5_NLB, 1, 128), h0r, h0i)


def _s5_post_kernel(y_ref, w_ref, b_ref, g_ref, o_ref):
    zz = jax.nn.gelu(y_ref[...])
    out = zz * jax.nn.sigmoid(_dot(zz.astype(BF16), w_ref[...]) + b_ref[...])
    o_ref[...] = _rms(out, g_ref[...])


def _s5_post(y, w_glu, b_glu, g_mix_c):
    n = y.shape[0]
    tt = min(n, TM)
    vec = pl.BlockSpec((1, W_C), lambda i: (0, 0))
    return pl.pallas_call(
        _s5_post_kernel,
        grid=(n // tt,),
        in_specs=[pl.BlockSpec((tt, W_C), lambda i: (i, 0)), pl.BlockSpec((W_C, W_C), lambda i: (0, 0)), vec, vec],
        out_specs=pl.BlockSpec((tt, W_C), lambda i: (i, 0)),
        out_shape=jax.ShapeDtypeStruct((n, W_C), F32),
        compiler_params=_params(("arbitrary",)),
        name="s5_post",
    )(y, w_glu.astype(BF16), b_glu.reshape(1, W_C), g_mix_c.reshape(1, W_C))


def _router_kernel(h_ref, w_ref, b_ref, ind_ref, wd_ref, cnt_ref):
    i = pl.program_id(0)
    scores = jax.nn.sigmoid(_dot(h_ref[...], w_ref[...]))
    lane = lax.broadcasted_iota(I32, (TD, N_EXPERTS), 1).astype(F32)
    row = lax.broadcasted_iota(I32, (TD, N_EXPERTS), 0) + i * TD
    sel = scores + b_ref[...]
    ind = jnp.zeros((TD, N_EXPERTS), F32)
    for _ in range(TOP_K):
        m = jnp.max(sel, axis=-1, keepdims=True)
        idx = jnp.min(jnp.where(sel == m, lane, float(N_EXPERTS)), axis=-1, keepdims=True)
        hit = lane == idx
        ind = jnp.where(hit, 1.0, ind)
        sel = jnp.where(hit, -jnp.inf, sel)
    ind = jnp.where(row < N_REAL, ind, 0.0)
    wsel = ind * scores
    wd_ref[...] = wsel / jnp.sum(wsel + (1.0 - jnp.max(ind, axis=-1, keepdims=True)) / N_EXPERTS,
                                 axis=-1, keepdims=True) * ROUTE_SCALE
    ind_ref[...] = ind

    @pl.when(i == 0)
    def _():
        cnt_ref[...] = jnp.zeros((8, N_EXPERTS), F32)

    cnt_ref[...] += jnp.sum(ind, axis=0, keepdims=True)


def _router(h2b, w_router, b_router):
    til = pl.BlockSpec((TD, N_EXPERTS), lambda i: (i, 0))
    return pl.pallas_call(
        _router_kernel,
        grid=(NP // TD,),
        in_specs=[pl.BlockSpec((TD, D), lambda i: (i, 0)), pl.BlockSpec((D, N_EXPERTS), lambda i: (0, 0)),
                  pl.BlockSpec((1, N_EXPERTS), lambda i: (0, 0))],
        out_specs=[til, til, pl.BlockSpec((8, N_EXPERTS), lambda i: (0, 0))],
        out_shape=[jax.ShapeDtypeStruct((NP, N_EXPERTS), F32)] * 2 + [jax.ShapeDtypeStruct((8, N_EXPERTS), F32)],
        compiler_params=_params(("arbitrary",)),
        name="router",
    )(h2b, w_router.astype(BF16), b_router.reshape(1, N_EXPERTS))


def _group_starts(cnt):
    padded = jnp.floor((cnt + (BM - 1)) / BM) * BM
    tri = (lax.broadcasted_iota(I32, (N_EXPERTS, N_EXPERTS), 0)
           <= lax.broadcasted_iota(I32, (N_EXPERTS, N_EXPERTS), 1)).astype(F32)
    pend = jnp.dot(padded, tri, preferred_element_type=F32, precision=lax.Precision.HIGHEST)
    return padded, pend - padded, pend


def _plan_kernel(cnt_ref, be_ref, misc_ref):
    cnt = cnt_ref[...]
    _, pstart, pend = _group_starts(cnt)
    rowi = (lax.broadcasted_iota(I32, (NBP, N_EXPERTS), 0) * BM).astype(F32)
    be = jnp.minimum(jnp.sum(jnp.where(pend[0:1] <= rowi, 1.0, 0.0), axis=-1, keepdims=True), N_EXPERTS - 1.0)
    be_ref[...] = jnp.broadcast_to(be, (NBP, 128)).astype(I32)
    misc_ref[0:8, :] = pstart.astype(I32)
    misc_ref[8:16, :] = cnt.astype(I32)
    misc_ref[16:24, :] = (pend / BM).astype(I32)


def _plan(cnt):
    return pl.pallas_call(
        _plan_kernel,
        out_shape=[jax.ShapeDtypeStruct((NBP, 128), I32), jax.ShapeDtypeStruct((24, N_EXPERTS), I32)],
        name="moe_plan",
    )(cnt)


def _slots_kernel(ind_ref, wd_ref, cnt_ref, dest_ref, w_ref, run_ref):
    @pl.when(pl.program_id(0) == 0)
    def _():
        run_ref[...] = jnp.zeros((8, N_EXPERTS), F32)

    ind = ind_ref[...]
    _, pstart, _ = _group_starts(cnt_ref[...])
    ltri = (lax.broadcasted_iota(I32, (TD, TD), 1) < lax.broadcasted_iota(I32, (TD, TD), 0)).astype(BF16)
    pos = _dot(ltri, ind.astype(BF16)) + run_ref[0:1, :]
    run_ref[...] += jnp.sum(ind, axis=0, keepdims=True)
    cur = jnp.where(ind > 0.0, pstart[0:1] + pos, BIG)
    lane = lax.broadcasted_iota(I32, (TD, 128), 1)
    dest = jnp.full((TD, 128), -1.0, F32)
    wts = jnp.zeros((TD, 128), F32)
    wd = wd_ref[...]
    for k in range(TOP_K):
        m = jnp.min(cur, axis=-1, keepdims=True)
        hit = cur == m
        wk = jnp.sum(jnp.where(hit, wd, 0.0), axis=-1, keepdims=True)
        dest = jnp.where(lane == k, jnp.where(m < BIG, m, -1.0), dest)
        wts = jnp.where(lane == k, jnp.where(m < BIG, wk, 0.0), wts)
        cur = jnp.where(hit, BIG, cur)
    dest_ref[...] = dest.astype(I32)
    w_ref[...] = wts


def _slots(ind, wd, cnt):
    til = pl.BlockSpec((TD, N_EXPERTS), lambda i: (i, 0))
    out = pl.BlockSpec((TD, 128), lambda i: (i, 0))
    return pl.pallas_call(
        _slots_kernel,
        grid=(NP // TD,),
        in_specs=[til, til, pl.BlockSpec((8, N_EXPERTS), lambda i: (0, 0))],
        out_specs=[out, out],
        out_shape=[jax.ShapeDtypeStruct((NP, 128), I32), jax.ShapeDtypeStruct((NP, 128), F32)],
        scratch_shapes=[pltpu.VMEM((8, N_EXPERTS), F32)],
        compiler_params=_params(("arbitrary",)),
        name="moe_slots",
    )(ind, wd, cnt)


def _dispatch_kernel(ps_ref, cnt_ref, nbu_ref, dest_ref, x_ref, xs_ref, zero_ref, sem):
    i = pl.program_id(0)

    def token(ref, t):
        start = t * ROW_TILES if isinstance(t, int) else pl.multiple_of(t * ROW_TILES, ROW_TILES)
        return ref.at[pl.ds(start, ROW_TILES), :]

    def row_copy(r, k):
        return pltpu.make_async_copy(token(x_ref, r), token(xs_ref, dest_ref[r * 8 + k]), sem)

    def zero_rows(d, n):
        rows = n * ROW_TILES
        return pltpu.make_async_copy(zero_ref.at[pl.ds(0, rows), :],
                                     xs_ref.at[pl.ds(pl.multiple_of(d * ROW_TILES, ROW_TILES), rows), :], sem)

    def zero_block(b):
        rows = BM * ROW_TILES
        return pltpu.make_async_copy(zero_ref, xs_ref.at[pl.ds(pl.multiple_of(b * rows, rows), rows), :], sem)

    @pl.when(i < N_DTILES)
    def _():
        n_rows = jnp.minimum(TD, N_REAL - i * TD)

        def issue(r, c):
            for k in range(TOP_K):
                row_copy(r, k).start(priority=k % 2)
            return c

        def drain(r, c):
            for k in range(TOP_K):
                row_copy(r, k).wait()
            return c

        lax.fori_loop(0, n_rows, issue, 0)
        lax.fori_loop(0, n_rows, drain, 0)

    @pl.when(i == N_DTILES)
    def _():
        zero_ref[...] = jnp.zeros(zero_ref.shape, F32)

        def per_expert(e, c):
            cnt = cnt_ref[e]
            base = ps_ref[e] + cnt
            rem = (((cnt + (BM - 1)) >> BM_SHIFT) << BM_SHIFT) - cnt
            for start in (True, False):
                for k in range(BM_SHIFT):
                    cp = zero_rows(base + ((rem >> (k + 1)) << (k + 1)), 1 << k)

                    @pl.when((rem & (1 << k)) != 0)
                    def _():
                        cp.start() if start else cp.wait()
            return c

        lax.fori_loop(0, N_EXPERTS, per_expert, 0)
        lax.fori_loop(nbu_ref[0], NB, lambda b, c: (zero_block(b).start(), c)[1], 0)
        lax.fori_loop(nbu_ref[0], NB, lambda b, c: (zero_block(b).wait(), c)[1], 0)


def _dispatch(pstart_i, cnt_i, nbu, dest_flat, x):
    return pl.pallas_call(
        _dispatch_kernel,
        grid_spec=pltpu.PrefetchScalarGridSpec(
            num_scalar_prefetch=3, grid=(N_DTILES + 1,),
            in_specs=[pl.BlockSpec((TD * 8,), lambda i, *_: (jnp.minimum(i, N_DTILES - 1),),
                                   memory_space=pltpu.SMEM),
                      pl.BlockSpec((TD * ROW_TILES, 128), lambda i, *_: (jnp.minimum(i, N_DTILES - 1), 0))],
            out_specs=pl.BlockSpec(memory_space=pl.ANY),
            scratch_shapes=[pltpu.VMEM((BM * ROW_TILES, 128), F32), pltpu.SemaphoreType.DMA(())]),
        out_shape=jax.ShapeDtypeStruct((CAP * ROW_TILES, 128), F32),
        compiler_params=_params(("arbitrary",)),
        name="moe_dispatch",
    )(pstart_i, cnt_i, nbu, dest_flat, x)


def _expert_kernel(be_ref, nbu_ref, gend_ref, x_ref, wg_hbm, wu_hbm, wd_hbm, o_ref,
                   wg_ref, wu_ref, wd_ref, slot_ref, sem, *, layer):
    i = pl.program_id(0)
    nbu = nbu_ref[0]

    def weight_copies(e, s):
        return [pltpu.make_async_copy(hbm.at[layer, e], buf.at[s], sem.at[s, j])
                for j, (hbm, buf) in enumerate(((wg_hbm, wg_ref), (wu_hbm, wu_ref), (wd_hbm, wd_ref)))]

    @pl.when(i < nbu)
    def _():
        e = be_ref[i]

        @pl.when(i == 0)
        def _():
            slot_ref[0] = 0
            for cp in weight_copies(e, 0):
                cp.start()

        new_group = jnp.logical_or(i == 0, be_ref[jnp.maximum(i - 1, 0)] != e)

        @pl.when(jnp.logical_and(new_group, i > 0))
        def _():
            slot_ref[0] = 1 - slot_ref[0]

        s = slot_ref[0]

        @pl.when(new_group)
        def _():
            for cp in weight_copies(e, s):
                cp.wait()
            nxt = gend_ref[e]

            @pl.when(nxt < nbu)
            def _():
                for cp in weight_copies(be_ref[jnp.minimum(nxt, NB - 1)], 1 - s):
                    cp.start()

        x = _load_token_rows(x_ref, BM).astype(BF16)
        g = _dot(x, wg_ref[s].astype(BF16))
        u = _dot(x, wu_ref[s].astype(BF16))
        a = (g * jax.nn.sigmoid(g) * u).astype(BF16)
        _store_token_rows(o_ref, _dot(a, wd_ref[s].astype(BF16)))

    @pl.when(i >= nbu)
    def _():
        o_ref[...] = jnp.zeros(o_ref.shape, F32)


def _experts(layer, be, nbu, group_end, xs, w_gate, w_up, w_down):
    any_spec = pl.BlockSpec(memory_space=pl.ANY)
    return pl.pallas_call(
        functools.partial(_expert_kernel, layer=layer),
        grid_spec=pltpu.PrefetchScalarGridSpec(
            num_scalar_prefetch=3, grid=(NB,),
            in_specs=[pl.BlockSpec((BM * ROW_TILES, 128), lambda i, be, nbu, ge: (jnp.minimum(i, nbu[0] - 1), 0)),
                      any_spec, any_spec, any_spec],
            out_specs=pl.BlockSpec((BM * ROW_TILES, 128), lambda i, be, nbu, ge: (i, 0)),
            scratch_shapes=[pltpu.VMEM((2, D, D_EXPERT), F32), pltpu.VMEM((2, D, D_EXPERT), F32),
                            pltpu.VMEM((2, D_EXPERT, D), F32), pltpu.SMEM((1,), I32),
                            pltpu.SemaphoreType.DMA((2, 3))]),
        out_shape=jax.ShapeDtypeStruct((CAP * ROW_TILES, 128), F32),
        compiler_params=_params(("arbitrary",)),
        name="moe_experts",
    )(be, nbu, group_end, xs, w_gate, w_up, w_down)


def _shared_kernel(h_ref, wg_ref, wu_ref, wd_ref, o_ref):
    h = h_ref[...]
    g = _dot(h, wg_ref[...])
    u = _dot(h, wu_ref[...])
    o_ref[...] = _dot((g * jax.nn.sigmoid(g) * u).astype(BF16), wd_ref[...])


def _shared(h2b, wg, wu, wd):
    return pl.pallas_call(
        _shared_kernel,
        grid=(NT,),
        in_specs=[pl.BlockSpec((TM, D), lambda i: (i, 0)),
                  pl.BlockSpec((D, D_SHARED), lambda i: (0, 0)),
                  pl.BlockSpec((D, D_SHARED), lambda i: (0, 0)),
                  pl.BlockSpec((D_SHARED, D), lambda i: (0, 0))],
        out_specs=pl.BlockSpec((TM, D), lambda i: (i, 0)),
        out_shape=jax.ShapeDtypeStruct((NP, D), F32),
        compiler_params=_params(("arbitrary",)),
        name="moe_shared",
    )(h2b, wg.astype(BF16), wu.astype(BF16), wd.astype(BF16))


def _combine_kernel(dest_ref, nxt_ref, w_ref, ys_ref, sh_ref, x_ref, gr_ref, gt_ref, o_ref, rows_ref, acc_ref, sem):
    i = pl.program_id(0)
    n_tiles = NP // TC_
    slot = i % 2

    def gather(d_ref, s, start):
        def body(r, c):
            for k in range(TOP_K):
                d = jnp.maximum(d_ref[r * 8 + k], 0)
                cp = pltpu.make_async_copy(
                    ys_ref.at[pl.ds(pl.multiple_of(d * ROW_TILES, ROW_TILES), ROW_TILES), :],
                    rows_ref.at[s, k, pl.ds(pl.multiple_of(r * ROW_TILES, ROW_TILES), ROW_TILES), :], sem.at[s])
                if start:
                    cp.start(priority=k % 2)
                else:
                    cp.wait()
            return c

        lax.fori_loop(0, TC_, body, 0)

    @pl.when(i == 0)
    def _():
        gather(dest_ref, 0, True)

    @pl.when(i + 1 < n_tiles)
    def _():
        gather(nxt_ref, 1 - slot, True)

    gather(dest_ref, slot, False)
    def weigh(t, c):
        rows = pl.ds(pl.multiple_of(t * ROW_TILES, ROW_TILES), ROW_TILES)
        acc = w_ref[t * 8] * rows_ref[slot, 0, rows, :]
        for k in range(1, TOP_K):
            acc = acc + w_ref[t * 8 + k] * rows_ref[slot, k, rows, :]
        acc_ref[rows, :] = acc
        return c

    lax.fori_loop(0, TC_, weigh, 0, unroll=4)
    routed = _load_token_rows(acc_ref, TC_)
    last = i >= N_PROMPT // TC_
    o_ref[...] = x_ref[...] + _pick(last, gt_ref, gr_ref) * (routed + sh_ref[...])


def _combine(dest_flat, w6, ys, shared, x, gate):
    n_tiles = NP // TC_
    n_p = N_PROMPT // TC_
    per = TM // TC_
    til = pl.BlockSpec((TC_, D), lambda i: (i, 0))
    return pl.pallas_call(
        _combine_kernel,
        grid=(n_tiles,),
        in_specs=[pl.BlockSpec((TC_ * 8,), lambda i: (i,), memory_space=pltpu.SMEM),
                  pl.BlockSpec((TC_ * 8,), lambda i: (jnp.minimum(i + 1, n_tiles - 1),), memory_space=pltpu.SMEM),
                  pl.BlockSpec((TC_ * 8,), lambda i: (i,), memory_space=pltpu.SMEM),
                  pl.BlockSpec(memory_space=pl.ANY), til, til,
                  pl.BlockSpec((None, 1, D), lambda i: (i // per, 0, 0)),
                  pl.BlockSpec((TC_, D), lambda i: (jnp.maximum(i - n_p, 0), 0))],
        out_specs=til,
        out_shape=jax.ShapeDtypeStruct((NP, D), F32),
        scratch_shapes=[pltpu.VMEM((2, TOP_K, TC_ * ROW_TILES, 128), F32), pltpu.VMEM((TC_ * ROW_TILES, 128), F32),
                        pltpu.SemaphoreType.DMA((2,))],
        compiler_params=_params(("arbitrary",)),
        name="moe_combine",
    )(dest_flat, dest_flat, w6, ys, shared, x, gate[0], gate[1])


def _pad_rows(a, rows):
    return jnp.concatenate([a, jnp.zeros((rows - a.shape[0],) + a.shape[1:], a.dtype)], axis=0)


def _s5_layer(z, lw, h0r_s, h0i_s):
    g_mix_c = lw["g_mix"][W_A + W_B:]
    nj = nsc = 16
    state = lambda h, b: h.reshape(S5_NLB, b, S5_GPB, SSM_STATE).transpose(1, 0, 2, 3).reshape(b, G_C, SSM_STATE)
    uc_p = z[:N_PROMPT, IN_COLS - W_C:]
    u_p = uc_p.reshape(B_P, nsc, nj, S5_L_P, S5_NLB, 128).transpose(4, 2, 1, 0, 3, 5)
    u_p = u_p.reshape(S5_NLB, nj * nsc * B_P, S5_L_P * 128)
    zeros_h = jnp.zeros((S5_NLB, B_P, S5_SW), F32)
    y_p, hr_p, hi_p = _s5(u_p, lw, zeros_h, zeros_h, S5_L_P, nj, nsc, B_P)
    y_p = y_p.reshape(S5_NLB, nj, nsc, B_P, S5_L_P, 128).transpose(3, 2, 1, 4, 0, 5).reshape(N_PROMPT, W_C)
    c_p = _s5_post(y_p, lw["w_glu"], lw["b_glu"], g_mix_c)

    uc_s = z[N_PROMPT:N_REAL, IN_COLS - W_C:]
    u_s = uc_s.reshape(B_S, T_S, S5_NLB, 128).transpose(2, 0, 1, 3).reshape(S5_NLB, B_S, S5_L_S * 128)
    h0 = lambda h: h.reshape(B_S, S5_NLB, S5_SW).transpose(1, 0, 2)
    y_s, hr_s, hi_s = _s5(u_s, lw, h0(h0r_s), h0(h0i_s), S5_L_S, 1, 1, B_S)
    y_s = y_s.reshape(S5_NLB, B_S, T_S, 128).transpose(1, 2, 0, 3).reshape(N_SAMPLE, W_C)
    c_s = _s5_post(y_s, lw["w_glu"], lw["b_glu"], g_mix_c)
    return c_p, c_s, state(hr_p, B_P), state(hi_p, B_P), state(hr_s, B_S), state(hi_s, B_S)


def _layer(x, l, lw, experts, mod, bias_p, bias_s, cache_k, cache_v, h0r_s, h0i_s):
    sh1, sc1, g1, sh2, sc2, g2 = [_tile_tabs(m) for m in jnp.split(mod, 6, axis=-1)]
    z = _matmul_norm(x, lw["g_attn"], sc1, sh1, lw["w_in"].astype(BF16), 1536)

    a_p = _attn_prompt(z, bias_p)
    a_s = _attn_sample(z, cache_k, cache_v, l, bias_s)

    g_mix_b = lw["g_mix"][W_A:W_A + W_B]
    gm = (lw["g_bv"], lw["b_bv"], lw["w_s"], lw["b_s"], g_mix_b)
    b_p, _ = _gmlp(z, N_PROMPT // CHUNK, 3 * W_A // W_B, 3 * W_A // W_B + 1, *gm)
    uv_s = z[N_PROMPT:N_REAL, 3 * W_A:3 * W_A + 2 * W_B].reshape(B_S, T_S, 2 * W_B)
    uv_s = jnp.pad(uv_s, ((0, 0), (0, CHUNK - T_S), (0, 0))).reshape(B_S * CHUNK, 2 * W_B)
    b_s, vn_s = _gmlp(uv_s, B_S, 0, 1, *gm)
    b_s = b_s.reshape(B_S, CHUNK, W_B)[:, :T_S].reshape(N_SAMPLE, W_B)
    new_vb = vn_s.reshape(B_S, CHUNK, W_B)[:, :T_S]

    c_p, c_s, hr_p, hi_p, hr_s, hi_s = _s5_layer(z, lw, h0r_s, h0i_s)

    x = _matmul_mix_residual(a_p, b_p, c_p, _pad_rows(a_s, TM), _pad_rows(b_s, TM), _pad_rows(c_s, TM),
                             lw["g_mix"][:W_A], lw["w_out"].astype(BF16), x, g1, 1024)

    h2b, h2 = _modnorm(x, lw["g_ffn"], sc2, sh2, True)
    ind, wd, cnt = _router(h2b, lw["w_router"], lw["b_router"])
    be, misc = _plan(cnt)
    dest, w6 = _slots(ind, wd, cnt)
    dest_flat = dest[:, :8].reshape(-1)
    xs = _dispatch(misc[0], misc[8], misc[16, N_EXPERTS - 1:], dest_flat, h2)
    ys = _experts(l, be[:NB, 0], misc[16, N_EXPERTS - 1:], misc[16], xs, *experts)
    shared = _shared(h2b, lw["w_sh_gate"], lw["w_sh_up"], lw["w_sh_down"])
    x = _combine(dest_flat, w6[:, :8].reshape(-1), ys, shared, x, g2)

    kv = lambda lo: (z[:N_PROMPT, lo:lo + W_A].reshape(B_P, T_P, H_A, HEAD_DIM)[:, T_P - BUF:],
                     z[N_PROMPT:N_REAL, lo:lo + W_A].reshape(B_S, T_S, H_A, HEAD_DIM))
    (k_p, k_s), (v_p, v_s) = kv(W_A), kv(2 * W_A)
    return x, (k_p, v_p, k_s, v_s, new_vb, hr_p, hi_p, hr_s, hi_s)


def kernel(x_prompt, x_sample, cache_a_k, cache_a_v, state_c_re, state_c_im, c_prompt, c_sample, rel_bias, w_ada, b_ada, g_attn, w_in, g_bv, b_bv, w_s, b_s, lam_re, lam_im, log_dt, ssm_b_re, ssm_b_im, ssm_c_re, ssm_c_im, ssm_d, w_glu, b_glu, g_mix, w_out, g_ffn, w_router, b_router, w_gate, w_up, w_down, w_sh_gate, w_sh_up, w_sh_down, g_final):
    assert x_prompt.shape == (B_P, T_P, D) and x_sample.shape == (B_S, T_S, D)
    assert cache_a_k.shape == (DEPTH, B_S, BUF, H_A, HEAD_DIM)
    per_layer = dict(g_attn=g_attn, w_in=w_in, g_bv=g_bv, b_bv=b_bv, w_s=w_s, b_s=b_s, lam_re=lam_re,
                     lam_im=lam_im, log_dt=log_dt, ssm_b_re=ssm_b_re, ssm_b_im=ssm_b_im, ssm_c_re=ssm_c_re,
                     ssm_c_im=ssm_c_im, ssm_d=ssm_d, w_glu=w_glu, b_glu=b_glu, g_mix=g_mix, w_out=w_out,
                     g_ffn=g_ffn, w_router=w_router, b_router=b_router, w_sh_gate=w_sh_gate, w_sh_up=w_sh_up,
                     w_sh_down=w_sh_down)

    x = _pad_rows(jnp.concatenate([x_prompt.reshape(N_PROMPT, D), x_sample.reshape(N_SAMPLE, D)], axis=0), NP)
    c_all = _pad_rows(jnp.concatenate([c_prompt, c_sample], axis=0), 16)
    mod = _mod_all(c_all, w_ada, b_ada)
    bias_p = _bias_bank(rel_bias, *_prompt_bias_tables())
    bias_s = _bias_bank(rel_bias, *_sample_bias_tables())[0]
    cache_k = cache_a_k.reshape(DEPTH * B_S, BUF, W_A)
    cache_v = cache_a_v.reshape(DEPTH * B_S, BUF, W_A)

    outs = []
    for l in range(DEPTH):
        lw = {name: w[l] for name, w in per_layer.items()}
        x, o = _layer(x, l, lw, (w_gate, w_up, w_down), mod[l], bias_p, bias_s, cache_k, cache_v,
                      state_c_re[l], state_c_im[l])
        outs.append(o)
    y = _finalnorm(x, g_final)
    stacked = [jnp.stack([o[j] for o in outs]) for j in range(9)]
    return (y[:N_PROMPT].reshape(B_P, T_P, D), y[N_PROMPT:N_REAL].reshape(B_S, T_S, D), *stacked)
```

```python
import functools
import math

import jax
import jax.numpy as jnp
import numpy as np
from jax import lax
from jax.experimental import pallas as pl
from jax.experimental.pallas import tpu as pltpu

F32 = jnp.float32
BF16 = jnp.bfloat16
I32 = jnp.int32

D = 2048
B_P, T_P = 2, 4096
B_S, T_S = 8, 8
DEPTH = 2
HEAD_DIM = 64
W_A = 1024
H_A = 16
W_B = 512
G_B = 8
W_C = 512
SSM_GROUP = 16
G_C = 32
SSM_STATE = 64
IN_COLS = 3 * W_A + 2 * W_B + W_C
DILATIONS = (1, 4, 16)
NK = 128
ATTN_UNITS = (8, 8, 8)
BUF = 2048
CHUNK = 128
N_BUCKETS = 32
REL_MAX_DIST = 2048
N_EXPERTS = 64
TOP_K = 6
D_EXPERT = 512
D_SHARED = 1024
ROUTE_SCALE = 2.5
EPS = 1e-6

N_PROMPT = B_P * T_P
N_SAMPLE = B_S * T_S
N_REAL = N_PROMPT + N_SAMPLE
TM = 512
NT = N_PROMPT // TM + 1
NP = NT * TM
TD = 256
N_DTILES = -(-N_REAL // TD)
ROW_TILES = D // 128
BM = 256
BM_SHIFT = 8
NB = -(-(N_REAL * TOP_K + N_EXPERTS * (BM - 1)) // BM)
NBP = -(-NB // 8) * 8
CAP = NB * BM
TC_ = 128
KF_ROWS = 2176
S5_L_P = 16
S5_L_S = T_S
S5_GPB = 128 // SSM_GROUP
S5_NLB = G_C // S5_GPB
S5_SW = S5_GPB * SSM_STATE
BIG = 1e9
assert BM == 1 << BM_SHIFT

VMEM_LIMIT_BYTES = 56 * 1024 * 1024


def _params(dims):
    return pltpu.CompilerParams(dimension_semantics=dims, vmem_limit_bytes=VMEM_LIMIT_BYTES)


def _dot(a, b):
    return jnp.dot(a, b, preferred_element_type=F32)


def _dot_nt(a, b, precision=None):
    return lax.dot_general(a, b, (((1,), (1,)), ((), ())), preferred_element_type=F32,
                           precision=precision)


def _rms(x, g):
    return x * lax.rsqrt(jnp.mean(x * x, axis=-1, keepdims=True) + EPS) * g


def _mod_kernel(c_ref, w_ref, b_ref, o_ref):
    c = c_ref[...]
    s = (c * jax.nn.sigmoid(c)).astype(BF16)
    o_ref[...] = _dot(s, w_ref[...].astype(BF16)) + b_ref[...]


def _mod_all(c_all, w_ada, b_ada):
    tn = 1024
    return pl.pallas_call(
        _mod_kernel,
        grid=(DEPTH, 6 * D // tn),
        in_specs=[pl.BlockSpec((16, D), lambda l, j: (0, 0)),
                  pl.BlockSpec((None, D, tn), lambda l, j: (l, 0, j)),
                  pl.BlockSpec((None, 1, tn), lambda l, j: (l, 0, j))],
        out_specs=pl.BlockSpec((None, 16, tn), lambda l, j: (l, 0, j)),
        out_shape=jax.ShapeDtypeStruct((DEPTH, 16, 6 * D), F32),
        compiler_params=_params(("arbitrary", "arbitrary")),
        name="mod",
    )(c_all, w_ada, b_ada.reshape(DEPTH, 1, 6 * D))


def _tile_tabs(m):
    row = jnp.concatenate([jnp.repeat(m[0:B_P], NT // B_P, axis=0), jnp.zeros((1, m.shape[1]), F32)], 0)
    tok = jnp.concatenate([jnp.repeat(m[B_P:B_P + B_S], T_S, axis=0),
                           jnp.zeros((TM - N_SAMPLE, m.shape[1]), F32)], 0)
    return row[:, None, :], tok


def _pick(last, tok_ref, row_ref):
    return jnp.where(last, tok_ref[...], row_ref[...])


def _modnorm_kernel(x_ref, g_ref, scr_ref, sct_ref, shr_ref, sht_ref, ob_ref, ot_ref):
    last = pl.program_id(0) == NT - 1
    y = _rms(x_ref[...], g_ref[...])
    y = y * (1.0 + _pick(last, sct_ref, scr_ref)) + _pick(last, sht_ref, shr_ref)
    ob_ref[...] = y.astype(BF16)
    _store_token_rows(ot_ref, y)


def _load_token_rows(ref, n, first=0):
    return jnp.concatenate([ref[pl.ds(first * ROW_TILES + j, n, stride=ROW_TILES), :] for j in range(ROW_TILES)],
                           axis=1)


def _store_token_rows(ref, val, first=0):
    for j in range(ROW_TILES):
        ref[pl.ds(first * ROW_TILES + j, val.shape[0], stride=ROW_TILES), :] = val[:, j * 128:(j + 1) * 128]


def _modnorm(x, g, sc, sh):
    row = pl.BlockSpec((None, 1, D), lambda i: (i, 0, 0))
    tok = pl.BlockSpec((TM, D), lambda i: (0, 0))
    til = pl.BlockSpec((TM, D), lambda i: (i, 0))
    return pl.pallas_call(
        _modnorm_kernel,
        grid=(NT,),
        in_specs=[til, pl.BlockSpec((1, D), lambda i: (0, 0)), row, tok, row, tok],
        out_specs=[til, pl.BlockSpec((TM * ROW_TILES, 128), lambda i: (i, 0))],
        out_shape=[jax.ShapeDtypeStruct((NP, D), BF16), jax.ShapeDtypeStruct((NP * ROW_TILES, 128), F32)],
        compiler_params=_params(("arbitrary",)),
        name="modnorm",
    )(x, g.reshape(1, D), sc[0], sc[1], sh[0], sh[1])


def _finalnorm_kernel(x_ref, g_ref, op_ref, os_ref):
    i = pl.program_id(0)
    y = _rms(x_ref[...], g_ref[...])

    @pl.when(i < NT - 1)
    def _():
        op_ref[...] = y

    @pl.when(i == NT - 1)
    def _():
        os_ref[...] = y


def _finalnorm(x, g):
    return pl.pallas_call(
        _finalnorm_kernel,
        grid=(NT,),
        in_specs=[pl.BlockSpec((TM, D), lambda i: (i, 0)), pl.BlockSpec((1, D), lambda i: (0, 0))],
        out_specs=[pl.BlockSpec((TM, D), lambda i: (jnp.minimum(i, NT - 2), 0)),
                   pl.BlockSpec((TM, D), lambda i: (0, 0))],
        out_shape=[jax.ShapeDtypeStruct((N_PROMPT, D), F32), jax.ShapeDtypeStruct((TM, D), F32)],
        compiler_params=_params(("arbitrary",)),
        name="finalnorm",
    )(x, g.reshape(1, D))


def _mm_norm_kernel(x_ref, g_ref, scr_ref, sct_ref, shr_ref, sht_ref, b_ref, o_ref):
    last = pl.program_id(1) == NT - 1
    y = _rms(x_ref[...], g_ref[...])
    y = y * (1.0 + _pick(last, sct_ref, scr_ref)) + _pick(last, sht_ref, shr_ref)
    o_ref[...] = _dot(y.astype(BF16), b_ref[...])


def _matmul_norm(x, g, sc, sh, b, tn):
    k, n = b.shape
    row = pl.BlockSpec((None, 1, D), lambda c, i: (i, 0, 0))
    tok = pl.BlockSpec((TM, D), lambda c, i: (0, 0))
    return pl.pallas_call(
        _mm_norm_kernel,
        grid=(n // tn, NT),
        in_specs=[pl.BlockSpec((TM, D), lambda c, i: (i, 0)), pl.BlockSpec((1, D), lambda c, i: (0, 0)),
                  row, tok, row, tok, pl.BlockSpec((k, tn), lambda c, i: (0, c))],
        out_specs=pl.BlockSpec((TM, tn), lambda c, i: (i, c)),
        out_shape=jax.ShapeDtypeStruct((NP, n), F32),
        compiler_params=_params(("arbitrary", "arbitrary")),
        name="matmul_norm",
    )(x, g.reshape(1, D), sc[0], sc[1], sh[0], sh[1], b)


def _mm_mix_kernel(ap_ref, bp_ref, cp_ref, as_ref, bs_ref, cs_ref, gm_ref, w_ref, x_ref, gr_ref, gt_ref, o_ref):
    last = pl.program_id(1) == NT - 1
    a = jnp.where(last, as_ref[...], ap_ref[...])
    b = jnp.where(last, bs_ref[...], bp_ref[...])
    c = jnp.where(last, cs_ref[...], cp_ref[...])
    mixed = jnp.concatenate([_rms(a, gm_ref[...]).astype(BF16), b.astype(BF16), c.astype(BF16)], axis=1)
    o_ref[...] = x_ref[...] + _pick(last, gt_ref, gr_ref) * _dot(mixed, w_ref[...])


def _matmul_mix_residual(a_p, b_p, c_p, a_s, b_s, c_s, g_mix_a, w, x, gate, tn):
    pr = lambda wd: pl.BlockSpec((TM, wd), lambda c, i: (jnp.minimum(i, NT - 2), 0))
    sm = lambda wd: pl.BlockSpec((TM, wd), lambda c, i: (0, 0))
    return pl.pallas_call(
        _mm_mix_kernel,
        grid=(D // tn, NT),
        in_specs=[pr(W_A), pr(W_B), pr(W_C), sm(W_A), sm(W_B), sm(W_C), pl.BlockSpec((1, W_A), lambda c, i: (0, 0)),
                  pl.BlockSpec((D, tn), lambda c, i: (0, c)),
                  pl.BlockSpec((TM, tn), lambda c, i: (i, c)),
                  pl.BlockSpec((None, 1, tn), lambda c, i: (i, 0, c)),
                  pl.BlockSpec((TM, tn), lambda c, i: (0, c))],
        out_specs=pl.BlockSpec((TM, tn), lambda c, i: (i, c)),
        out_shape=jax.ShapeDtypeStruct((NP, D), F32),
        compiler_params=_params(("arbitrary", "arbitrary")),
        name="matmul_mix_residual",
    )(a_p, b_p, c_p, a_s, b_s, c_s, g_mix_a.reshape(1, W_A), w, x, gate[0], gate[1])


def _t5_bucket(dist):
    max_exact = N_BUCKETS // 2
    dist = np.maximum(dist, 0)
    ratio = (np.log(np.maximum(dist, 1).astype(np.float32) / np.float32(max_exact))
             / np.float32(math.log(REL_MAX_DIST / max_exact)))
    large = np.minimum(max_exact + (ratio * np.float32(N_BUCKETS - max_exact)).astype(np.int32), N_BUCKETS - 1)
    return np.where(dist < max_exact, dist, large)


def _bias_kernel(rb_ref, bucket_ref, add_ref, o_ref):
    h = pl.program_id(1)
    bucket = bucket_ref[...]
    acc = jnp.full(bucket.shape, -jnp.inf, F32)
    for b in range(N_BUCKETS):
        acc = jnp.where(bucket == b, rb_ref[b * H_A + h], acc)
    o_ref[...] = acc + add_ref[...]


def _bias_bank(rel_bias, bucket, add):
    n, r, c = bucket.shape
    blk = pl.BlockSpec((None, r, c), lambda i, h, rb: (i, 0, 0))
    return pl.pallas_call(
        _bias_kernel,
        grid_spec=pltpu.PrefetchScalarGridSpec(
            num_scalar_prefetch=1, grid=(n, H_A), in_specs=[blk, blk],
            out_specs=pl.BlockSpec((None, None, r, c), lambda i, h, rb: (i, h, 0, 0))),
        out_shape=jax.ShapeDtypeStruct((n, H_A, r, c), F32),
        compiler_params=_params(("arbitrary", "arbitrary")),
        name="bias_bank",
    )(rel_bias.reshape(-1), bucket, add)


def _prompt_bias_tables():
    qi = np.arange(NK)[:, None]
    kj = np.arange(2 * NK)[None, :]
    rel = qi + NK - kj
    band = (rel >= 0) & (rel <= NK)
    bucket = np.stack([np.where(band, _t5_bucket(rel * d), -1) for d in DILATIONS]).astype(np.int32)
    return jnp.asarray(bucket), jnp.zeros(bucket.shape, F32)


def _sample_bias_tables():
    dist_np = BUF + np.arange(T_S)[:, None] - np.arange(KF_ROWS)[None, :]
    mult = np.zeros(dist_np.shape, np.float64)
    for d in DILATIONS:
        mult += (dist_np >= 0) & (dist_np % d == 0) & (dist_np // d <= NK)
    mult = np.where(np.arange(KF_ROWS)[None, :] < BUF + T_S, mult, 0.0)
    valid = mult > 0
    add = np.where(valid, np.log(np.maximum(mult, 1.0)), 0.0).astype(np.float32)
    bucket = np.where(valid, _t5_bucket(dist_np), -1).astype(np.int32)
    return jnp.asarray(bucket)[None], jnp.asarray(add)[None]


def _attn_prompt_kernel(q_ref, k_ref, v_ref, bias_ref, o_ref, *stats):
    lane = lax.broadcasted_iota(I32, (NK, 128), 1)
    head0 = lane < HEAD_DIM
    prev_cols = lax.broadcasted_iota(I32, (NK, 2 * NK), 1) < NK
    ones = jnp.ones((NK, 128), F32)
    n_br = len(DILATIONS)
    m_refs, l_refs, a_refs = stats[0:n_br], stats[n_br:2 * n_br], stats[2 * n_br:3 * n_br]
    for di in range(n_br):
        d = DILATIONS[di]
        n_blocks = T_P // (d * NK)
        n_units = ATTN_UNITS[di]
        run = min(n_units, n_blocks)
        assert n_units % run == 0 and n_blocks % run == 0

        def group(gi, carry, di=di, d=d, n_blocks=n_blocks, n_units=n_units, run=run):
            def idx(start):
                return pl.ds(start, NK) if d == 1 else pl.ds(start, NK, stride=d)

            def keys(start):
                return k_ref[idx(start), :].astype(BF16)

            def values(start):
                return jnp.concatenate([v_ref[idx(start), :], ones], axis=1).astype(BF16)

            units = []
            for j in range(n_units):
                u = gi * n_units + j
                r = u >> int(math.log2(n_blocks))
                n = u & (n_blocks - 1)
                cur = r + d * NK * n
                if j % run == 0:
                    prv = r + d * NK * jnp.maximum(n - 1, 0)
                    k_prev, v_prev = keys(prv), values(prv)
                    pen = jnp.where(n == 0, -jnp.inf, 0.0).astype(F32)
                else:
                    k_prev, v_prev, pen = k_cur, v_cur, None
                k_cur, v_cur = keys(cur), values(cur)
                units.append(dict(cur=cur, pen=pen, q=q_ref[idx(cur), :] * (HEAD_DIM ** -0.5),
                                  k2=jnp.concatenate([k_prev, k_cur], axis=0),
                                  v2=jnp.concatenate([v_prev, v_cur], axis=0)))
            for unit in units:
                pv, rs, mn = [], [], []
                for h in range(2):
                    qh = jnp.where(head0 if h == 0 else ~head0, unit["q"], 0.0).astype(BF16)
                    s = _dot_nt(qh, unit["k2"]) + bias_ref[di, h]
                    if unit["pen"] is not None:
                        s = s + jnp.where(prev_cols, unit["pen"], 0.0)
                    m_h = jnp.max(s, axis=-1, keepdims=True)
                    pv_rs = _dot(jnp.exp(s - m_h).astype(BF16), unit["v2"])
                    pv.append(pv_rs[:, 0:128])
                    rs.append(pv_rs[:, 128:256])
                    mn.append(m_h)
                m_refs[di][idx(unit["cur"]), :] = jnp.where(head0, mn[0], mn[1])
                l_refs[di][idx(unit["cur"]), :] = jnp.where(head0, rs[0], rs[1])
                a_refs[di][idx(unit["cur"]), :] = jnp.where(head0, pv[0], pv[1])
            return carry

        lax.fori_loop(0, T_P // NK // n_units, group, 0)

    def merge(bi, carry):
        rows = pl.ds(pl.multiple_of(bi * NK, NK), NK)
        ms = [m_ref[rows, :] for m_ref in m_refs]
        m = functools.reduce(jnp.maximum, ms)
        num = jnp.zeros((NK, 128), F32)
        den = jnp.zeros((NK, 128), F32)
        for m_i, l_ref, a_ref in zip(ms, l_refs, a_refs):
            w = jnp.exp(m_i - m)
            num = num + w * a_ref[rows, :]
            den = den + w * l_ref[rows, :]
        o_ref[rows, :] = num / den
        return carry

    lax.fori_loop(0, T_P // NK, merge, 0, unroll=2)


def _attn_prompt(z, bias):
    blk = lambda off: pl.BlockSpec((T_P, 128), lambda b, h, off=off: (b, off + h))
    return pl.pallas_call(
        _attn_prompt_kernel,
        grid=(B_P, H_A // 2),
        in_specs=[blk(0), blk(W_A // 128), blk(2 * W_A // 128),
                  pl.BlockSpec((len(DILATIONS), 2, NK, 2 * NK), lambda b, h: (0, h, 0, 0))],
        out_specs=pl.BlockSpec((T_P, 128), lambda b, h: (b, h)),
        out_shape=jax.ShapeDtypeStruct((N_PROMPT, W_A), F32),
        scratch_shapes=[pltpu.VMEM((T_P, 128), F32)] * (3 * len(DILATIONS)),
        compiler_params=_params(("arbitrary", "arbitrary")),
        name="attn_prompt",
    )(z, z, z, bias)


def _attn_sample_kernel(q_ref, kn_ref, vn_ref, ck_ref, cv_ref, bias_ref, o_ref, kf_ref, vf_ref):
    for f_ref, c_ref, n_ref in ((kf_ref, ck_ref, kn_ref), (vf_ref, cv_ref, vn_ref)):
        f_ref[0:BUF, :] = c_ref[...]
        f_ref[BUF:BUF + T_S, :] = n_ref[...]
        f_ref[BUF + T_S:KF_ROWS, :] = jnp.zeros((KF_ROWS - BUF - T_S, 128), F32)
    head0 = lax.broadcasted_iota(I32, (T_S, 128), 1) < HEAD_DIM
    q = q_ref[...] * (HEAD_DIM ** -0.5)
    q2 = jnp.concatenate([jnp.where(head0, q, 0.0), jnp.where(head0, 0.0, q)], axis=0).astype(BF16)
    bias2 = jnp.concatenate([bias_ref[0], bias_ref[1]], axis=0)
    s = _dot_nt(q2, kf_ref[...].astype(BF16)) + bias2
    m = jnp.max(s, axis=-1, keepdims=True)
    p = jnp.exp(s - m)
    l = jnp.sum(p, axis=-1, keepdims=True)
    o2 = _dot(p.astype(BF16), vf_ref[...].astype(BF16)) / l
    o_ref[...] = jnp.where(head0, o2[0:T_S], o2[T_S:2 * T_S])


def _attn_sample(z, cache_k, cache_v, layer, bias):
    row0 = N_PROMPT // T_S
    blk = lambda off: pl.BlockSpec((T_S, 128), lambda b, h, off=off: (row0 + b, off + h))
    cache = pl.BlockSpec((None, BUF, 128), lambda b, h: (layer * B_S + b, 0, h))
    return pl.pallas_call(
        _attn_sample_kernel,
        grid=(B_S, H_A // 2),
        in_specs=[blk(0), blk(W_A // 128), blk(2 * W_A // 128), cache, cache,
                  pl.BlockSpec((2, T_S, KF_ROWS), lambda b, h: (h, 0, 0))],
        out_specs=pl.BlockSpec((T_S, 128), lambda b, h: (b, h)),
        out_shape=jax.ShapeDtypeStruct((N_SAMPLE, W_A), F32),
        scratch_shapes=[pltpu.VMEM((KF_ROWS, 128), F32)] * 2,
        compiler_params=_params(("arbitrary", "arbitrary")),
        name="attn_sample",
    )(z, z, z, cache_k, cache_v, bias)


def _gmlp_kernel(ub_ref, vb_ref, gbv_ref, bbv_ref, ws_ref, bs_ref, gmix_ref, bo_ref, vn_ref):
    v = jax.nn.gelu(vb_ref[...])
    mu = jnp.mean(v, axis=-1, keepdims=True)
    var = jnp.mean(jnp.square(v - mu), axis=-1, keepdims=True)
    vn = (v - mu) * lax.rsqrt(var + EPS) * gbv_ref[...] + bbv_ref[...]
    vn_ref[...] = vn
    vnb = vn.astype(BF16)
    causal = (lax.broadcasted_iota(I32, (CHUNK, CHUNK), 1) <= lax.broadcasted_iota(I32, (CHUNK, CHUNK), 0))
    first = lax.broadcasted_iota(I32, (CHUNK, 128), 1) < W_B // G_B
    parts = []
    for gp in range(G_B // 2):
        v2 = vnb[:, gp * 128:(gp + 1) * 128]
        m0 = _dot(jnp.where(causal, ws_ref[2 * gp], 0.0).astype(BF16), v2)
        m1 = _dot(jnp.where(causal, ws_ref[2 * gp + 1], 0.0).astype(BF16), v2)
        parts.append(jnp.where(first, m0, m1))
    mix = jnp.concatenate(parts, axis=1) + bs_ref[...]
    bo_ref[...] = _rms(jax.nn.gelu(ub_ref[...]) * mix, gmix_ref[...])


def _gmlp(z, n_chunks, col_u, col_v, g_bv, b_bv, w_s, b_s, g_mix_b):
    bs_wide = jnp.repeat(b_s.T, W_B // G_B, axis=1)
    vec = pl.BlockSpec((1, W_B), lambda i: (0, 0))
    out = pl.BlockSpec((CHUNK, W_B), lambda i: (i, 0))
    return pl.pallas_call(
        _gmlp_kernel,
        grid=(n_chunks,),
        in_specs=[pl.BlockSpec((CHUNK, W_B), lambda i: (i, col_u)),
                  pl.BlockSpec((CHUNK, W_B), lambda i: (i, col_v)),
                  vec, vec,
                  pl.BlockSpec((G_B, CHUNK, CHUNK), lambda i: (0, 0, 0)),
                  pl.BlockSpec((CHUNK, W_B), lambda i: (0, 0)),
                  vec],
        out_specs=[out, out],
        out_shape=[jax.ShapeDtypeStruct((n_chunks * CHUNK, W_B), F32)] * 2,
        compiler_params=_params(("arbitrary",)),
        name="gmlp",
    )(z, z, g_bv.reshape(1, W_B), b_bv.reshape(1, W_B), w_s, bs_wide, g_mix_b.reshape(1, W_B))


def _cmul(a, b):
    return a[0] * b[0] - a[1] * b[1], a[0] * b[1] + a[1] * b[0]


def _s5_kernel(u_ref, lr_ref, li_ref, ldt_ref, btr_ref, bti_ref, cr_ref, ci_ref, d_ref, h0r_ref, h0i_ref,
               y_ref, hr_ref, hi_ref, er_ref, ei_ref, gr_ref, gi_ref, *, L, nj, nsc, nb):
    hi = lax.Precision.HIGHEST
    lr, li = lr_ref[...], li_ref[...]
    dt = jnp.exp(ldt_ref[...])
    mag = jnp.exp(lr * dt)
    a = (mag * jnp.cos(li * dt), mag * jnp.sin(li * dt))
    den = lr * lr + li * li
    q = (((a[0] - 1.0) * lr + a[1] * li) / den, (a[1] * lr - (a[0] - 1.0) * li) / den)
    same = ((lax.broadcasted_iota(I32, (128, S5_SW), 0) >> int(math.log2(SSM_GROUP)))
            == (lax.broadcasted_iota(I32, (128, S5_SW), 1) >> int(math.log2(SSM_STATE))))
    bb = _cmul(q, (jnp.where(same, btr_ref[...], 0.0), jnp.where(same, bti_ref[...], 0.0)))
    c = (jnp.where(same, cr_ref[...], 0.0), jnp.where(same, ci_ref[...], 0.0))
    pw = [(jnp.ones_like(lr), jnp.zeros_like(lr))]
    for _ in range(L):
        pw.append(_cmul(pw[-1], a))
    ck = [_cmul(c, p) for p in pw]
    kbd = [(_dot_nt(bb[0], ck[k][0], hi) - _dot_nt(bb[1], ck[k][1], hi)).astype(BF16) for k in range(L)]
    mins = [tuple(x.astype(BF16) for x in _cmul(pw[L - 1 - s], bb)) for s in range(L)]
    cob = [tuple(x.astype(BF16) for x in ck[i + 1]) for i in range(L)]

    us = [u_ref[:, s * 128:(s + 1) * 128] for s in range(L)]
    ub = [x.astype(BF16) for x in us]
    sr = _dot(ub[0], mins[0][0])
    si = _dot(ub[0], mins[0][1])
    for s in range(1, L):
        sr = sr + _dot(ub[s], mins[s][0])
        si = si + _dot(ub[s], mins[s][1])
    al = pw[L]
    r = nsc * nb
    xl = [(sr[0:r], si[0:r])]
    for j in range(1, nj):
        t = _cmul(al, xl[-1])
        xl.append((t[0] + sr[j * r:(j + 1) * r], t[1] + si[j * r:(j + 1) * r]))
    alp = [al]
    for j in range(1, nj):
        alp.append(_cmul(alp[-1], al))
    er_ref[...] = xl[-1][0]
    ei_ref[...] = xl[-1][1]
    g = (h0r_ref[...], h0i_ref[...])
    for sc in range(nsc):
        gr_ref[sc * nb:(sc + 1) * nb, :] = g[0]
        gi_ref[sc * nb:(sc + 1) * nb, :] = g[1]
        t = _cmul(alp[nj - 1], g)
        g = (t[0] + er_ref[sc * nb:(sc + 1) * nb, :], t[1] + ei_ref[sc * nb:(sc + 1) * nb, :])
    hr_ref[...] = g[0]
    hi_ref[...] = g[1]
    gcat = (gr_ref[...], gi_ref[...])
    hprev = [gcat]
    for j in range(1, nj):
        t = _cmul(alp[j - 1], gcat)
        hprev.append((xl[j - 1][0] + t[0], xl[j - 1][1] + t[1]))
    hpr = jnp.concatenate([x[0] for x in hprev], axis=0).astype(BF16)
    hpi = jnp.concatenate([x[1] for x in hprev], axis=0).astype(BF16)
    y = (jnp.concatenate([d_ref[...]] * L, axis=1) * u_ref[...]
         + _dot_nt(hpr, jnp.concatenate([x[0] for x in cob], axis=0))
         - _dot_nt(hpi, jnp.concatenate([x[1] for x in cob], axis=0)))
    for s in range(L):
        conv = _dot(ub[s], jnp.concatenate(kbd[0:L - s], axis=1))
        y = y + conv if s == 0 else jnp.concatenate([y[:, :s * 128], y[:, s * 128:] + conv], axis=1)
    y_ref[...] = y


def _s5(u, lw, h0r, h0i, L, nj, nsc, nb):
    m, n = u.shape[1], u.shape[2]
    lanes = lambda x: x.reshape(S5_NLB, 1, S5_SW)
    rep = lambda x: jnp.tile(x.reshape(S5_NLB, 128, SSM_STATE), (1, 1, S5_GPB))
    vec = pl.BlockSpec((None, 1, S5_SW), lambda g: (g, 0, 0))
    mat = pl.BlockSpec((None, 128, S5_SW), lambda g: (g, 0, 0))
    st = pl.BlockSpec((None, nb, S5_SW), lambda g: (g, 0, 0))
    return pl.pallas_call(
        functools.partial(_s5_kernel, L=L, nj=nj, nsc=nsc, nb=nb),
        grid=(S5_NLB,),
        in_specs=[pl.BlockSpec((None, m, n), lambda g: (g, 0, 0)), vec, vec, vec, mat, mat, mat, mat,
                  pl.BlockSpec((None, 1, 128), lambda g: (g, 0, 0)), st, st],
        out_specs=[pl.BlockSpec((None, m, n), lambda g: (g, 0, 0)), st, st],
        out_shape=[jax.ShapeDtypeStruct((S5_NLB, m, n), F32)] + [jax.ShapeDtypeStruct((S5_NLB, nb, S5_SW), F32)] * 2,
        scratch_shapes=[pltpu.VMEM((nsc * nb, S5_SW), F32)] * 4,
        compiler_params=_params(("arbitrary",)),
        name="s5",
    )(u, lanes(lw["lam_re"]), lanes(lw["lam_im"]), lanes(jnp.repeat(lw["log_dt"], SSM_STATE)),
      rep(lw["ssm_b_re"].transpose(0, 2, 1)), rep(lw["ssm_b_im"].transpose(0, 2, 1)),
      rep(lw["ssm_c_re"]), rep(lw["ssm_c_im"]), lw["ssm_d"].reshape(S5_NLB, 1, 128), h0r, h0i)


def _s5_post_kernel(y_ref, w_ref, b_ref, g_ref, o_ref):
    zz = jax.nn.gelu(y_ref[...])
    out = zz * jax.nn.sigmoid(_dot(zz.astype(BF16), w_ref[...]) + b_ref[...])
    o_ref[...] = _rms(out, g_ref[...])


def _s5_post(y, w_glu, b_glu, g_mix_c):
    n = y.shape[0]
    tt = min(n, TM)
    vec = pl.BlockSpec((1, W_C), lambda i: (0, 0))
    return pl.pallas_call(
        _s5_post_kernel,
        grid=(n // tt,),
        in_specs=[pl.BlockSpec((tt, W_C), lambda i: (i, 0)), pl.BlockSpec((W_C, W_C), lambda i: (0, 0)), vec, vec],
        out_specs=pl.BlockSpec((tt, W_C), lambda i: (i, 0)),
        out_shape=jax.ShapeDtypeStruct((n, W_C), F32),
        compiler_params=_params(("arbitrary",)),
        name="s5_post",
    )(y, w_glu.astype(BF16), b_glu.reshape(1, W_C), g_mix_c.reshape(1, W_C))


def _router_kernel(h_ref, w_ref, b_ref, ind_ref, wd_ref, cnt_ref):
    i = pl.program_id(0)
    scores = jax.nn.sigmoid(_dot(h_ref[...], w_ref[...]))
    lane = lax.broadcasted_iota(I32, (TD, N_EXPERTS), 1).astype(F32)
    row = lax.broadcasted_iota(I32, (TD, N_EXPERTS), 0) + i * TD
    sel = scores + b_ref[...]
    ind = jnp.zeros((TD, N_EXPERTS), F32)
    for _ in range(TOP_K):
        m = jnp.max(sel, axis=-1, keepdims=True)
        idx = jnp.min(jnp.where(sel == m, lane, float(N_EXPERTS)), axis=-1, keepdims=True)
        hit = lane == idx
        ind = jnp.where(hit, 1.0, ind)
        sel = jnp.where(hit, -jnp.inf, sel)
    ind = jnp.where(row < N_REAL, ind, 0.0)
    wsel = ind * scores
    wd_ref[...] = wsel / jnp.sum(wsel + (1.0 - jnp.max(ind, axis=-1, keepdims=True)) / N_EXPERTS,
                                 axis=-1, keepdims=True) * ROUTE_SCALE
    ind_ref[...] = ind

    @pl.when(i == 0)
    def _():
        cnt_ref[...] = jnp.zeros((8, N_EXPERTS), F32)

    cnt_ref[...] += jnp.sum(ind, axis=0, keepdims=True)


def _router(h2b, w_router, b_router):
    til = pl.BlockSpec((TD, N_EXPERTS), lambda i: (i, 0))
    return pl.pallas_call(
        _router_kernel,
        grid=(NP // TD,),
        in_specs=[pl.BlockSpec((TD, D), lambda i: (i, 0)), pl.BlockSpec((D, N_EXPERTS), lambda i: (0, 0)),
                  pl.BlockSpec((1, N_EXPERTS), lambda i: (0, 0))],
        out_specs=[til, til, pl.BlockSpec((8, N_EXPERTS), lambda i: (0, 0))],
        out_shape=[jax.ShapeDtypeStruct((NP, N_EXPERTS), F32)] * 2 + [jax.ShapeDtypeStruct((8, N_EXPERTS), F32)],
        compiler_params=_params(("arbitrary",)),
        name="router",
    )(h2b, w_router.astype(BF16), b_router.reshape(1, N_EXPERTS))


def _group_starts(cnt):
    padded = jnp.floor((cnt + (BM - 1)) / BM) * BM
    tri = (lax.broadcasted_iota(I32, (N_EXPERTS, N_EXPERTS), 0)
           <= lax.broadcasted_iota(I32, (N_EXPERTS, N_EXPERTS), 1)).astype(F32)
    pend = jnp.dot(padded, tri, preferred_element_type=F32, precision=lax.Precision.HIGHEST)
    return padded, pend - padded, pend


def _plan_kernel(cnt_ref, be_ref, misc_ref):
    cnt = cnt_ref[...]
    _, pstart, pend = _group_starts(cnt)
    rowi = (lax.broadcasted_iota(I32, (NBP, N_EXPERTS), 0) * BM).astype(F32)
    be = jnp.minimum(jnp.sum(jnp.where(pend[0:1] <= rowi, 1.0, 0.0), axis=-1, keepdims=True), N_EXPERTS - 1.0)
    be_ref[...] = jnp.broadcast_to(be, (NBP, 128)).astype(I32)
    misc_ref[0:8, :] = pstart.astype(I32)
    misc_ref[8:16, :] = cnt.astype(I32)
    misc_ref[16:24, :] = (pend / BM).astype(I32)


def _plan(cnt):
    return pl.pallas_call(
        _plan_kernel,
        out_shape=[jax.ShapeDtypeStruct((NBP, 128), I32), jax.ShapeDtypeStruct((24, N_EXPERTS), I32)],
        name="moe_plan",
    )(cnt)


def _slots_kernel(ind_ref, wd_ref, cnt_ref, dest_ref, w_ref, run_ref):
    @pl.when(pl.program_id(0) == 0)
    def _():
        run_ref[...] = jnp.zeros((8, N_EXPERTS), F32)

    ind = ind_ref[...]
    _, pstart, _ = _group_starts(cnt_ref[...])
    ltri = (lax.broadcasted_iota(I32, (TD, TD), 1) < lax.broadcasted_iota(I32, (TD, TD), 0)).astype(BF16)
    pos = _dot(ltri, ind.astype(BF16)) + run_ref[0:1, :]
    run_ref[...] += jnp.sum(ind, axis=0, keepdims=True)
    cur = jnp.where(ind > 0.0, pstart[0:1] + pos, BIG)
    lane = lax.broadcasted_iota(I32, (TD, 128), 1)
    dest = jnp.full((TD, 128), -1.0, F32)
    wts = jnp.zeros((TD, 128), F32)
    wd = wd_ref[...]
    for k in range(TOP_K):
        m = jnp.min(cur, axis=-1, keepdims=True)
        hit = cur == m
        wk = jnp.sum(jnp.where(hit, wd, 0.0), axis=-1, keepdims=True)
        dest = jnp.where(lane == k, jnp.where(m < BIG, m, -1.0), dest)
        wts = jnp.where(lane == k, jnp.where(m < BIG, wk, 0.0), wts)
        cur = jnp.where(hit, BIG, cur)
    dest_ref[...] = dest.astype(I32)
    w_ref[...] = wts


def _slots(ind, wd, cnt):
    til = pl.BlockSpec((TD, N_EXPERTS), lambda i: (i, 0))
    out = pl.BlockSpec((TD, 128), lambda i: (i, 0))
    return pl.pallas_call(
        _slots_kernel,
        grid=(NP // TD,),
        in_specs=[til, til, pl.BlockSpec((8, N_EXPERTS), lambda i: (0, 0))],
        out_specs=[out, out],
        out_shape=[jax.ShapeDtypeStruct((NP, 128), I32), jax.ShapeDtypeStruct((NP, 128), F32)],
        scratch_shapes=[pltpu.VMEM((8, N_EXPERTS), F32)],
        compiler_params=_params(("arbitrary",)),
        name="moe_slots",
    )(ind, wd, cnt)


def _dispatch_kernel(ps_ref, cnt_ref, nbu_ref, dest_ref, x_ref, xs_ref, zero_ref, sem):
    i = pl.program_id(0)

    def token(ref, t):
        start = t * ROW_TILES if isinstance(t, int) else pl.multiple_of(t * ROW_TILES, ROW_TILES)
        return ref.at[pl.ds(start, ROW_TILES), :]

    def row_copy(r, k):
        return pltpu.make_async_copy(token(x_ref, r), token(xs_ref, dest_ref[r * 8 + k]), sem)

    def zero_rows(d, n):
        rows = n * ROW_TILES
        return pltpu.make_async_copy(zero_ref.at[pl.ds(0, rows), :],
                                     xs_ref.at[pl.ds(pl.multiple_of(d * ROW_TILES, ROW_TILES), rows), :], sem)

    def zero_block(b):
        rows = BM * ROW_TILES
        return pltpu.make_async_copy(zero_ref, xs_ref.at[pl.ds(pl.multiple_of(b * rows, rows), rows), :], sem)

    @pl.when(i < N_DTILES)
    def _():
        n_rows = jnp.minimum(TD, N_REAL - i * TD)

        def issue(r, c):
            for k in range(TOP_K):
                row_copy(r, k).start(priority=k % 2)
            return c

        def drain(r, c):
            for k in range(TOP_K):
                row_copy(r, k).wait()
            return c

        lax.fori_loop(0, n_rows, issue, 0)
        lax.fori_loop(0, n_rows, drain, 0)

    @pl.when(i == N_DTILES)
    def _():
        zero_ref[...] = jnp.zeros(zero_ref.shape, F32)

        def per_expert(e, c):
            cnt = cnt_ref[e]
            base = ps_ref[e] + cnt
            rem = (((cnt + (BM - 1)) >> BM_SHIFT) << BM_SHIFT) - cnt
            for start in (True, False):
                for k in range(BM_SHIFT):
                    cp = zero_rows(base + ((rem >> (k + 1)) << (k + 1)), 1 << k)

                    @pl.when((rem & (1 << k)) != 0)
                    def _():
                        cp.start() if start else cp.wait()
            return c

        lax.fori_loop(0, N_EXPERTS, per_expert, 0)
        lax.fori_loop(nbu_ref[0], NB, lambda b, c: (zero_block(b).start(), c)[1], 0)
        lax.fori_loop(nbu_ref[0], NB, lambda b, c: (zero_block(b).wait(), c)[1], 0)


def _dispatch(pstart_i, cnt_i, nbu, dest_flat, x):
    return pl.pallas_call(
        _dispatch_kernel,
        grid_spec=pltpu.PrefetchScalarGridSpec(
            num_scalar_prefetch=3, grid=(N_DTILES + 1,),
            in_specs=[pl.BlockSpec((TD * 8,), lambda i, *_: (jnp.minimum(i, N_DTILES - 1),),
                                   memory_space=pltpu.SMEM),
                      pl.BlockSpec((TD * ROW_TILES, 128), lambda i, *_: (jnp.minimum(i, N_DTILES - 1), 0))],
            out_specs=pl.BlockSpec(memory_space=pl.ANY),
            scratch_shapes=[pltpu.VMEM((BM * ROW_TILES, 128), F32), pltpu.SemaphoreType.DMA(())]),
        out_shape=jax.ShapeDtypeStruct((CAP * ROW_TILES, 128), F32),
        compiler_params=_params(("arbitrary",)),
        name="moe_dispatch",
    )(pstart_i, cnt_i, nbu, dest_flat, x)


def _expert_kernel(be_ref, nbu_ref, gend_ref, x_ref, wg_hbm, wu_hbm, wd_hbm, o_ref,
                   wg_ref, wu_ref, wd_ref, slot_ref, sem, *, layer):
    i = pl.program_id(0)
    nbu = nbu_ref[0]

    def weight_copies(e, s):
        return [pltpu.make_async_copy(hbm.at[layer, e], buf.at[s], sem.at[s, j])
                for j, (hbm, buf) in enumerate(((wg_hbm, wg_ref), (wu_hbm, wu_ref), (wd_hbm, wd_ref)))]

    @pl.when(i < nbu)
    def _():
        e = be_ref[i]

        @pl.when(i == 0)
        def _():
            slot_ref[0] = 0
            for cp in weight_copies(e, 0):
                cp.start()

        new_group = jnp.logical_or(i == 0, be_ref[jnp.maximum(i - 1, 0)] != e)

        @pl.when(jnp.logical_and(new_group, i > 0))
        def _():
            slot_ref[0] = 1 - slot_ref[0]

        s = slot_ref[0]

        @pl.when(new_group)
        def _():
            for cp in weight_copies(e, s):
                cp.wait()
            nxt = gend_ref[e]

            @pl.when(nxt < nbu)
            def _():
                for cp in weight_copies(be_ref[jnp.minimum(nxt, NB - 1)], 1 - s):
                    cp.start()

        x = _load_token_rows(x_ref, BM).astype(BF16)
        g = _dot(x, wg_ref[s].astype(BF16))
        u = _dot(x, wu_ref[s].astype(BF16))
        a = (g * jax.nn.sigmoid(g) * u).astype(BF16)
        _store_token_rows(o_ref, _dot(a, wd_ref[s].astype(BF16)))

    @pl.when(i >= nbu)
    def _():
        o_ref[...] = jnp.zeros(o_ref.shape, F32)


def _experts(layer, be, nbu, group_end, xs, w_gate, w_up, w_down):
    any_spec = pl.BlockSpec(memory_space=pl.ANY)
    return pl.pallas_call(
        functools.partial(_expert_kernel, layer=layer),
        grid_spec=pltpu.PrefetchScalarGridSpec(
            num_scalar_prefetch=3, grid=(NB,),
            in_specs=[pl.BlockSpec((BM * ROW_TILES, 128), lambda i, be, nbu, ge: (jnp.minimum(i, nbu[0] - 1), 0)),
                      any_spec, any_spec, any_spec],
            out_specs=pl.BlockSpec((BM * ROW_TILES, 128), lambda i, be, nbu, ge: (i, 0)),
            scratch_shapes=[pltpu.VMEM((2, D, D_EXPERT), F32), pltpu.VMEM((2, D, D_EXPERT), F32),
                            pltpu.VMEM((2, D_EXPERT, D), F32), pltpu.SMEM((1,), I32),
                            pltpu.SemaphoreType.DMA((2, 3))]),
        out_shape=jax.ShapeDtypeStruct((CAP * ROW_TILES, 128), F32),
        compiler_params=_params(("arbitrary",)),
        name="moe_experts",
    )(be, nbu, group_end, xs, w_gate, w_up, w_down)


def _shared_kernel(h_ref, wg_ref, wu_ref, wd_ref, o_ref):
    h = h_ref[...]
    g = _dot(h, wg_ref[...])
    u = _dot(h, wu_ref[...])
    o_ref[...] = _dot((g * jax.nn.sigmoid(g) * u).astype(BF16), wd_ref[...])


def _shared(h2b, wg, wu, wd):
    return pl.pallas_call(
        _shared_kernel,
        grid=(NT,),
        in_specs=[pl.BlockSpec((TM, D), lambda i: (i, 0)),
                  pl.BlockSpec((D, D_SHARED), lambda i: (0, 0)),
                  pl.BlockSpec((D, D_SHARED), lambda i: (0, 0)),
                  pl.BlockSpec((D_SHARED, D), lambda i: (0, 0))],
        out_specs=pl.BlockSpec((TM, D), lambda i: (i, 0)),
        out_shape=jax.ShapeDtypeStruct((NP, D), F32),
        compiler_params=_params(("arbitrary",)),
        name="moe_shared",
    )(h2b, wg.astype(BF16), wu.astype(BF16), wd.astype(BF16))


def _combine_kernel(dest_ref, nxt_ref, w_ref, ys_ref, sh_ref, x_ref, gr_ref, gt_ref, o_ref, rows_ref, acc_ref, sem):
    i = pl.program_id(0)
    n_tiles = NP // TC_
    slot = i % 2

    def gather(d_ref, s, start):
        def body(r, c):
            for k in range(TOP_K):
                d = jnp.maximum(d_ref[r * 8 + k], 0)
                cp = pltpu.make_async_copy(
                    ys_ref.at[pl.ds(pl.multiple_of(d * ROW_TILES, ROW_TILES), ROW_TILES), :],
                    rows_ref.at[s, k, pl.ds(pl.multiple_of(r * ROW_TILES, ROW_TILES), ROW_TILES), :], sem.at[s])
                if start:
                    cp.start(priority=k % 2)
                else:
                    cp.wait()
            return c

        lax.fori_loop(0, TC_, body, 0)

    @pl.when(i == 0)
    def _():
        gather(dest_ref, 0, True)

    @pl.when(i + 1 < n_tiles)
    def _():
        gather(nxt_ref, 1 - slot, True)

    gather(dest_ref, slot, False)
    def weigh(t, c):
        rows = pl.ds(pl.multiple_of(t * ROW_TILES, ROW_TILES), ROW_TILES)
        acc = w_ref[t * 8] * rows_ref[slot, 0, rows, :]
        for k in range(1, TOP_K):
            acc = acc + w_ref[t * 8 + k] * rows_ref[slot, k, rows, :]
        acc_ref[rows, :] = acc
        return c

    lax.fori_loop(0, TC_, weigh, 0, unroll=4)
    routed = _load_token_rows(acc_ref, TC_)
    last = i >= N_PROMPT // TC_
    o_ref[...] = x_ref[...] + _pick(last, gt_ref, gr_ref) * (routed + sh_ref[...])


def _combine(dest_flat, w6, ys, shared, x, gate):
    n_tiles = NP // TC_
    n_p = N_PROMPT // TC_
    per = TM // TC_
    til = pl.BlockSpec((TC_, D), lambda i: (i, 0))
    return pl.pallas_call(
        _combine_kernel,
        grid=(n_tiles,),
        in_specs=[pl.BlockSpec((TC_ * 8,), lambda i: (i,), memory_space=pltpu.SMEM),
                  pl.BlockSpec((TC_ * 8,), lambda i: (jnp.minimum(i + 1, n_tiles - 1),), memory_space=pltpu.SMEM),
                  pl.BlockSpec((TC_ * 8,), lambda i: (i,), memory_space=pltpu.SMEM),
                  pl.BlockSpec(memory_space=pl.ANY), til, til,
                  pl.BlockSpec((None, 1, D), lambda i: (i // per, 0, 0)),
                  pl.BlockSpec((TC_, D), lambda i: (jnp.maximum(i - n_p, 0), 0))],
        out_specs=til,
        out_shape=jax.ShapeDtypeStruct((NP, D), F32),
        scratch_shapes=[pltpu.VMEM((2, TOP_K, TC_ * ROW_TILES, 128), F32), pltpu.VMEM((TC_ * ROW_TILES, 128), F32),
                        pltpu.SemaphoreType.DMA((2,))],
        compiler_params=_params(("arbitrary",)),
        name="moe_combine",
    )(dest_flat, dest_flat, w6, ys, shared, x, gate[0], gate[1])


def _pad_rows(a, rows):
    return jnp.concatenate([a, jnp.zeros((rows - a.shape[0],) + a.shape[1:], a.dtype)], axis=0)


def _s5_layer(z, lw, h0r_s, h0i_s):
    g_mix_c = lw["g_mix"][W_A + W_B:]
    nj = nsc = 16
    state = lambda h, b: h.reshape(S5_NLB, b, S5_GPB, SSM_STATE).transpose(1, 0, 2, 3).reshape(b, G_C, SSM_STATE)
    uc_p = z[:N_PROMPT, IN_COLS - W_C:]
    u_p = uc_p.reshape(B_P, nsc, nj, S5_L_P, S5_NLB, 128).transpose(4, 2, 1, 0, 3, 5)
    u_p = u_p.reshape(S5_NLB, nj * nsc * B_P, S5_L_P * 128)
    zeros_h = jnp.zeros((S5_NLB, B_P, S5_SW), F32)
    y_p, hr_p, hi_p = _s5(u_p, lw, zeros_h, zeros_h, S5_L_P, nj, nsc, B_P)
    y_p = y_p.reshape(S5_NLB, nj, nsc, B_P, S5_L_P, 128).transpose(3, 2, 1, 4, 0, 5).reshape(N_PROMPT, W_C)
    c_p = _s5_post(y_p, lw["w_glu"], lw["b_glu"], g_mix_c)

    uc_s = z[N_PROMPT:N_REAL, IN_COLS - W_C:]
    u_s = uc_s.reshape(B_S, T_S, S5_NLB, 128).transpose(2, 0, 1, 3).reshape(S5_NLB, B_S, S5_L_S * 128)
    h0 = lambda h: h.reshape(B_S, S5_NLB, S5_SW).transpose(1, 0, 2)
    y_s, hr_s, hi_s = _s5(u_s, lw, h0(h0r_s), h0(h0i_s), S5_L_S, 1, 1, B_S)
    y_s = y_s.reshape(S5_NLB, B_S, T_S, 128).transpose(1, 2, 0, 3).reshape(N_SAMPLE, W_C)
    c_s = _s5_post(y_s, lw["w_glu"], lw["b_glu"], g_mix_c)
    return c_p, c_s, state(hr_p, B_P), state(hi_p, B_P), state(hr_s, B_S), state(hi_s, B_S)


def _layer(x, l, lw, experts, mod, bias_p, bias_s, cache_k, cache_v, h0r_s, h0i_s):
    sh1, sc1, g1, sh2, sc2, g2 = [_tile_tabs(m) for m in jnp.split(mod, 6, axis=-1)]
    z = _matmul_norm(x, lw["g_attn"], sc1, sh1, lw["w_in"].astype(BF16), 1536)

    a_p = _attn_prompt(z, bias_p)
    a_s = _attn_sample(z, cache_k, cache_v, l, bias_s)

    g_mix_b = lw["g_mix"][W_A:W_A + W_B]
    gm = (lw["g_bv"], lw["b_bv"], lw["w_s"], lw["b_s"], g_mix_b)
    b_p, _ = _gmlp(z, N_PROMPT // CHUNK, 3 * W_A // W_B, 3 * W_A // W_B + 1, *gm)
    uv_s = z[N_PROMPT:N_REAL, 3 * W_A:3 * W_A + 2 * W_B].reshape(B_S, T_S, 2 * W_B)
    uv_s = jnp.pad(uv_s, ((0, 0), (0, CHUNK - T_S), (0, 0))).reshape(B_S * CHUNK, 2 * W_B)
    b_s, vn_s = _gmlp(uv_s, B_S, 0, 1, *gm)
    b_s = b_s.reshape(B_S, CHUNK, W_B)[:, :T_S].reshape(N_SAMPLE, W_B)
    new_vb = vn_s.reshape(B_S, CHUNK, W_B)[:, :T_S]

    c_p, c_s, hr_p, hi_p, hr_s, hi_s = _s5_layer(z, lw, h0r_s, h0i_s)

    x = _matmul_mix_residual(a_p, b_p, c_p, _pad_rows(a_s, TM), _pad_rows(b_s, TM), _pad_rows(c_s, TM),
                             lw["g_mix"][:W_A], lw["w_out"].astype(BF16), x, g1, 1024)

    h2b, h2 = _modnorm(x, lw["g_ffn"], sc2, sh2)
    ind, wd, cnt = _router(h2b, lw["w_router"], lw["b_router"])
    be, misc = _plan(cnt)
    dest, w6 = _slots(ind, wd, cnt)
    dest_flat = dest[:, :8].reshape(-1)
    xs = _dispatch(misc[0], misc[8], misc[16, N_EXPERTS - 1:], dest_flat, h2)
    ys = _experts(l, be[:NB, 0], misc[16, N_EXPERTS - 1:], misc[16], xs, *experts)
    shared = _shared(h2b, lw["w_sh_gate"], lw["w_sh_up"], lw["w_sh_down"])
    x = _combine(dest_flat, w6[:, :8].reshape(-1), ys, shared, x, g2)

    kv = lambda lo: (z[:N_PROMPT, lo:lo + W_A].reshape(B_P, T_P, H_A, HEAD_DIM)[:, T_P - BUF:],
                     z[N_PROMPT:N_REAL, lo:lo + W_A].reshape(B_S, T_S, H_A, HEAD_DIM))
    (k_p, k_s), (v_p, v_s) = kv(W_A), kv(2 * W_A)
    return x, (k_p, v_p, k_s, v_s, new_vb, hr_p, hi_p, hr_s, hi_s)


def kernel(x_prompt, x_sample, cache_a_k, cache_a_v, state_c_re, state_c_im, c_prompt, c_sample, rel_bias, w_ada, b_ada, g_attn, w_in, g_bv, b_bv, w_s, b_s, lam_re, lam_im, log_dt, ssm_b_re, ssm_b_im, ssm_c_re, ssm_c_im, ssm_d, w_glu, b_glu, g_mix, w_out, g_ffn, w_router, b_router, w_gate, w_up, w_down, w_sh_gate, w_sh_up, w_sh_down, g_final):
    assert x_prompt.shape == (B_P, T_P, D) and x_sample.shape == (B_S, T_S, D)
    assert cache_a_k.shape == (DEPTH, B_S, BUF, H_A, HEAD_DIM)
    per_layer = dict(g_attn=g_attn, w_in=w_in, g_bv=g_bv, b_bv=b_bv, w_s=w_s, b_s=b_s, lam_re=lam_re,
                     lam_im=lam_im, log_dt=log_dt, ssm_b_re=ssm_b_re, ssm_b_im=ssm_b_im, ssm_c_re=ssm_c_re,
                     ssm_c_im=ssm_c_im, ssm_d=ssm_d, w_glu=w_glu, b_glu=b_glu, g_mix=g_mix, w_out=w_out,
                     g_ffn=g_ffn, w_router=w_router, b_router=b_router, w_sh_gate=w_sh_gate, w_sh_up=w_sh_up,
                     w_sh_down=w_sh_down)

    x = _pad_rows(jnp.concatenate([x_prompt.reshape(N_PROMPT, D), x_sample.reshape(N_SAMPLE, D)], axis=0), NP)
    c_all = _pad_rows(jnp.concatenate([c_prompt, c_sample], axis=0), 16)
    mod = _mod_all(c_all, w_ada, b_ada)
    bias_p = _bias_bank(rel_bias, *_prompt_bias_tables())
    bias_s = _bias_bank(rel_bias, *_sample_bias_tables())[0]
    cache_k = cache_a_k.reshape(DEPTH * B_S, BUF, W_A)
    cache_v = cache_a_v.reshape(DEPTH * B_S, BUF, W_A)

    outs = []
    for l in range(DEPTH):
        lw = {name: w[l] for name, w in per_layer.items()}
        x, o = _layer(x, l, lw, (w_gate, w_up, w_down), mod[l], bias_p, bias_s, cache_k, cache_v,
                      state_c_re[l], state_c_im[l])
        outs.append(o)
    y_p, y_s = _finalnorm(x, g_final)
    stacked = [jnp.stack([o[j] for o in outs]) for j in range(9)]
    return (y_p.reshape(B_P, T_P, D), y_s[:N_SAMPLE].reshape(B_S, T_S, D), *stacked)
```

```python
import functools
import math

import jax
import jax.numpy as jnp
import numpy as np
from jax import lax
from jax.experimental import pallas as pl
from jax.experimental.pallas import tpu as pltpu

F32 = jnp.float32
BF16 = jnp.bfloat16
I32 = jnp.int32

D = 2048
B_P, T_P = 2, 4096
B_S, T_S = 8, 8
DEPTH = 2
HEAD_DIM = 64
W_A = 1024
H_A = 16
W_B = 512
G_B = 8
W_C = 512
SSM_GROUP = 16
G_C = 32
SSM_STATE = 64
IN_COLS = 3 * W_A + 2 * W_B + W_C
DILATIONS = (1, 4, 16)
NK = 128
ATTN_UNITS = (8, 8, 8)
BUF = 2048
CHUNK = 128
N_BUCKETS = 32
REL_MAX_DIST = 2048
N_EXPERTS = 64
TOP_K = 6
D_EXPERT = 512
D_SHARED = 1024
ROUTE_SCALE = 2.5
EPS = 1e-6

N_PROMPT = B_P * T_P
N_SAMPLE = B_S * T_S
N_REAL = N_PROMPT + N_SAMPLE
TM = 512
NT = N_PROMPT // TM + 1
NP = NT * TM
TD = 256
N_DTILES = -(-N_REAL // TD)
ROW_TILES = D // 128
BM = 256
BM_SHIFT = 8
NB = -(-(N_REAL * TOP_K + N_EXPERTS * (BM - 1)) // BM)
NBP = -(-NB // 8) * 8
CAP = NB * BM
TC_ = 128
KF_ROWS = 2176
S5_L_P = 16
S5_L_S = T_S
S5_GPB = 128 // SSM_GROUP
S5_NLB = G_C // S5_GPB
S5_SW = S5_GPB * SSM_STATE
BIG = 1e9
assert BM == 1 << BM_SHIFT

VMEM_LIMIT_BYTES = 56 * 1024 * 1024


def _params(dims):
    return pltpu.CompilerParams(dimension_semantics=dims, vmem_limit_bytes=VMEM_LIMIT_BYTES)


def _dot(a, b):
    return jnp.dot(a, b, preferred_element_type=F32)


def _dot_nt(a, b, precision=None):
    return lax.dot_general(a, b, (((1,), (1,)), ((), ())), preferred_element_type=F32,
                           precision=precision)


def _rms(x, g):
    return x * lax.rsqrt(jnp.mean(x * x, axis=-1, keepdims=True) + EPS) * g


def _mod_kernel(c_ref, w_ref, b_ref, o_ref):
    c = c_ref[...]
    s = (c * jax.nn.sigmoid(c)).astype(BF16)
    o_ref[...] = _dot(s, w_ref[...].astype(BF16)) + b_ref[...]


def _mod_all(c_all, w_ada, b_ada):
    tn = 1024
    return pl.pallas_call(
        _mod_kernel,
        grid=(DEPTH, 6 * D // tn),
        in_specs=[pl.BlockSpec((16, D), lambda l, j: (0, 0)),
                  pl.BlockSpec((None, D, tn), lambda l, j: (l, 0, j)),
                  pl.BlockSpec((None, 1, tn), lambda l, j: (l, 0, j))],
        out_specs=pl.BlockSpec((None, 16, tn), lambda l, j: (l, 0, j)),
        out_shape=jax.ShapeDtypeStruct((DEPTH, 16, 6 * D), F32),
        compiler_params=_params(("arbitrary", "arbitrary")),
        name="mod",
    )(c_all, w_ada, b_ada.reshape(DEPTH, 1, 6 * D))


def _tile_tabs(m):
    row = jnp.concatenate([jnp.repeat(m[0:B_P], NT // B_P, axis=0), jnp.zeros((1, m.shape[1]), F32)], 0)
    tok = jnp.concatenate([jnp.repeat(m[B_P:B_P + B_S], T_S, axis=0),
                           jnp.zeros((TM - N_SAMPLE, m.shape[1]), F32)], 0)
    return row[:, None, :], tok


def _pick(last, tok_ref, row_ref):
    return jnp.where(last, tok_ref[...], row_ref[...])


def _modnorm_kernel(x_ref, g_ref, scr_ref, sct_ref, shr_ref, sht_ref, ob_ref, ot_ref):
    last = pl.program_id(0) == NT - 1
    y = _rms(x_ref[...], g_ref[...])
    y = y * (1.0 + _pick(last, sct_ref, scr_ref)) + _pick(last, sht_ref, shr_ref)
    ob_ref[...] = y.astype(BF16)
    _store_token_rows(ot_ref, y)


def _load_token_rows(ref, n, first=0):
    return jnp.concatenate([ref[pl.ds(first * ROW_TILES + j, n, stride=ROW_TILES), :] for j in range(ROW_TILES)],
                           axis=1)


def _store_token_rows(ref, val, first=0):
    for j in range(ROW_TILES):
        ref[pl.ds(first * ROW_TILES + j, val.shape[0], stride=ROW_TILES), :] = val[:, j * 128:(j + 1) * 128]


def _modnorm(x, g, sc, sh):
    row = pl.BlockSpec((None, 1, D), lambda i: (i, 0, 0))
    tok = pl.BlockSpec((TM, D), lambda i: (0, 0))
    til = pl.BlockSpec((TM, D), lambda i: (i, 0))
    return pl.pallas_call(
        _modnorm_kernel,
        grid=(NT,),
        in_specs=[til, pl.BlockSpec((1, D), lambda i: (0, 0)), row, tok, row, tok],
        out_specs=[til, pl.BlockSpec((TM * ROW_TILES, 128), lambda i: (i, 0))],
        out_shape=[jax.ShapeDtypeStruct((NP, D), BF16), jax.ShapeDtypeStruct((NP * ROW_TILES, 128), F32)],
        compiler_params=_params(("arbitrary",)),
        name="modnorm",
    )(x, g.reshape(1, D), sc[0], sc[1], sh[0], sh[1])


def _finalnorm_kernel(x_ref, g_ref, op_ref, os_ref):
    i = pl.program_id(0)
    y = _rms(x_ref[...], g_ref[...])

    @pl.when(i < NT - 1)
    def _():
        op_ref[...] = y

    @pl.when(i == NT - 1)
    def _():
        os_ref[...] = y


def _finalnorm(x, g):
    return pl.pallas_call(
        _finalnorm_kernel,
        grid=(NT,),
        in_specs=[pl.BlockSpec((TM, D), lambda i: (i, 0)), pl.BlockSpec((1, D), lambda i: (0, 0))],
        out_specs=[pl.BlockSpec((TM, D), lambda i: (jnp.minimum(i, NT - 2), 0)),
                   pl.BlockSpec((TM, D), lambda i: (0, 0))],
        out_shape=[jax.ShapeDtypeStruct((N_PROMPT, D), F32), jax.ShapeDtypeStruct((TM, D), F32)],
        compiler_params=_params(("arbitrary",)),
        name="finalnorm",
    )(x, g.reshape(1, D))


def _mm_norm_kernel(x_ref, g_ref, scr_ref, sct_ref, shr_ref, sht_ref, b_ref, o_ref):
    last = pl.program_id(1) == NT - 1
    y = _rms(x_ref[...], g_ref[...])
    y = y * (1.0 + _pick(last, sct_ref, scr_ref)) + _pick(last, sht_ref, shr_ref)
    o_ref[...] = _dot(y.astype(BF16), b_ref[...])


def _matmul_norm(x, g, sc, sh, b, tn):
    k, n = b.shape
    row = pl.BlockSpec((None, 1, D), lambda c, i: (i, 0, 0))
    tok = pl.BlockSpec((TM, D), lambda c, i: (0, 0))
    return pl.pallas_call(
        _mm_norm_kernel,
        grid=(n // tn, NT),
        in_specs=[pl.BlockSpec((TM, D), lambda c, i: (i, 0)), pl.BlockSpec((1, D), lambda c, i: (0, 0)),
                  row, tok, row, tok, pl.BlockSpec((k, tn), lambda c, i: (0, c))],
        out_specs=pl.BlockSpec((TM, tn), lambda c, i: (i, c)),
        out_shape=jax.ShapeDtypeStruct((NP, n), F32),
        compiler_params=_params(("arbitrary", "arbitrary")),
        name="matmul_norm",
    )(x, g.reshape(1, D), sc[0], sc[1], sh[0], sh[1], b)


def _mm_mix_kernel(ap_ref, bp_ref, cp_ref, as_ref, bs_ref, cs_ref, gm_ref, w_ref, x_ref, gr_ref, gt_ref, o_ref):
    last = pl.program_id(1) == NT - 1
    a = jnp.where(last, as_ref[...], ap_ref[...])
    b = jnp.where(last, bs_ref[...], bp_ref[...])
    c = jnp.where(last, cs_ref[...], cp_ref[...])
    mixed = jnp.concatenate([_rms(a, gm_ref[...]).astype(BF16), b.astype(BF16), c.astype(BF16)], axis=1)
    o_ref[...] = x_ref[...] + _pick(last, gt_ref, gr_ref) * _dot(mixed, w_ref[...])


def _matmul_mix_residual(a_p, b_p, c_p, a_s, b_s, c_s, g_mix_a, w, x, gate, tn):
    pr = lambda wd: pl.BlockSpec((TM, wd), lambda c, i: (jnp.minimum(i, NT - 2), 0))
    sm = lambda wd: pl.BlockSpec((TM, wd), lambda c, i: (0, 0))
    return pl.pallas_call(
        _mm_mix_kernel,
        grid=(D // tn, NT),
        in_specs=[pr(W_A), pr(W_B), pr(W_C), sm(W_A), sm(W_B), sm(W_C), pl.BlockSpec((1, W_A), lambda c, i: (0, 0)),
                  pl.BlockSpec((D, tn), lambda c, i: (0, c)),
                  pl.BlockSpec((TM, tn), lambda c, i: (i, c)),
                  pl.BlockSpec((None, 1, tn), lambda c, i: (i, 0, c)),
                  pl.BlockSpec((TM, tn), lambda c, i: (0, c))],
        out_specs=pl.BlockSpec((TM, tn), lambda c, i: (i, c)),
        out_shape=jax.ShapeDtypeStruct((NP, D), F32),
        compiler_params=_params(("arbitrary", "arbitrary")),
        name="matmul_mix_residual",
    )(a_p, b_p, c_p, a_s, b_s, c_s, g_mix_a.reshape(1, W_A), w, x, gate[0], gate[1])


def _t5_bucket(dist):
    max_exact = N_BUCKETS // 2
    dist = np.maximum(dist, 0)
    ratio = (np.log(np.maximum(dist, 1).astype(np.float32) / np.float32(max_exact))
             / np.float32(math.log(REL_MAX_DIST / max_exact)))
    large = np.minimum(max_exact + (ratio * np.float32(N_BUCKETS - max_exact)).astype(np.int32), N_BUCKETS - 1)
    return np.where(dist < max_exact, dist, large)


def _bias_kernel(rb_ref, bucket_ref, add_ref, o_ref):
    h = pl.program_id(1)
    bucket = bucket_ref[...]
    acc = jnp.full(bucket.shape, -jnp.inf, F32)
    for b in range(N_BUCKETS):
        acc = jnp.where(bucket == b, rb_ref[b * H_A + h], acc)
    o_ref[...] = acc + add_ref[...]


def _bias_bank(rel_bias, bucket, add):
    n, r, c = bucket.shape
    blk = pl.BlockSpec((None, r, c), lambda i, h, rb: (i, 0, 0))
    return pl.pallas_call(
        _bias_kernel,
        grid_spec=pltpu.PrefetchScalarGridSpec(
            num_scalar_prefetch=1, grid=(n, H_A), in_specs=[blk, blk],
            out_specs=pl.BlockSpec((None, None, r, c), lambda i, h, rb: (i, h, 0, 0))),
        out_shape=jax.ShapeDtypeStruct((n, H_A, r, c), F32),
        compiler_params=_params(("arbitrary", "arbitrary")),
        name="bias_bank",
    )(rel_bias.reshape(-1), bucket, add)


def _prompt_bias_tables():
    qi = np.arange(NK)[:, None]
    kj = np.arange(2 * NK)[None, :]
    rel = qi + NK - kj
    band = (rel >= 0) & (rel <= NK)
    bucket = np.stack([np.where(band, _t5_bucket(rel * d), -1) for d in DILATIONS]).astype(np.int32)
    return jnp.asarray(bucket), jnp.zeros(bucket.shape, F32)


def _sample_bias_tables():
    dist_np = BUF + np.arange(T_S)[:, None] - np.arange(KF_ROWS)[None, :]
    mult = np.zeros(dist_np.shape, np.float64)
    for d in DILATIONS:
        mult += (dist_np >= 0) & (dist_np % d == 0) & (dist_np // d <= NK)
    mult = np.where(np.arange(KF_ROWS)[None, :] < BUF + T_S, mult, 0.0)
    valid = mult > 0
    add = np.where(valid, np.log(np.maximum(mult, 1.0)), 0.0).astype(np.float32)
    bucket = np.where(valid, _t5_bucket(dist_np), -1).astype(np.int32)
    return jnp.asarray(bucket)[None], jnp.asarray(add)[None]


def _attn_prompt_kernel(q_ref, k_ref, v_ref, bias_ref, o_ref, *stats):
    lane = lax.broadcasted_iota(I32, (NK, 128), 1)
    head0 = lane < HEAD_DIM
    prev_cols = lax.broadcasted_iota(I32, (NK, 2 * NK), 1) < NK
    ones = jnp.ones((NK, 128), F32)
    n_br = len(DILATIONS)
    m_refs, l_refs, a_refs = stats[0:n_br], stats[n_br:2 * n_br], stats[2 * n_br:3 * n_br]
    for di in range(n_br):
        d = DILATIONS[di]
        n_blocks = T_P // (d * NK)
        n_units = ATTN_UNITS[di]
        run = min(n_units, n_blocks)
        assert n_units % run == 0 and n_blocks % run == 0

        def group(gi, carry, di=di, d=d, n_blocks=n_blocks, n_units=n_units, run=run):
            def idx(start):
                return pl.ds(start, NK) if d == 1 else pl.ds(start, NK, stride=d)

            def keys(start):
                return k_ref[idx(start), :].astype(BF16)

            def values(start):
                return jnp.concatenate([v_ref[idx(start), :], ones], axis=1).astype(BF16)

            units = []
            for j in range(n_units):
                u = gi * n_units + j
                r = u >> int(math.log2(n_blocks))
                n = u & (n_blocks - 1)
                cur = r + d * NK * n
                if j % run == 0:
                    prv = r + d * NK * jnp.maximum(n - 1, 0)
                    k_prev, v_prev = keys(prv), values(prv)
                    pen = jnp.where(n == 0, -jnp.inf, 0.0).astype(F32)
                else:
                    k_prev, v_prev, pen = k_cur, v_cur, None
                k_cur, v_cur = keys(cur), values(cur)
                units.append(dict(cur=cur, pen=pen, q=q_ref[idx(cur), :] * (HEAD_DIM ** -0.5),
                                  k2=jnp.concatenate([k_prev, k_cur], axis=0),
                                  v2=jnp.concatenate([v_prev, v_cur], axis=0)))
            for unit in units:
                pv, rs, mn = [], [], []
                for h in range(2):
                    qh = jnp.where(head0 if h == 0 else ~head0, unit["q"], 0.0).astype(BF16)
                    s = _dot_nt(qh, unit["k2"]) + bias_ref[di, h]
                    if unit["pen"] is not None:
                        s = s + jnp.where(prev_cols, unit["pen"], 0.0)
                    m_h = jnp.max(s, axis=-1, keepdims=True)
                    pv_rs = _dot(jnp.exp(s - m_h).astype(BF16), unit["v2"])
                    pv.append(pv_rs[:, 0:128])
                    rs.append(pv_rs[:, 128:256])
                    mn.append(m_h)
                m_refs[di][idx(unit["cur"]), :] = jnp.where(head0, mn[0], mn[1])
                l_refs[di][idx(unit["cur"]), :] = jnp.where(head0, rs[0], rs[1])
                a_refs[di][idx(unit["cur"]), :] = jnp.where(head0, pv[0], pv[1])
            return carry

        lax.fori_loop(0, T_P // NK // n_units, group, 0)

    def merge(bi, carry):
        rows = pl.ds(pl.multiple_of(bi * NK, NK), NK)
        ms = [m_ref[rows, :] for m_ref in m_refs]
        m = functools.reduce(jnp.maximum, ms)
        num = jnp.zeros((NK, 128), F32)
        den = jnp.zeros((NK, 128), F32)
        for m_i, l_ref, a_ref in zip(ms, l_refs, a_refs):
            w = jnp.exp(m_i - m)
            num = num + w * a_ref[rows, :]
            den = den + w * l_ref[rows, :]
        o_ref[rows, :] = num / den
        return carry

    lax.fori_loop(0, T_P // NK, merge, 0, unroll=2)


def _attn_prompt(z, bias):
    blk = lambda off: pl.BlockSpec((T_P, 128), lambda b, h, off=off: (b, off + h))
    return pl.pallas_call(
        _attn_prompt_kernel,
        grid=(B_P, H_A // 2),
        in_specs=[blk(0), blk(W_A // 128), blk(2 * W_A // 128),
                  pl.BlockSpec((len(DILATIONS), 2, NK, 2 * NK), lambda b, h: (0, h, 0, 0))],
        out_specs=pl.BlockSpec((T_P, 128), lambda b, h: (b, h)),
        out_shape=jax.ShapeDtypeStruct((N_PROMPT, W_A), F32),
        scratch_shapes=[pltpu.VMEM((T_P, 128), F32)] * (3 * len(DILATIONS)),
        compiler_params=_params(("arbitrary", "arbitrary")),
        name="attn_prompt",
    )(z, z, z, bias)


def _attn_sample_kernel(q_ref, kn_ref, vn_ref, ck_ref, cv_ref, bias_ref, o_ref, kf_ref, vf_ref):
    for f_ref, c_ref, n_ref in ((kf_ref, ck_ref, kn_ref), (vf_ref, cv_ref, vn_ref)):
        f_ref[0:BUF, :] = c_ref[...]
        f_ref[BUF:BUF + T_S, :] = n_ref[...]
        f_ref[BUF + T_S:KF_ROWS, :] = jnp.zeros((KF_ROWS - BUF - T_S, 128), F32)
    head0 = lax.broadcasted_iota(I32, (T_S, 128), 1) < HEAD_DIM
    q = q_ref[...] * (HEAD_DIM ** -0.5)
    q2 = jnp.concatenate([jnp.where(head0, q, 0.0), jnp.where(head0, 0.0, q)], axis=0).astype(BF16)
    bias2 = jnp.concatenate([bias_ref[0], bias_ref[1]], axis=0)
    s = _dot_nt(q2, kf_ref[...].astype(BF16)) + bias2
    m = jnp.max(s, axis=-1, keepdims=True)
    p = jnp.exp(s - m)
    l = jnp.sum(p, axis=-1, keepdims=True)
    o2 = _dot(p.astype(BF16), vf_ref[...].astype(BF16)) / l
    o_ref[...] = jnp.where(head0, o2[0:T_S], o2[T_S:2 * T_S])


def _attn_sample(z, cache_k, cache_v, layer, bias):
    row0 = N_PROMPT // T_S
    blk = lambda off: pl.BlockSpec((T_S, 128), lambda b, h, off=off: (row0 + b, off + h))
    cache = pl.BlockSpec((None, BUF, 128), lambda b, h: (layer * B_S + b, 0, h))
    return pl.pallas_call(
        _attn_sample_kernel,
        grid=(B_S, H_A // 2),
        in_specs=[blk(0), blk(W_A // 128), blk(2 * W_A // 128), cache, cache,
                  pl.BlockSpec((2, T_S, KF_ROWS), lambda b, h: (h, 0, 0))],
        out_specs=pl.BlockSpec((T_S, 128), lambda b, h: (b, h)),
        out_shape=jax.ShapeDtypeStruct((N_SAMPLE, W_A), F32),
        scratch_shapes=[pltpu.VMEM((KF_ROWS, 128), F32)] * 2,
        compiler_params=_params(("arbitrary", "arbitrary")),
        name="attn_sample",
    )(z, z, z, cache_k, cache_v, bias)


def _gmlp_kernel(ub_ref, vb_ref, gbv_ref, bbv_ref, ws_ref, bs_ref, gmix_ref, bo_ref, vn_ref):
    v = jax.nn.gelu(vb_ref[...])
    mu = jnp.mean(v, axis=-1, keepdims=True)
    var = jnp.mean(jnp.square(v - mu), axis=-1, keepdims=True)
    vn = (v - mu) * lax.rsqrt(var + EPS) * gbv_ref[...] + bbv_ref[...]
    vn_ref[...] = vn
    vnb = vn.astype(BF16)
    causal = (lax.broadcasted_iota(I32, (CHUNK, CHUNK), 1) <= lax.broadcasted_iota(I32, (CHUNK, CHUNK), 0))
    first = lax.broadcasted_iota(I32, (CHUNK, 128), 1) < W_B // G_B
    parts = []
    for gp in range(G_B // 2):
        v2 = vnb[:, gp * 128:(gp + 1) * 128]
        m0 = _dot(jnp.where(causal, ws_ref[2 * gp], 0.0).astype(BF16), v2)
        m1 = _dot(jnp.where(causal, ws_ref[2 * gp + 1], 0.0).astype(BF16), v2)
        parts.append(jnp.where(first, m0, m1))
    mix = jnp.concatenate(parts, axis=1) + bs_ref[...]
    bo_ref[...] = _rms(jax.nn.gelu(ub_ref[...]) * mix, gmix_ref[...])


def _gmlp(z, n_chunks, col_u, col_v, g_bv, b_bv, w_s, b_s, g_mix_b):
    bs_wide = jnp.repeat(b_s.T, W_B // G_B, axis=1)
    vec = pl.BlockSpec((1, W_B), lambda i: (0, 0))
    out = pl.BlockSpec((CHUNK, W_B), lambda i: (i, 0))
    return pl.pallas_call(
        _gmlp_kernel,
        grid=(n_chunks,),
        in_specs=[pl.BlockSpec((CHUNK, W_B), lambda i: (i, col_u)),
                  pl.BlockSpec((CHUNK, W_B), lambda i: (i, col_v)),
                  vec, vec,
                  pl.BlockSpec((G_B, CHUNK, CHUNK), lambda i: (0, 0, 0)),
                  pl.BlockSpec((CHUNK, W_B), lambda i: (0, 0)),
                  vec],
        out_specs=[out, out],
        out_shape=[jax.ShapeDtypeStruct((n_chunks * CHUNK, W_B), F32)] * 2,
        compiler_params=_params(("arbitrary",)),
        name="gmlp",
    )(z, z, g_bv.reshape(1, W_B), b_bv.reshape(1, W_B), w_s, bs_wide, g_mix_b.reshape(1, W_B))


def _cmul(a, b):
    return a[0] * b[0] - a[1] * b[1], a[0] * b[1] + a[1] * b[0]


def _s5_kernel(u_ref, lr_ref, li_ref, ldt_ref, btr_ref, bti_ref, cr_ref, ci_ref, d_ref, h0r_ref, h0i_ref,
               y_ref, hr_ref, hi_ref, er_ref, ei_ref, gr_ref, gi_ref, *, L, nj, nsc, nb):
    hi = lax.Precision.HIGHEST
    lr, li = lr_ref[...], li_ref[...]
    dt = jnp.exp(ldt_ref[...])
    mag = jnp.exp(lr * dt)
    a = (mag * jnp.cos(li * dt), mag * jnp.sin(li * dt))
    den = lr * lr + li * li
    q = (((a[0] - 1.0) * lr + a[1] * li) / den, (a[1] * lr - (a[0] - 1.0) * li) / den)
    same = ((lax.broadcasted_iota(I32, (128, S5_SW), 0) >> int(math.log2(SSM_GROUP)))
            == (lax.broadcasted_iota(I32, (128, S5_SW), 1) >> int(math.log2(SSM_STATE))))
    bb = _cmul(q, (jnp.where(same, btr_ref[...], 0.0), jnp.where(same, bti_ref[...], 0.0)))
    c = (jnp.where(same, cr_ref[...], 0.0), jnp.where(same, ci_ref[...], 0.0))
    pw = [(jnp.ones_like(lr), jnp.zeros_like(lr))]
    for _ in range(L):
        pw.append(_cmul(pw[-1], a))
    ck = [_cmul(c, p) for p in pw]
    kbd = [(_dot_nt(bb[0], ck[k][0], hi) - _dot_nt(bb[1], ck[k][1], hi)).astype(BF16) for k in range(L)]
    mins = [tuple(x.astype(BF16) for x in _cmul(pw[L - 1 - s], bb)) for s in range(L)]
    cob = [tuple(x.astype(BF16) for x in ck[i + 1]) for i in range(L)]

    us = [u_ref[:, s * 128:(s + 1) * 128] for s in range(L)]
    ub = [x.astype(BF16) for x in us]
    sr = _dot(ub[0], mins[0][0])
    si = _dot(ub[0], mins[0][1])
    for s in range(1, L):
        sr = sr + _dot(ub[s], mins[s][0])
        si = si + _dot(ub[s], mins[s][1])
    al = pw[L]
    r = nsc * nb
    xl = [(sr[0:r], si[0:r])]
    for j in range(1, nj):
        t = _cmul(al, xl[-1])
        xl.append((t[0] + sr[j * r:(j + 1) * r], t[1] + si[j * r:(j + 1) * r]))
    alp = [al]
    for j in range(1, nj):
        alp.append(_cmul(alp[-1], al))
    er_ref[...] = xl[-1][0]
    ei_ref[...] = xl[-1][1]
    g = (h0r_ref[...], h0i_ref[...])
    for sc in range(nsc):
        gr_ref[sc * nb:(sc + 1) * nb, :] = g[0]
        gi_ref[sc * nb:(sc + 1) * nb, :] = g[1]
        t = _cmul(alp[nj - 1], g)
        g = (t[0] + er_ref[sc * nb:(sc + 1) * nb, :], t[1] + ei_ref[sc * nb:(sc + 1) * nb, :])
    hr_ref[...] = g[0]
    hi_ref[...] = g[1]
    gcat = (gr_ref[...], gi_ref[...])
    hprev = [gcat]
    for j in range(1, nj):
        t = _cmul(alp[j - 1], gcat)
        hprev.append((xl[j - 1][0] + t[0], xl[j - 1][1] + t[1]))
    hpr = jnp.concatenate([x[0] for x in hprev], axis=0).astype(BF16)
    hpi = jnp.concatenate([x[1] for x in hprev], axis=0).astype(BF16)
    y = (jnp.concatenate([d_ref[...]] * L, axis=1) * u_ref[...]
         + _dot_nt(hpr, jnp.concatenate([x[0] for x in cob], axis=0))
         - _dot_nt(hpi, jnp.concatenate([x[1] for x in cob], axis=0)))
    for s in range(L):
        conv = _dot(ub[s], jnp.concatenate(kbd[0:L - s], axis=1))
        y = y + conv if s == 0 else jnp.concatenate([y[:, :s * 128], y[:, s * 128:] + conv], axis=1)
    y_ref[...] = y


def _s5(u, lw, h0r, h0i, L, nj, nsc, nb):
    m, n = u.shape[1], u.shape[2]
    lanes = lambda x: x.reshape(S5_NLB, 1, S5_SW)
    rep = lambda x: jnp.tile(x.reshape(S5_NLB, 128, SSM_STATE), (1, 1, S5_GPB))
    vec = pl.BlockSpec((None, 1, S5_SW), lambda g: (g, 0, 0))
    mat = pl.BlockSpec((None, 128, S5_SW), lambda g: (g, 0, 0))
    st = pl.BlockSpec((None, nb, S5_SW), lambda g: (g, 0, 0))
    return pl.pallas_call(
        functools.partial(_s5_kernel, L=L, nj=nj, nsc=nsc, nb=nb),
        grid=(S5_NLB,),
        in_specs=[pl.BlockSpec((None, m, n), lambda g: (g, 0, 0)), vec, vec, vec, mat, mat, mat, mat,
                  pl.BlockSpec((None, 1, 128), lambda g: (g, 0, 0)), st, st],
        out_specs=[pl.BlockSpec((None, m, n), lambda g: (g, 0, 0)), st, st],
        out_shape=[jax.ShapeDtypeStruct((S5_NLB, m, n), F32)] + [jax.ShapeDtypeStruct((S5_NLB, nb, S5_SW), F32)] * 2,
        scratch_shapes=[pltpu.VMEM((nsc * nb, S5_SW), F32)] * 4,
        compiler_params=_params(("arbitrary",)),
        name="s5",
    )(u, lanes(lw["lam_re"]), lanes(lw["lam_im"]), lanes(jnp.repeat(lw["log_dt"], SSM_STATE)),
      rep(lw["ssm_b_re"].transpose(0, 2, 1)), rep(lw["ssm_b_im"].transpose(0, 2, 1)),
      rep(lw["ssm_c_re"]), rep(lw["ssm_c_im"]), lw["ssm_d"].reshape(S5_NLB, 1, 128), h0r, h0i)


def _s5_post_kernel(y_ref, w_ref, b_ref, g_ref, o_ref):
    zz = jax.nn.gelu(y_ref[...])
    out = zz * jax.nn.sigmoid(_dot(zz.astype(BF16), w_ref[...]) + b_ref[...])
    o_ref[...] = _rms(out, g_ref[...])


def _s5_post(y, w_glu, b_glu, g_mix_c):
    n = y.shape[0]
    tt = min(n, TM)
    vec = pl.BlockSpec((1, W_C), lambda i: (0, 0))
    return pl.pallas_call(
        _s5_post_kernel,
        grid=(n // tt,),
        in_specs=[pl.BlockSpec((tt, W_C), lambda i: (i, 0)), pl.BlockSpec((W_C, W_C), lambda i: (0, 0)), vec, vec],
        out_specs=pl.BlockSpec((tt, W_C), lambda i: (i, 0)),
        out_shape=jax.ShapeDtypeStruct((n, W_C), F32),
        compiler_params=_params(("arbitrary",)),
        name="s5_post",
    )(y, w_glu.astype(BF16), b_glu.reshape(1, W_C), g_mix_c.reshape(1, W_C))


def _router_kernel(h_ref, w_ref, b_ref, ind_ref, wd_ref, cnt_ref):
    i = pl.program_id(0)
    scores = jax.nn.sigmoid(_dot(h_ref[...], w_ref[...]))
    lane = lax.broadcasted_iota(I32, (TD, N_EXPERTS), 1).astype(F32)
    row = lax.broadcasted_iota(I32, (TD, N_EXPERTS), 0) + i * TD
    sel = scores + b_ref[...]
    ind = jnp.zeros((TD, N_EXPERTS), F32)
    for _ in range(TOP_K):
        m = jnp.max(sel, axis=-1, keepdims=True)
        idx = jnp.min(jnp.where(sel == m, lane, float(N_EXPERTS)), axis=-1, keepdims=True)
        hit = lane == idx
        ind = jnp.where(hit, 1.0, ind)
        sel = jnp.where(hit, -jnp.inf, sel)
    ind = jnp.where(row < N_REAL, ind, 0.0)
    wsel = ind * scores
    wd_ref[...] = wsel / jnp.sum(wsel + (1.0 - jnp.max(ind, axis=-1, keepdims=True)) / N_EXPERTS,
                                 axis=-1, keepdims=True) * ROUTE_SCALE
    ind_ref[...] = ind

    @pl.when(i == 0)
    def _():
        cnt_ref[...] = jnp.zeros((8, N_EXPERTS), F32)

    cnt_ref[...] += jnp.sum(ind, axis=0, keepdims=True)


def _router(h2b, w_router, b_router):
    til = pl.BlockSpec((TD, N_EXPERTS), lambda i: (i, 0))
    return pl.pallas_call(
        _router_kernel,
        grid=(NP // TD,),
        in_specs=[pl.BlockSpec((TD, D), lambda i: (i, 0)), pl.BlockSpec((D, N_EXPERTS), lambda i: (0, 0)),
                  pl.BlockSpec((1, N_EXPERTS), lambda i: (0, 0))],
        out_specs=[til, til, pl.BlockSpec((8, N_EXPERTS), lambda i: (0, 0))],
        out_shape=[jax.ShapeDtypeStruct((NP, N_EXPERTS), F32)] * 2 + [jax.ShapeDtypeStruct((8, N_EXPERTS), F32)],
        compiler_params=_params(("arbitrary",)),
        name="router",
    )(h2b, w_router.astype(BF16), b_router.reshape(1, N_EXPERTS))


def _group_starts(cnt):
    padded = jnp.floor((cnt + (BM - 1)) / BM) * BM
    tri = (lax.broadcasted_iota(I32, (N_EXPERTS, N_EXPERTS), 0)
           <= lax.broadcasted_iota(I32, (N_EXPERTS, N_EXPERTS), 1)).astype(F32)
    pend = jnp.dot(padded, tri, preferred_element_type=F32, precision=lax.Precision.HIGHEST)
    return padded, pend - padded, pend


def _plan_kernel(cnt_ref, be_ref, misc_ref):
    cnt = cnt_ref[...]
    _, pstart, pend = _group_starts(cnt)
    rowi = (lax.broadcasted_iota(I32, (NBP, N_EXPERTS), 0) * BM).astype(F32)
    be = jnp.minimum(jnp.sum(jnp.where(pend[0:1] <= rowi, 1.0, 0.0), axis=-1, keepdims=True), N_EXPERTS - 1.0)
    be_ref[...] = jnp.broadcast_to(be, (NBP, 128)).astype(I32)
    misc_ref[0:8, :] = pstart.astype(I32)
    misc_ref[8:16, :] = cnt.astype(I32)
    misc_ref[16:24, :] = (pend / BM).astype(I32)


def _plan(cnt):
    return pl.pallas_call(
        _plan_kernel,
        out_shape=[jax.ShapeDtypeStruct((NBP, 128), I32), jax.ShapeDtypeStruct((24, N_EXPERTS), I32)],
        name="moe_plan",
    )(cnt)


def _slots_kernel(ind_ref, wd_ref, cnt_ref, dest_ref, w_ref, run_ref):
    @pl.when(pl.program_id(0) == 0)
    def _():
        run_ref[...] = jnp.zeros((8, N_EXPERTS), F32)

    ind = ind_ref[...]
    _, pstart, _ = _group_starts(cnt_ref[...])
    ltri = (lax.broadcasted_iota(I32, (TD, TD), 1) < lax.broadcasted_iota(I32, (TD, TD), 0)).astype(BF16)
    pos = _dot(ltri, ind.astype(BF16)) + run_ref[0:1, :]
    run_ref[...] += jnp.sum(ind, axis=0, keepdims=True)
    cur = jnp.where(ind > 0.0, pstart[0:1] + pos, BIG)
    lane = lax.broadcasted_iota(I32, (TD, 128), 1)
    dest = jnp.full((TD, 128), -1.0, F32)
    wts = jnp.zeros((TD, 128), F32)
    wd = wd_ref[...]
    for k in range(TOP_K):
        m = jnp.min(cur, axis=-1, keepdims=True)
        hit = cur == m
        wk = jnp.sum(jnp.where(hit, wd, 0.0), axis=-1, keepdims=True)
        dest = jnp.where(lane == k, jnp.where(m < BIG, m, -1.0), dest)
        wts = jnp.where(lane == k, jnp.where(m < BIG, wk, 0.0), wts)
        cur = jnp.where(hit, BIG, cur)
    dest_ref[...] = dest.astype(I32)
    w_ref[...] = wts


def _slots(ind, wd, cnt):
    til = pl.BlockSpec((TD, N_EXPERTS), lambda i: (i, 0))
    out = pl.BlockSpec((TD, 128), lambda i: (i, 0))
    return pl.pallas_call(
        _slots_kernel,
        grid=(NP // TD,),
        in_specs=[til, til, pl.BlockSpec((8, N_EXPERTS), lambda i: (0, 0))],
        out_specs=[out, out],
        out_shape=[jax.ShapeDtypeStruct((NP, 128), I32), jax.ShapeDtypeStruct((NP, 128), F32)],
        scratch_shapes=[pltpu.VMEM((8, N_EXPERTS), F32)],
        compiler_params=_params(("arbitrary",)),
        name="moe_slots",
    )(ind, wd, cnt)


def _dispatch_kernel(ps_ref, cnt_ref, nbu_ref, dest_ref, x_ref, h_ref, wg_ref, wu_ref, wd_ref,
                     xs_ref, sh_ref, zero_ref, sem):
    i = pl.program_id(0)

    def shared_expert():
        h = h_ref[...]
        g = _dot(h, wg_ref[...])
        u = _dot(h, wu_ref[...])
        sh_ref[...] = _dot((g * jax.nn.sigmoid(g) * u).astype(BF16), wd_ref[...])

    def token(ref, t):
        start = t * ROW_TILES if isinstance(t, int) else pl.multiple_of(t * ROW_TILES, ROW_TILES)
        return ref.at[pl.ds(start, ROW_TILES), :]

    def row_copy(r, k):
        return pltpu.make_async_copy(token(x_ref, r), token(xs_ref, dest_ref[r * 8 + k]), sem)

    def zero_rows(d, n):
        rows = n * ROW_TILES
        return pltpu.make_async_copy(zero_ref.at[pl.ds(0, rows), :],
                                     xs_ref.at[pl.ds(pl.multiple_of(d * ROW_TILES, ROW_TILES), rows), :], sem)

    def zero_block(b):
        rows = BM * ROW_TILES
        return pltpu.make_async_copy(zero_ref, xs_ref.at[pl.ds(pl.multiple_of(b * rows, rows), rows), :], sem)

    @pl.when(i < N_DTILES)
    def _():
        n_rows = jnp.minimum(TD, N_REAL - i * TD)

        def issue(r, c):
            for k in range(TOP_K):
                row_copy(r, k).start(priority=k % 2)
            return c

        def drain(r, c):
            for k in range(TOP_K):
                row_copy(r, k).wait()
            return c

        lax.fori_loop(0, n_rows, issue, 0)
        shared_expert()
        lax.fori_loop(0, n_rows, drain, 0)

    @pl.when(i == N_DTILES)
    def _():
        shared_expert()
        zero_ref[...] = jnp.zeros(zero_ref.shape, F32)

        def per_expert(e, c):
            cnt = cnt_ref[e]
            base = ps_ref[e] + cnt
            rem = (((cnt + (BM - 1)) >> BM_SHIFT) << BM_SHIFT) - cnt
            for start in (True, False):
                for k in range(BM_SHIFT):
                    cp = zero_rows(base + ((rem >> (k + 1)) << (k + 1)), 1 << k)

                    @pl.when((rem & (1 << k)) != 0)
                    def _():
                        cp.start() if start else cp.wait()
            return c

        lax.fori_loop(0, N_EXPERTS, per_expert, 0)
        lax.fori_loop(nbu_ref[0], NB, lambda b, c: (zero_block(b).start(), c)[1], 0)
        lax.fori_loop(nbu_ref[0], NB, lambda b, c: (zero_block(b).wait(), c)[1], 0)


def _dispatch(pstart_i, cnt_i, nbu, dest_flat, x, h2b, wg, wu, wd):
    assert NP // TD == N_DTILES + 1
    whole = lambda r, c: pl.BlockSpec((r, c), lambda i, *_: (0, 0))
    return pl.pallas_call(
        _dispatch_kernel,
        grid_spec=pltpu.PrefetchScalarGridSpec(
            num_scalar_prefetch=3, grid=(N_DTILES + 1,),
            in_specs=[pl.BlockSpec((TD * 8,), lambda i, *_: (jnp.minimum(i, N_DTILES - 1),),
                                   memory_space=pltpu.SMEM),
                      pl.BlockSpec((TD * ROW_TILES, 128), lambda i, *_: (jnp.minimum(i, N_DTILES - 1), 0)),
                      pl.BlockSpec((TD, D), lambda i, *_: (i, 0)),
                      whole(D, D_SHARED), whole(D, D_SHARED), whole(D_SHARED, D)],
            out_specs=[pl.BlockSpec(memory_space=pl.ANY), pl.BlockSpec((TD, D), lambda i, *_: (i, 0))],
            scratch_shapes=[pltpu.VMEM((BM * ROW_TILES, 128), F32), pltpu.SemaphoreType.DMA(())]),
        out_shape=[jax.ShapeDtypeStruct((CAP * ROW_TILES, 128), F32), jax.ShapeDtypeStruct((NP, D), F32)],
        compiler_params=_params(("arbitrary",)),
        name="moe_dispatch",
    )(pstart_i, cnt_i, nbu, dest_flat, x, h2b, wg.astype(BF16), wu.astype(BF16), wd.astype(BF16))


def _expert_kernel(be_ref, nbu_ref, gend_ref, x_ref, wg_hbm, wu_hbm, wd_hbm, o_ref,
                   wg_ref, wu_ref, wd_ref, slot_ref, sem, *, layer):
    i = pl.program_id(0)
    nbu = nbu_ref[0]

    def weight_copies(e, s):
        return [pltpu.make_async_copy(hbm.at[layer, e], buf.at[s], sem.at[s, j])
                for j, (hbm, buf) in enumerate(((wg_hbm, wg_ref), (wu_hbm, wu_ref), (wd_hbm, wd_ref)))]

    @pl.when(i < nbu)
    def _():
        e = be_ref[i]

        @pl.when(i == 0)
        def _():
            slot_ref[0] = 0
            for cp in weight_copies(e, 0):
                cp.start()

        new_group = jnp.logical_or(i == 0, be_ref[jnp.maximum(i - 1, 0)] != e)

        @pl.when(jnp.logical_and(new_group, i > 0))
        def _():
            slot_ref[0] = 1 - slot_ref[0]

        s = slot_ref[0]

        @pl.when(new_group)
        def _():
            for cp in weight_copies(e, s):
                cp.wait()
            nxt = gend_ref[e]

            @pl.when(nxt < nbu)
            def _():
                for cp in weight_copies(be_ref[jnp.minimum(nxt, NB - 1)], 1 - s):
                    cp.start()

        x = _load_token_rows(x_ref, BM).astype(BF16)
        g = _dot(x, wg_ref[s].astype(BF16))
        u = _dot(x, wu_ref[s].astype(BF16))
        a = (g * jax.nn.sigmoid(g) * u).astype(BF16)
        _store_token_rows(o_ref, _dot(a, wd_ref[s].astype(BF16)))

    @pl.when(i >= nbu)
    def _():
        o_ref[...] = jnp.zeros(o_ref.shape, F32)


def _experts(layer, be, nbu, group_end, xs, w_gate, w_up, w_down):
    any_spec = pl.BlockSpec(memory_space=pl.ANY)
    return pl.pallas_call(
        functools.partial(_expert_kernel, layer=layer),
        grid_spec=pltpu.PrefetchScalarGridSpec(
            num_scalar_prefetch=3, grid=(NB,),
            in_specs=[pl.BlockSpec((BM * ROW_TILES, 128), lambda i, be, nbu, ge: (jnp.minimum(i, nbu[0] - 1), 0)),
                      any_spec, any_spec, any_spec],
            out_specs=pl.BlockSpec((BM * ROW_TILES, 128), lambda i, be, nbu, ge: (i, 0)),
            scratch_shapes=[pltpu.VMEM((2, D, D_EXPERT), F32), pltpu.VMEM((2, D, D_EXPERT), F32),
                            pltpu.VMEM((2, D_EXPERT, D), F32), pltpu.SMEM((1,), I32),
                            pltpu.SemaphoreType.DMA((2, 3))]),
        out_shape=jax.ShapeDtypeStruct((CAP * ROW_TILES, 128), F32),
        compiler_params=_params(("arbitrary",)),
        name="moe_experts",
    )(be, nbu, group_end, xs, w_gate, w_up, w_down)


def _combine_kernel(dest_ref, nxt_ref, w_ref, ys_ref, sh_ref, x_ref, gr_ref, gt_ref, o_ref, rows_ref, acc_ref, sem):
    i = pl.program_id(0)
    n_tiles = NP // TC_
    slot = i % 2

    def gather(d_ref, s, start):
        def body(r, c):
            for k in range(TOP_K):
                d = jnp.maximum(d_ref[r * 8 + k], 0)
                cp = pltpu.make_async_copy(
                    ys_ref.at[pl.ds(pl.multiple_of(d * ROW_TILES, ROW_TILES), ROW_TILES), :],
                    rows_ref.at[s, k, pl.ds(pl.multiple_of(r * ROW_TILES, ROW_TILES), ROW_TILES), :], sem.at[s])
                if start:
                    cp.start(priority=k % 2)
                else:
                    cp.wait()
            return c

        lax.fori_loop(0, TC_, body, 0)

    @pl.when(i == 0)
    def _():
        gather(dest_ref, 0, True)

    @pl.when(i + 1 < n_tiles)
    def _():
        gather(nxt_ref, 1 - slot, True)

    gather(dest_ref, slot, False)
    def weigh(t, c):
        rows = pl.ds(pl.multiple_of(t * ROW_TILES, ROW_TILES), ROW_TILES)
        acc = w_ref[t * 8] * rows_ref[slot, 0, rows, :]
        for k in range(1, TOP_K):
            acc = acc + w_ref[t * 8 + k] * rows_ref[slot, k, rows, :]
        acc_ref[rows, :] = acc
        return c

    lax.fori_loop(0, TC_, weigh, 0, unroll=4)
    routed = _load_token_rows(acc_ref, TC_)
    last = i >= N_PROMPT // TC_
    o_ref[...] = x_ref[...] + _pick(last, gt_ref, gr_ref) * (routed + sh_ref[...])


def _combine(dest_flat, w6, ys, shared, x, gate):
    n_tiles = NP // TC_
    n_p = N_PROMPT // TC_
    per = TM // TC_
    til = pl.BlockSpec((TC_, D), lambda i: (i, 0))
    return pl.pallas_call(
        _combine_kernel,
        grid=(n_tiles,),
        in_specs=[pl.BlockSpec((TC_ * 8,), lambda i: (i,), memory_space=pltpu.SMEM),
                  pl.BlockSpec((TC_ * 8,), lambda i: (jnp.minimum(i + 1, n_tiles - 1),), memory_space=pltpu.SMEM),
                  pl.BlockSpec((TC_ * 8,), lambda i: (i,), memory_space=pltpu.SMEM),
                  pl.BlockSpec(memory_space=pl.ANY), til, til,
                  pl.BlockSpec((None, 1, D), lambda i: (i // per, 0, 0)),
                  pl.BlockSpec((TC_, D), lambda i: (jnp.maximum(i - n_p, 0), 0))],
        out_specs=til,
        out_shape=jax.ShapeDtypeStruct((NP, D), F32),
        scratch_shapes=[pltpu.VMEM((2, TOP_K, TC_ * ROW_TILES, 128), F32), pltpu.VMEM((TC_ * ROW_TILES, 128), F32),
                        pltpu.SemaphoreType.DMA((2,))],
        compiler_params=_params(("arbitrary",)),
        name="moe_combine",
    )(dest_flat, dest_flat, w6, ys, shared, x, gate[0], gate[1])


def _pad_rows(a, rows):
    return jnp.concatenate([a, jnp.zeros((rows - a.shape[0],) + a.shape[1:], a.dtype)], axis=0)


def _s5_layer(z, lw, h0r_s, h0i_s):
    g_mix_c = lw["g_mix"][W_A + W_B:]
    nj = nsc = 16
    state = lambda h, b: h.reshape(S5_NLB, b, S5_GPB, SSM_STATE).transpose(1, 0, 2, 3).reshape(b, G_C, SSM_STATE)
    uc_p = z[:N_PROMPT, IN_COLS - W_C:]
    u_p = uc_p.reshape(B_P, nsc, nj, S5_L_P, S5_NLB, 128).transpose(4, 2, 1, 0, 3, 5)
    u_p = u_p.reshape(S5_NLB, nj * nsc * B_P, S5_L_P * 128)
    zeros_h = jnp.zeros((S5_NLB, B_P, S5_SW), F32)
    y_p, hr_p, hi_p = _s5(u_p, lw, zeros_h, zeros_h, S5_L_P, nj, nsc, B_P)
    y_p = y_p.reshape(S5_NLB, nj, nsc, B_P, S5_L_P, 128).transpose(3, 2, 1, 4, 0, 5).reshape(N_PROMPT, W_C)
    c_p = _s5_post(y_p, lw["w_glu"], lw["b_glu"], g_mix_c)

    uc_s = z[N_PROMPT:N_REAL, IN_COLS - W_C:]
    u_s = uc_s.reshape(B_S, T_S, S5_NLB, 128).transpose(2, 0, 1, 3).reshape(S5_NLB, B_S, S5_L_S * 128)
    h0 = lambda h: h.reshape(B_S, S5_NLB, S5_SW).transpose(1, 0, 2)
    y_s, hr_s, hi_s = _s5(u_s, lw, h0(h0r_s), h0(h0i_s), S5_L_S, 1, 1, B_S)
    y_s = y_s.reshape(S5_NLB, B_S, T_S, 128).transpose(1, 2, 0, 3).reshape(N_SAMPLE, W_C)
    c_s = _s5_post(y_s, lw["w_glu"], lw["b_glu"], g_mix_c)
    return c_p, c_s, state(hr_p, B_P), state(hi_p, B_P), state(hr_s, B_S), state(hi_s, B_S)


def _layer(x, l, lw, experts, mod, bias_p, bias_s, cache_k, cache_v, h0r_s, h0i_s):
    sh1, sc1, g1, sh2, sc2, g2 = [_tile_tabs(m) for m in jnp.split(mod, 6, axis=-1)]
    z = _matmul_norm(x, lw["g_attn"], sc1, sh1, lw["w_in"].astype(BF16), 1536)

    a_p = _attn_prompt(z, bias_p)
    a_s = _attn_sample(z, cache_k, cache_v, l, bias_s)

    g_mix_b = lw["g_mix"][W_A:W_A + W_B]
    gm = (lw["g_bv"], lw["b_bv"], lw["w_s"], lw["b_s"], g_mix_b)
    b_p, _ = _gmlp(z, N_PROMPT // CHUNK, 3 * W_A // W_B, 3 * W_A // W_B + 1, *gm)
    uv_s = z[N_PROMPT:N_REAL, 3 * W_A:3 * W_A + 2 * W_B].reshape(B_S, T_S, 2 * W_B)
    uv_s = jnp.pad(uv_s, ((0, 0), (0, CHUNK - T_S), (0, 0))).reshape(B_S * CHUNK, 2 * W_B)
    b_s, vn_s = _gmlp(uv_s, B_S, 0, 1, *gm)
    b_s = b_s.reshape(B_S, CHUNK, W_B)[:, :T_S].reshape(N_SAMPLE, W_B)
    new_vb = vn_s.reshape(B_S, CHUNK, W_B)[:, :T_S]

    c_p, c_s, hr_p, hi_p, hr_s, hi_s = _s5_layer(z, lw, h0r_s, h0i_s)

    x = _matmul_mix_residual(a_p, b_p, c_p, _pad_rows(a_s, TM), _pad_rows(b_s, TM), _pad_rows(c_s, TM),
                             lw["g_mix"][:W_A], lw["w_out"].astype(BF16), x, g1, 1024)

    h2b, h2 = _modnorm(x, lw["g_ffn"], sc2, sh2)
    ind, wd, cnt = _router(h2b, lw["w_router"], lw["b_router"])
    be, misc = _plan(cnt)
    dest, w6 = _slots(ind, wd, cnt)
    dest_flat = dest[:, :8].reshape(-1)
    xs, shared = _dispatch(misc[0], misc[8], misc[16, N_EXPERTS - 1:], dest_flat, h2, h2b,
                           lw["w_sh_gate"], lw["w_sh_up"], lw["w_sh_down"])
    ys = _experts(l, be[:NB, 0], misc[16, N_EXPERTS - 1:], misc[16], xs, *experts)
    x = _combine(dest_flat, w6[:, :8].reshape(-1), ys, shared, x, g2)

    kv = lambda lo: (z[:N_PROMPT, lo:lo + W_A].reshape(B_P, T_P, H_A, HEAD_DIM)[:, T_P - BUF:],
                     z[N_PROMPT:N_REAL, lo:lo + W_A].reshape(B_S, T_S, H_A, HEAD_DIM))
    (k_p, k_s), (v_p, v_s) = kv(W_A), kv(2 * W_A)
    return x, (k_p, v_p, k_s, v_s, new_vb, hr_p, hi_p, hr_s, hi_s)


def kernel(x_prompt, x_sample, cache_a_k, cache_a_v, state_c_re, state_c_im, c_prompt, c_sample, rel_bias, w_ada, b_ada, g_attn, w_in, g_bv, b_bv, w_s, b_s, lam_re, lam_im, log_dt, ssm_b_re, ssm_b_im, ssm_c_re, ssm_c_im, ssm_d, w_glu, b_glu, g_mix, w_out, g_ffn, w_router, b_router, w_gate, w_up, w_down, w_sh_gate, w_sh_up, w_sh_down, g_final):
    assert x_prompt.shape == (B_P, T_P, D) and x_sample.shape == (B_S, T_S, D)
    assert cache_a_k.shape == (DEPTH, B_S, BUF, H_A, HEAD_DIM)
    per_layer = dict(g_attn=g_attn, w_in=w_in, g_bv=g_bv, b_bv=b_bv, w_s=w_s, b_s=b_s, lam_re=lam_re,
                     lam_im=lam_im, log_dt=log_dt, ssm_b_re=ssm_b_re, ssm_b_im=ssm_b_im, ssm_c_re=ssm_c_re,
                     ssm_c_im=ssm_c_im, ssm_d=ssm_d, w_glu=w_glu, b_glu=b_glu, g_mix=g_mix, w_out=w_out,
                     g_ffn=g_ffn, w_router=w_router, b_router=b_router, w_sh_gate=w_sh_gate, w_sh_up=w_sh_up,
                     w_sh_down=w_sh_down)

    x = _pad_rows(jnp.concatenate([x_prompt.reshape(N_PROMPT, D), x_sample.reshape(N_SAMPLE, D)], axis=0), NP)
    c_all = _pad_rows(jnp.concatenate([c_prompt, c_sample], axis=0), 16)
    mod = _mod_all(c_all, w_ada, b_ada)
    bias_p = _bias_bank(rel_bias, *_prompt_bias_tables())
    bias_s = _bias_bank(rel_bias, *_sample_bias_tables())[0]
    cache_k = cache_a_k.reshape(DEPTH * B_S, BUF, W_A)
    cache_v = cache_a_v.reshape(DEPTH * B_S, BUF, W_A)

    outs = []
    for l in range(DEPTH):
        lw = {name: w[l] for name, w in per_layer.items()}
        x, o = _layer(x, l, lw, (w_gate, w_up, w_down), mod[l], bias_p, bias_s, cache_k, cache_v,
                      state_c_re[l], state_c_im[l])
        outs.append(o)
    y_p, y_s = _finalnorm(x, g_final)
    stacked = [jnp.stack([o[j] for o in outs]) for j in range(9)]
    return (y_p.reshape(B_P, T_P, D), y_s[:N_SAMPLE].reshape(B_S, T_S, D), *stacked)
```
